```python
import jax, jax.numpy as jnp
from jax import lax
import numpy as np

D_MODEL = 1024
BATCH = 2
SEQ = 8192
DEPTH = 4

CHUNK = 64
QBLOCK = 128
PLE_DIM = 256
EPS = 1e-6
NEG_INF = -1e30
MLA_HEADS = 8
MLA_NOPE = 64
MLA_ROPE = 32
MLA_V = 64
MLA_Q_LORA = 384
MLA_KV_LORA = 256
ROPE_THETA = 10000.0
CHK_HEADS = 8
CHK_DIM = 64
LEFT_CHUNKS = 8
REL_CLIP = 256
FOX_HEADS = 16
FOX_DIM = 64
MLA_WIDTH = MLA_HEADS * MLA_V
CHK_WIDTH = CHK_HEADS * CHK_DIM
FOX_WIDTH = FOX_HEADS * FOX_DIM
AB_SIZES = (MLA_Q_LORA, MLA_KV_LORA, MLA_ROPE, MLA_WIDTH, CHK_WIDTH, CHK_WIDTH, CHK_WIDTH, CHK_WIDTH)
AB_IN = sum(AB_SIZES)
FOX_SIZES = (FOX_WIDTH, FOX_WIDTH, FOX_WIDTH, FOX_WIDTH, FOX_HEADS)
FOX_IN = sum(FOX_SIZES)
N_AB = (DEPTH + 1) // 2
N_FOX = DEPTH // 2

kernel_name = 'hybrid_mla_chunkrel_fox_trunk'


def rms_norm(x, g):
    xf = x.astype(jnp.float32)
    y = xf * lax.rsqrt(jnp.mean(xf * xf, axis=-1, keepdims=True) + EPS)
    return (y * g.astype(jnp.float32)).astype(x.dtype)


def split_cols(z, sizes):
    return jnp.split(z, np.cumsum(sizes)[:-1].tolist(), axis=-1)


def apply_rope(x, positions):
    half = x.shape[-1] // 2
    inv_freq = 1.0 / (ROPE_THETA ** (jnp.arange(half, dtype=jnp.float32) / half))
    ang = positions.astype(jnp.float32)[:, :, None, None] * inv_freq
    cos, sin = jnp.cos(ang), jnp.sin(ang)
    xf = x.astype(jnp.float32)
    x1, x2 = xf[..., :half], xf[..., half:]
    return jnp.concatenate([x1 * cos - x2 * sin, x2 * cos + x1 * sin], axis=-1).astype(x.dtype)


def blocked_attention(q, k, v, scale, logit_fn):
    B, S, H, Dk = q.shape
    nb = S // QBLOCK
    qb = q.reshape(B, nb, QBLOCK, H, Dk).transpose(1, 0, 2, 3, 4)

    def one_block(args):
        q_i, b_i = args
        s = jnp.einsum('bqhd,bkhd->bhqk', q_i, k, preferred_element_type=jnp.float32) * scale
        s = logit_fn(s, b_i)
        pr = jax.nn.softmax(s, axis=-1).astype(v.dtype)
        return jnp.einsum('bhqk,bkhd->bqhd', pr, v)

    out = lax.map(one_block, (qb, jnp.arange(nb)))
    return out.transpose(1, 0, 2, 3, 4).reshape(B, S, H, v.shape[-1])


def mla_attention(cq, ckv, kr, positions, q_norm, w_uq, kv_norm, w_ukv, q_gain, k_gain):
    B, S, _ = cq.shape
    q = (rms_norm(cq, q_norm) @ w_uq).reshape(B, S, MLA_HEADS, MLA_NOPE + MLA_ROPE)
    kv = (rms_norm(ckv, kv_norm) @ w_ukv).reshape(B, S, MLA_HEADS, MLA_NOPE + MLA_V)
    k_nope, v = kv[..., :MLA_NOPE], kv[..., MLA_NOPE:]
    k_rope = jnp.broadcast_to(kr[:, :, None, :], (B, S, MLA_HEADS, MLA_ROPE))
    k = jnp.concatenate([k_nope, k_rope], axis=-1)
    q = rms_norm(q, q_gain)
    k = rms_norm(k, k_gain)
    q = jnp.concatenate([q[..., :MLA_NOPE], apply_rope(q[..., MLA_NOPE:], positions)], axis=-1)
    k = jnp.concatenate([k[..., :MLA_NOPE], apply_rope(k[..., MLA_NOPE:], positions)], axis=-1)
    key_chunk = jnp.arange(S) // CHUNK

    def chunk_causal(s, b_i):
        q_chunk = (b_i * QBLOCK + jnp.arange(QBLOCK)) // CHUNK
        mask = key_chunk[None, :] <= q_chunk[:, None]
        return jnp.where(mask[None, None], s, NEG_INF)

    o = blocked_attention(q, k, v, (MLA_NOPE + MLA_ROPE) ** -0.5, chunk_causal)
    return o.reshape(B, S, MLA_WIDTH)


def chunk_rel_attention(q, k, v, rel_bias):
    B, S, H, Dh = q.shape
    nC = S // CHUNK
    band = (LEFT_CHUNKS + 1) * CHUNK
    qc = q.reshape(B, nC, CHUNK, H, Dh)
    pad = ((0, 0), (LEFT_CHUNKS, 0), (0, 0), (0, 0), (0, 0))
    kp = jnp.pad(k.reshape(B, nC, CHUNK, H, Dh), pad)
    vp = jnp.pad(v.reshape(B, nC, CHUNK, H, Dh), pad)
    cidx = jnp.arange(nC)[:, None] + jnp.arange(LEFT_CHUNKS + 1)[None, :]
    kb = kp[:, cidx].reshape(B, nC, band, H, Dh)
    vb = vp[:, cidx].reshape(B, nC, band, H, Dh)
    s = jnp.einsum('bcqhd,bckhd->bhcqk', qc, kb, preferred_element_type=jnp.float32) * (Dh ** -0.5)
    dist = LEFT_CHUNKS * CHUNK + jnp.arange(CHUNK)[:, None] - jnp.arange(band)[None, :]
    ridx = jnp.clip(dist, -REL_CLIP, REL_CLIP) + REL_CLIP
    bias = rel_bias.astype(jnp.float32)[:, ridx]
    valid = jnp.repeat(cidx >= LEFT_CHUNKS, CHUNK, axis=1)
    s = jnp.where(valid[None, None, :, None, :], s + bias[None, :, None], NEG_INF)
    pr = jax.nn.softmax(s, axis=-1).astype(v.dtype)
    o = jnp.einsum('bhcqk,bckhd->bcqhd', pr, vb)
    return o.reshape(B, S, H * Dh)


def forgetting_attention(q, k, v, f_logit, b_f):
    B, S, H, Dh = q.shape
    log_f = jax.nn.log_sigmoid(f_logit.astype(jnp.float32) + b_f.astype(jnp.float32))
    cum = lax.cumsum(log_f, axis=1).transpose(0, 2, 1)
    kpos = jnp.arange(S)

    def decay_causal(s, b_i):
        cq = lax.dynamic_slice_in_dim(cum, b_i * QBLOCK, QBLOCK, axis=2)
        qpos = b_i * QBLOCK + jnp.arange(QBLOCK)
        mask = kpos[None, :] <= qpos[:, None]
        return jnp.where(mask[None, None], s + cq[..., :, None] - cum[..., None, :], NEG_INF)

    o = blocked_attention(q, k, v, Dh ** -0.5, decay_causal)
    return o.reshape(B, S, H * Dh)


def setup_inputs(seed: int = 0) -> dict:
    key = jax.random.key(seed)
    ks = jax.random.split(key, 24)

    def nrm(k, shape, scale):
        return jax.random.normal(k, shape, jnp.float32) * scale

    def gain(k, shape):
        return 1.0 + nrm(k, shape, 0.05)

    offsets = jax.random.randint(ks[2], (BATCH, 1), 0, 4096, dtype=jnp.int32)
    positions = offsets + jnp.arange(SEQ, dtype=jnp.int32)[None, :]
    return {
        'x': nrm(ks[0], (BATCH, SEQ, D_MODEL), 1.0),
        'p': nrm(ks[1], (DEPTH, BATCH, SEQ, PLE_DIM), 1.0),
        'positions': positions,
        'norm_g': gain(ks[3], (DEPTH, D_MODEL)),
        'ab_w_in': nrm(ks[4], (N_AB, D_MODEL, AB_IN), D_MODEL ** -0.5),
        'mla_q_norm': gain(ks[5], (N_AB, MLA_Q_LORA)),
        'mla_w_uq': nrm(ks[6], (N_AB, MLA_Q_LORA, MLA_HEADS * (MLA_NOPE + MLA_ROPE)), MLA_Q_LORA ** -0.5),
        'mla_kv_norm': gain(ks[7], (N_AB, MLA_KV_LORA)),
        'mla_w_ukv': nrm(ks[8], (N_AB, MLA_KV_LORA, MLA_HEADS * (MLA_NOPE + MLA_V)), MLA_KV_LORA ** -0.5),
        'mla_q_gain': gain(ks[9], (N_AB, MLA_NOPE + MLA_ROPE)),
        'mla_k_gain': gain(ks[10], (N_AB, MLA_NOPE + MLA_ROPE)),
        'chk_q_gain': gain(ks[11], (N_AB, CHK_DIM)),
        'chk_k_gain': gain(ks[12], (N_AB, CHK_DIM)),
        'chk_rel_bias': nrm(ks[13], (N_AB, CHK_HEADS, 2 * REL_CLIP + 1), 0.1),
        'ab_w_out': nrm(ks[14], (N_AB, MLA_WIDTH + CHK_WIDTH, D_MODEL), (MLA_WIDTH + CHK_WIDTH) ** -0.5),
        'fox_w_in': nrm(ks[15], (N_FOX, D_MODEL, FOX_IN), D_MODEL ** -0.5),
        'fox_b_f': 2.0 + nrm(ks[16], (N_FOX, FOX_HEADS), 0.1),
        'fox_q_gain': gain(ks[17], (N_FOX, FOX_DIM)),
        'fox_k_gain': gain(ks[18], (N_FOX, FOX_DIM)),
        'fox_w_out': nrm(ks[19], (N_FOX, FOX_WIDTH, D_MODEL), FOX_WIDTH ** -0.5),
        'pe_w': nrm(ks[20], (DEPTH, PLE_DIM, D_MODEL), PLE_DIM ** -0.5),
        'pe_gate_norm': gain(ks[21], (DEPTH, D_MODEL)),
        'pe_gate_w': nrm(ks[22], (DEPTH, D_MODEL, D_MODEL), D_MODEL ** -0.5),
    }


def reference(x, p, positions, norm_g, ab_w_in, mla_q_norm, mla_w_uq, mla_kv_norm, mla_w_ukv,
              mla_q_gain, mla_k_gain, chk_q_gain, chk_k_gain, chk_rel_bias, ab_w_out,
              fox_w_in, fox_b_f, fox_q_gain, fox_k_gain, fox_w_out, pe_w, pe_gate_norm, pe_gate_w):
    B, S, _ = x.shape
    h = x
    for i in range(DEPTH):
        u = rms_norm(h, norm_g[i])
        l = i // 2
        if i % 2 == 0:
            cq, ckv, kr, g_a, q_b, k_b, v_b, g_b = split_cols(u @ ab_w_in[l], AB_SIZES)
            o_a = mla_attention(cq, ckv, kr, positions, mla_q_norm[l], mla_w_uq[l],
                                mla_kv_norm[l], mla_w_ukv[l], mla_q_gain[l], mla_k_gain[l])
            q_b = rms_norm(q_b.reshape(B, S, CHK_HEADS, CHK_DIM), chk_q_gain[l])
            k_b = rms_norm(k_b.reshape(B, S, CHK_HEADS, CHK_DIM), chk_k_gain[l])
            v_b = v_b.reshape(B, S, CHK_HEADS, CHK_DIM)
            o_b = chunk_rel_attention(q_b, k_b, v_b, chk_rel_bias[l])
            mixed = jnp.concatenate([o_a * jax.nn.silu(g_a), o_b * jax.nn.silu(g_b)], axis=-1) @ ab_w_out[l]
        else:
            q_c, k_c, v_c, g_c, f_c = split_cols(u @ fox_w_in[l], FOX_SIZES)
            q_c = rms_norm(q_c.reshape(B, S, FOX_HEADS, FOX_DIM), fox_q_gain[l])
            k_c = rms_norm(k_c.reshape(B, S, FOX_HEADS, FOX_DIM), fox_k_gain[l])
            v_c = v_c.reshape(B, S, FOX_HEADS, FOX_DIM)
            o_c = forgetting_attention(q_c, k_c, v_c, f_c, fox_b_f[l])
            mixed = (o_c * jax.nn.silu(g_c)) @ fox_w_out[l]
        h = h + mixed
        ple_gate = jax.nn.sigmoid(rms_norm(h, pe_gate_norm[i]) @ pe_gate_w[i])
        h = h + (p[i] @ pe_w[i]) * ple_gate
    return h
```

```python
import functools

import numpy as np
import jax
import jax.numpy as jnp
from jax import lax
from jax.experimental import pallas as pl
from jax.experimental.pallas import tpu as pltpu

F32 = jnp.float32
BF16 = jnp.bfloat16

D_MODEL = 1024
DEPTH = 4
CHUNK = 64
PLE_DIM = 256
EPS = 1e-6
NEG_INF = -1e30
MLA_HEADS = 8
MLA_NOPE = 64
MLA_ROPE = 32
MLA_QK = MLA_NOPE + MLA_ROPE
MLA_V = 64
MLA_Q_LORA = 384
MLA_KV_LORA = 256
ROPE_THETA = 10000.0
CHK_HEADS = 8
CHK_DIM = 64
LEFT_CHUNKS = 8
REL_CLIP = 256
FOX_HEADS = 16
FOX_DIM = 64
MLA_WIDTH = MLA_HEADS * MLA_V
CHK_WIDTH = CHK_HEADS * CHK_DIM
FOX_WIDTH = FOX_HEADS * FOX_DIM

TM = 512
TQ = 512
TK = 512
TC = 256
QK_PAD = 128
VMEM_LIMIT = 56 * 1024 * 1024


def _nt(a, b):
    return lax.dot_general(a, b, (((1,), (1,)), ((), ())), preferred_element_type=F32)


def _nn(a, b):
    return jnp.dot(a, b, preferred_element_type=F32)


def _sigmoid(x):
    return 1.0 / (1.0 + jnp.exp(-x))


def _rms_rows(x, g):
    ms = jnp.mean(x * x, axis=-1, keepdims=True)
    return x * lax.rsqrt(ms + EPS) * g


def _split3(x):
    hi = x.astype(BF16)
    r1 = x - hi.astype(F32)
    mid = r1.astype(BF16)
    lo = (r1 - mid.astype(F32)).astype(BF16)
    return hi, mid, lo


def _rope_t(x, cos, sin):
    half = x.shape[0] // 2
    x1, x2 = x[:half], x[half:]
    return jnp.concatenate([x1 * cos - x2 * sin, x2 * cos + x1 * sin], axis=0)


def _fox_proj_kernel(h_ref, ng_ref, wq_ref, wk_ref, wv_ref, wg_ref, wf_ref, bf_ref, qg_ref, kg_ref,
                     tri_ref, qa_ref, ka_ref, vt_ref, gt_ref, carry_ref):
    t = pl.program_id(1)
    u = _rms_rows(h_ref[0], ng_ref[...]).astype(BF16)

    z = _nt(wf_ref[...], u) + bf_ref[...]
    logf = jnp.minimum(z, 0.0) - jnp.log(1.0 + jnp.exp(-jnp.abs(z)))
    parts = jnp.concatenate(_split3(logf), axis=0)
    c3 = _nn(parts, tri_ref[...])
    local = c3[0:FOX_HEADS] + c3[FOX_HEADS:2 * FOX_HEADS] + c3[2 * FOX_HEADS:3 * FOX_HEADS]

    @pl.when(t == 0)
    def _():
        carry_ref[...] = jnp.zeros_like(carry_ref)

    cum = local + carry_ref[:, 0:1]
    carry_ref[...] = jnp.broadcast_to(cum[:, TM - 1:TM], carry_ref.shape)
    c_hi, c_mid, c_lo = [p.astype(F32) for p in _split3(cum)]

    row = lax.broadcasted_iota(jnp.int32, (16, TM), 0)
    zeros48 = jnp.zeros((QK_PAD - FOX_DIM - 16, TM), F32)
    qg = qg_ref[...]
    kg = kg_ref[...]
    n_chunk = FOX_WIDTH // 256
    for c in range(n_chunk):
        rows = slice(c * 256, (c + 1) * 256)
        qc = _nt(wq_ref[rows, :], u)
        kc = _nt(wk_ref[rows, :], u)
        vc = _nt(wv_ref[rows, :], u)
        gc = _nt(wg_ref[rows, :], u)
        gt_ref[0, rows, :] = gc.astype(BF16)
        for j in range(4):
            hd = c * 4 + j
            hs = slice(j * FOX_DIM, (j + 1) * FOX_DIM)
            hi, mid, lo = c_hi[hd:hd + 1], c_mid[hd:hd + 1], c_lo[hd:hd + 1]
            qh = qc[hs]
            qn = qh * lax.rsqrt(jnp.mean(qh * qh, axis=0, keepdims=True) + EPS) * qg
            exq = jnp.where(row < 3, 1.0, jnp.where(row == 3, hi, jnp.where(row == 4, mid,
                            jnp.where(row == 5, lo, 0.0))))
            qa_ref[0, hd, 0:FOX_DIM, :] = qn.astype(BF16)
            qa_ref[0, hd, FOX_DIM:FOX_DIM + 16, :] = exq.astype(BF16)
            qa_ref[0, hd, FOX_DIM + 16:QK_PAD, :] = zeros48.astype(BF16)
            kh = kc[hs]
            kn = kh * lax.rsqrt(jnp.mean(kh * kh, axis=0, keepdims=True) + EPS) * kg
            exk = jnp.where(row == 0, -hi, jnp.where(row == 1, -mid, jnp.where(row == 2, -lo,
                            jnp.where(row < 6, 1.0, 0.0))))
            kfull = jnp.concatenate([kn, exk, zeros48], axis=0)
            ka_ref[0, hd, 0] = kfull.T.astype(BF16)
            vt_ref[0, hd, 0] = vc[hs].astype(BF16)


def _fox_proj(h, ng, wq, wk, wv, wg, wf, bfc, qg, kg, tri):
    B, S, _ = h.shape
    nt = S // TM
    full = lambda shape: pl.BlockSpec(shape, lambda b, t: (0,) * len(shape))
    return pl.pallas_call(
        _fox_proj_kernel,
        grid=(B, nt),
        in_specs=[
            pl.BlockSpec((1, TM, D_MODEL), lambda b, t: (b, t, 0)),
            full((1, D_MODEL)),
            full((FOX_WIDTH, D_MODEL)), full((FOX_WIDTH, D_MODEL)),
            full((FOX_WIDTH, D_MODEL)), full((FOX_WIDTH, D_MODEL)),
            full((FOX_HEADS, D_MODEL)), full((FOX_HEADS, 1)),
            full((FOX_DIM, 1)), full((FOX_DIM, 1)),
            full((TM, TM)),
        ],
        out_specs=[
            pl.BlockSpec((1, FOX_HEADS, QK_PAD, TM), lambda b, t: (b, 0, 0, t)),
            pl.BlockSpec((1, FOX_HEADS, 1, TM, QK_PAD), lambda b, t: (b, 0, t, 0, 0)),
            pl.BlockSpec((1, FOX_HEADS, 1, FOX_DIM, TM), lambda b, t: (b, 0, t, 0, 0)),
            pl.BlockSpec((1, FOX_WIDTH, TM), lambda b, t: (b, 0, t)),
        ],
        out_shape=[
            jax.ShapeDtypeStruct((B, FOX_HEADS, QK_PAD, S), BF16),
            jax.ShapeDtypeStruct((B, FOX_HEADS, nt, TM, QK_PAD), BF16),
            jax.ShapeDtypeStruct((B, FOX_HEADS, nt, FOX_DIM, TM), BF16),
            jax.ShapeDtypeStruct((B, FOX_WIDTH, S), BF16),
        ],
        scratch_shapes=[pltpu.VMEM((FOX_HEADS, 128), F32)],
        compiler_params=pltpu.CompilerParams(
            dimension_semantics=("parallel", "arbitrary"), vmem_limit_bytes=VMEM_LIMIT),
        name="fox_proj",
    )(h, ng, wq, wk, wv, wg, wf, bfc, qg, kg, tri)


def _ab_proj_kernel(h_ref, ng_ref, wcq_ref, wckv_ref, wkr_ref, wg_ref, wqb_ref, wkb_ref, wvb_ref,
                    qnorm_ref, wuq_ref, kvnorm_ref, wukv_ref, qg_ref, kg_ref, cqg_ref, ckg_ref,
                    cos_ref, sin_ref,
                    qt_ref, k_ref, vt_ref, gt_ref, qbt_ref, kb_ref, vbt_ref):
    u = _rms_rows(h_ref[0], ng_ref[...]).astype(BF16)
    cos = cos_ref[0]
    sin = sin_ref[0]

    cqn = _rms_rows(_nn(u, wcq_ref[...]), qnorm_ref[...]).astype(BF16)
    ckvn = _rms_rows(_nn(u, wckv_ref[...]), kvnorm_ref[...]).astype(BF16)
    krt = _nt(wkr_ref[...], u)
    ss_kr = jnp.sum(krt * krt, axis=0, keepdims=True)
    qg = qg_ref[...]
    kg = kg_ref[...]
    zeros32 = jnp.zeros((QK_PAD - MLA_QK, TM), F32)
    for hd in range(MLA_HEADS):
        qh = _nt(wuq_ref[hd * MLA_QK:(hd + 1) * MLA_QK, :], cqn)
        qn = qh * lax.rsqrt(jnp.mean(qh * qh, axis=0, keepdims=True) + EPS) * qg
        qt_ref[0, hd, 0:MLA_NOPE, :] = qn[0:MLA_NOPE].astype(BF16)
        qt_ref[0, hd, MLA_NOPE:MLA_QK, :] = _rope_t(qn[MLA_NOPE:MLA_QK], cos, sin).astype(BF16)
        qt_ref[0, hd, MLA_QK:QK_PAD, :] = zeros32.astype(BF16)
        kv = _nt(wukv_ref[hd * 128:(hd + 1) * 128, :], ckvn)
        kn = kv[0:MLA_NOPE]
        rk = lax.rsqrt((jnp.sum(kn * kn, axis=0, keepdims=True) + ss_kr) * (1.0 / MLA_QK) + EPS)
        kfull = jnp.concatenate([kn * rk * kg[0:MLA_NOPE],
                                 _rope_t(krt * rk * kg[MLA_NOPE:MLA_QK], cos, sin),
                                 zeros32], axis=0)
        k_ref[0, hd, 0] = kfull.T.astype(BF16)
        vt_ref[0, hd, 0] = kv[MLA_NOPE:MLA_NOPE + MLA_V].astype(BF16)

    for c in range((MLA_WIDTH + CHK_WIDTH) // 256):
        rows = slice(c * 256, (c + 1) * 256)
        gt_ref[0, rows, :] = _nt(wg_ref[rows, :], u).astype(BF16)

    cqg = cqg_ref[...]
    ckg = ckg_ref[...]
    zeros64 = jnp.zeros((CHK_DIM, TM), F32)
    for c in range(CHK_WIDTH // 256):
        rows = slice(c * 256, (c + 1) * 256)
        qc = _nt(wqb_ref[rows, :], u)
        kc = _nt(wkb_ref[rows, :], u)
        vbt_ref[0, rows, :] = _nt(wvb_ref[rows, :], u).astype(BF16)
        kns = []
        for j in range(4):
            hd = c * 4 + j
            hs = slice(j * CHK_DIM, (j + 1) * CHK_DIM)
            qh = qc[hs]
            qn = (qh * lax.rsqrt(jnp.mean(qh * qh, axis=0, keepdims=True) + EPS) * cqg).astype(BF16)
            lo, hi = (0, CHK_DIM) if hd % 2 == 0 else (CHK_DIM, 2 * CHK_DIM)
            qbt_ref[0, hd, lo:hi, :] = qn
            qbt_ref[0, hd, CHK_DIM - lo:2 * CHK_DIM - lo, :] = zeros64.astype(BF16)
            kh = kc[hs]
            kns.append(kh * lax.rsqrt(jnp.mean(kh * kh, axis=0, keepdims=True) + EPS) * ckg)
        for pr in range(2):
            pair = jnp.concatenate([kns[2 * pr], kns[2 * pr + 1]], axis=0)
            kb_ref[0, c * 2 + pr] = pair.T.astype(BF16)


def _ab_proj(h, ng, wcq, wckv, wkr, wg, wqb, wkb, wvb, qnorm, wuq, kvnorm, wukv, qg, kg, cqg, ckg,
             cos, sin):
    B, S, _ = h.shape
    nt = S // TM
    full = lambda shape: pl.BlockSpec(shape, lambda b, t: (0,) * len(shape))
    return pl.pallas_call(
        _ab_proj_kernel,
        grid=(B, nt),
        in_specs=[
            pl.BlockSpec((1, TM, D_MODEL), lambda b, t: (b, t, 0)),
            full((1, D_MODEL)),
            full((D_MODEL, MLA_Q_LORA)), full((D_MODEL, MLA_KV_LORA)), full((MLA_ROPE, D_MODEL)),
            full((MLA_WIDTH + CHK_WIDTH, D_MODEL)),
            full((CHK_WIDTH, D_MODEL)), full((CHK_WIDTH, D_MODEL)), full((CHK_WIDTH, D_MODEL)),
            full((1, MLA_Q_LORA)), full((MLA_HEADS * MLA_QK, MLA_Q_LORA)),
            full((1, MLA_KV_LORA)), full((MLA_HEADS * 128, MLA_KV_LORA)),
            full((MLA_QK, 1)), full((MLA_QK, 1)), full((CHK_DIM, 1)), full((CHK_DIM, 1)),
            pl.BlockSpec((1, MLA_ROPE // 2, TM), lambda b, t: (b, 0, t)),
            pl.BlockSpec((1, MLA_ROPE // 2, TM), lambda b, t: (b, 0, t)),
        ],
        out_specs=[
            pl.BlockSpec((1, MLA_HEADS, QK_PAD, TM), lambda b, t: (b, 0, 0, t)),
            pl.BlockSpec((1, MLA_HEADS, 1, TM, QK_PAD), lambda b, t: (b, 0, t, 0, 0)),
            pl.BlockSpec((1, MLA_HEADS, 1, MLA_V, TM), lambda b, t: (b, 0, t, 0, 0)),
            pl.BlockSpec((1, MLA_WIDTH + CHK_WIDTH, TM), lambda b, t: (b, 0, t)),
            pl.BlockSpec((1, CHK_HEADS, 2 * CHK_DIM, TM), lambda b, t: (b, 0, 0, t)),
            pl.BlockSpec((1, CHK_HEADS // 2, TM, 2 * CHK_DIM), lambda b, t: (b, 0, t, 0)),
            pl.BlockSpec((1, CHK_WIDTH, TM), lambda b, t: (b, 0, t)),
        ],
        out_shape=[
            jax.ShapeDtypeStruct((B, MLA_HEADS, QK_PAD, S), BF16),
            jax.ShapeDtypeStruct((B, MLA_HEADS, nt, TM, QK_PAD), BF16),
            jax.ShapeDtypeStruct((B, MLA_HEADS, nt, MLA_V, TM), BF16),
            jax.ShapeDtypeStruct((B, MLA_WIDTH + CHK_WIDTH, S), BF16),
            jax.ShapeDtypeStruct((B, CHK_HEADS, 2 * CHK_DIM, S), BF16),
            jax.ShapeDtypeStruct((B, CHK_HEADS // 2, S, 2 * CHK_DIM), BF16),
            jax.ShapeDtypeStruct((B, CHK_WIDTH, S), BF16),
        ],
        compiler_params=pltpu.CompilerParams(
            dimension_semantics=("parallel", "parallel"), vmem_limit_bytes=VMEM_LIMIT),
        name="ab_proj",
    )(h, ng, wcq, wckv, wkr, wg, wqb, wkb, wvb, qnorm, wuq, kvnorm, wukv, qg, kg, cqg, ckg, cos, sin)


def _causal_attn_kernel(qt_ref, k_ref, vt_ref, o_ref, *, gran_shift):
    qi = pl.program_id(2)
    qt = qt_ref[0, 0]
    dv = vt_ref.shape[3]

    def step(j, carry, masked):
        m, l, acc = carry
        s = _nn(k_ref[0, 0, j], qt)
        if masked:
            kpos = lax.broadcasted_iota(jnp.int32, (TK, TQ), 0)
            qpos = lax.broadcasted_iota(jnp.int32, (TK, TQ), 1)
            s = jnp.where((kpos >> gran_shift) <= (qpos >> gran_shift), s, NEG_INF)
        m_new = jnp.maximum(m, jnp.max(s, axis=0, keepdims=True))
        p = jnp.exp(s - m_new)
        alpha = jnp.exp(m - m_new)
        l = alpha * l + jnp.sum(p, axis=0, keepdims=True)
        acc = alpha * acc + _nn(vt_ref[0, 0, j], p.astype(BF16))
        return m_new, l, acc

    init = (jnp.full((1, TQ), NEG_INF, F32), jnp.zeros((1, TQ), F32), jnp.zeros((dv, TQ), F32))
    carry = lax.fori_loop(0, qi, lambda j, c: step(j, c, False), init)
    m, l, acc = step(qi, carry, True)
    o_ref[0] = (acc / l).astype(o_ref.dtype)


def _causal_attn(qt, k, vt, gran):
    B, H, _, S = qt.shape
    nk, dv = k.shape[2], vt.shape[3]
    assert TQ == TK
    return pl.pallas_call(
        functools.partial(_causal_attn_kernel, gran_shift=int(np.log2(gran))),
        grid=(B, H, S // TQ),
        in_specs=[
            pl.BlockSpec((1, 1, QK_PAD, TQ), lambda b, h, q: (b, h, 0, q)),
            pl.BlockSpec((1, 1, nk, TK, QK_PAD), lambda b, h, q: (b, h, 0, 0, 0)),
            pl.BlockSpec((1, 1, nk, dv, TK), lambda b, h, q: (b, h, 0, 0, 0)),
        ],
        out_specs=pl.BlockSpec((1, dv, TQ), lambda b, h, q: (b, h, q)),
        out_shape=jax.ShapeDtypeStruct((B, H * dv, S), BF16),
        compiler_params=pltpu.CompilerParams(
            dimension_semantics=("parallel", "parallel", "arbitrary"), vmem_limit_bytes=VMEM_LIMIT),
        name="causal_attn",
    )(qt, k, vt)


def _chunk_attn_kernel(qt_ref, k0_ref, k1_ref, k2_ref, v0_ref, v1_ref, v2_ref, bias_ref, o_ref):
    t = pl.program_id(1)
    k_refs = (k0_ref, k1_ref, k2_ref)
    v_refs = (v0_ref, v1_ref, v2_ref)
    for hd in range(CHK_HEADS):
        qt = qt_ref[0, hd]
        ss = []
        for d in range(3):
            s = _nn(k_refs[d][0, hd // 2], qt) + bias_ref[hd, d * TC:(d + 1) * TC, :]
            if d < 2:
                s = jnp.where(t + (d - 2) >= 0, s, NEG_INF)
            ss.append(s)
        m = jnp.max(jnp.maximum(jnp.maximum(ss[0], ss[1]), ss[2]), axis=0, keepdims=True)
        ps = [jnp.exp(s - m) for s in ss]
        l = jnp.sum(ps[0] + ps[1] + ps[2], axis=0, keepdims=True)
        hs = slice(hd * CHK_DIM, (hd + 1) * CHK_DIM)
        acc = _nn(v_refs[0][0, hs, :], ps[0].astype(BF16))
        acc += _nn(v_refs[1][0, hs, :], ps[1].astype(BF16))
        acc += _nn(v_refs[2][0, hs, :], ps[2].astype(BF16))
        o_ref[0, hs, :] = (acc / l).astype(o_ref.dtype)


def _chunk_attn(qbt, kb, vbt, bias_t):
    B, _, _, S = qbt.shape
    kspec = lambda d: pl.BlockSpec((1, CHK_HEADS // 2, TC, 2 * CHK_DIM),
                                   lambda b, t: (b, 0, jnp.maximum(t + (d - 2), 0), 0))
    vspec = lambda d: pl.BlockSpec((1, CHK_WIDTH, TC),
                                   lambda b, t: (b, 0, jnp.maximum(t + (d - 2), 0)))
    return pl.pallas_call(
        _chunk_attn_kernel,
        grid=(B, S // TC),
        in_specs=[
            pl.BlockSpec((1, CHK_HEADS, 2 * CHK_DIM, TC), lambda b, t: (b, 0, 0, t)),
            kspec(0), kspec(1), kspec(2), vspec(0), vspec(1), vspec(2),
            pl.BlockSpec((CHK_HEADS, 3 * TC, TC), lambda b, t: (0, 0, 0)),
        ],
        out_specs=pl.BlockSpec((1, CHK_WIDTH, TC), lambda b, t: (b, 0, t)),
        out_shape=jax.ShapeDtypeStruct((B, CHK_WIDTH, S), BF16),
        compiler_params=pltpu.CompilerParams(
            dimension_semantics=("parallel", "parallel"), vmem_limit_bytes=VMEM_LIMIT),
        name="chunk_attn",
    )(qbt, kb, kb, kb, vbt, vbt, vbt, bias_t)


def _out_ple_kernel(*refs, n_o):
    o_refs = refs[:n_o]
    gt_ref, h_ref, p_ref, wo_ref, pgn_ref, wgate_ref, pew_ref, out_ref = refs[n_o:]
    ot = jnp.concatenate([r[0] for r in o_refs], axis=0) if n_o > 1 else o_refs[0][0]
    g = gt_ref[0].astype(F32)
    og = (ot.astype(F32) * (g * _sigmoid(g))).astype(BF16)
    mixed_t = _nn(wo_ref[...], og)
    h1 = h_ref[0] + mixed_t.T
    a = _rms_rows(h1, pgn_ref[...]).astype(BF16)
    gate = _sigmoid(_nn(a, wgate_ref[...]))
    pe = _nn(p_ref[0, 0].astype(BF16), pew_ref[...])
    out_ref[0] = h1 + pe * gate


def _out_ple(o_list, gt, h, p, layer, wo_t, pgn, wgate, pew):
    B, S, _ = h.shape
    full = lambda shape: pl.BlockSpec(shape, lambda b, t: (0,) * len(shape))
    width = gt.shape[1]
    return pl.pallas_call(
        functools.partial(_out_ple_kernel, n_o=len(o_list)),
        grid=(B, S // TM),
        in_specs=[pl.BlockSpec((1, o.shape[1], TM), lambda b, t: (b, 0, t)) for o in o_list] + [
            pl.BlockSpec((1, width, TM), lambda b, t: (b, 0, t)),
            pl.BlockSpec((1, TM, D_MODEL), lambda b, t: (b, t, 0)),
            pl.BlockSpec((1, 1, TM, PLE_DIM), lambda b, t: (layer, b, t, 0)),
            full((D_MODEL, width)), full((1, D_MODEL)), full((D_MODEL, D_MODEL)),
            full((PLE_DIM, D_MODEL)),
        ],
        out_specs=pl.BlockSpec((1, TM, D_MODEL), lambda b, t: (b, t, 0)),
        out_shape=jax.ShapeDtypeStruct((B, S, D_MODEL), F32),
        compiler_params=pltpu.CompilerParams(
            dimension_semantics=("parallel", "parallel"), vmem_limit_bytes=VMEM_LIMIT),
        name="out_ple",
    )(*o_list, gt, h, p, wo_t, pgn, wgate, pew)


def _col(v):
    return v.astype(F32).reshape(-1, 1)


def _chunk_bias_table(rel_bias):
    kk = np.arange(3 * TC)[:, None]
    qq = np.arange(TC)[None, :]
    dist = qq + 2 * TC - kk
    ridx = np.clip(dist, -REL_CLIP, REL_CLIP) + REL_CLIP
    qchunk = qq // CHUNK + (2 * TC) // CHUNK
    kchunk = kk // CHUNK
    valid = (kchunk <= qchunk) & (kchunk >= qchunk - LEFT_CHUNKS)
    return jnp.where(valid[None], rel_bias.astype(F32)[:, ridx], NEG_INF)


def kernel(x, p, positions, norm_g, ab_w_in, mla_q_norm, mla_w_uq, mla_kv_norm, mla_w_ukv, mla_q_gain, mla_k_gain, chk_q_gain, chk_k_gain, chk_rel_bias, ab_w_out, fox_w_in, fox_b_f, fox_q_gain, fox_k_gain, fox_w_out, pe_w, pe_gate_norm, pe_gate_w):
    B, S, _ = x.shape
    half = MLA_ROPE // 2
    inv_freq = 1.0 / (ROPE_THETA ** (jnp.arange(half, dtype=F32) / half))
    ang = positions.astype(F32)[:, None, :] * inv_freq[None, :, None]
    cos, sin = jnp.cos(ang), jnp.sin(ang)
    tri = (np.arange(TM)[:, None] <= np.arange(TM)[None, :]).astype(np.float32)
    tri = jnp.asarray(tri, BF16)

    h = x
    for i in range(DEPTH):
        l = i // 2
        ng = norm_g[i].astype(F32).reshape(1, -1)
        if i % 2 == 0:
            w = ab_w_in[l]
            o = np.cumsum((0, MLA_Q_LORA, MLA_KV_LORA, MLA_ROPE, MLA_WIDTH) + (CHK_WIDTH,) * 4)
            wcq = w[:, o[0]:o[1]].astype(BF16)
            wckv = w[:, o[1]:o[2]].astype(BF16)
            wkr = w[:, o[2]:o[3]].T.astype(BF16)
            wg = jnp.concatenate([w[:, o[3]:o[4]], w[:, o[7]:o[8]]], axis=1).T.astype(BF16)
            wqb = w[:, o[4]:o[5]].T.astype(BF16)
            wkb = w[:, o[5]:o[6]].T.astype(BF16)
            wvb = w[:, o[6]:o[7]].T.astype(BF16)
            qt, k, vt, gt, qbt, kb, vbt = _ab_proj(
                h, ng, wcq, wckv, wkr, wg, wqb, wkb, wvb,
                mla_q_norm[l].astype(F32).reshape(1, -1), mla_w_uq[l].T.astype(BF16),
                mla_kv_norm[l].astype(F32).reshape(1, -1), mla_w_ukv[l].T.astype(BF16),
                _col(mla_q_gain[l]) * (MLA_QK ** -0.5), _col(mla_k_gain[l]),
                _col(chk_q_gain[l]) * (CHK_DIM ** -0.5), _col(chk_k_gain[l]), cos, sin)
            o_a = _causal_attn(qt, k, vt, CHUNK)
            o_b = _chunk_attn(qbt, kb, vbt, _chunk_bias_table(chk_rel_bias[l]))
            o_list, wo = [o_a, o_b], ab_w_out[l]
        else:
            w = fox_w_in[l]
            o = np.cumsum((0,) + (FOX_WIDTH,) * 4 + (FOX_HEADS,))
            wq, wk, wv, wgc, wf = [w[:, o[n]:o[n + 1]].T.astype(BF16) for n in range(5)]
            qa, ka, vt, gt = _fox_proj(
                h, ng, wq, wk, wv, wgc, wf, _col(fox_b_f[l]),
                _col(fox_q_gain[l]) * (FOX_DIM ** -0.5), _col(fox_k_gain[l]), tri)
            o_list, wo = [_causal_attn(qa, ka, vt, 1)], fox_w_out[l]
        h = _out_ple(o_list, gt, h, p, i, wo.T.astype(BF16),
                     pe_gate_norm[i].astype(F32).reshape(1, -1),
                     pe_gate_w[i].astype(BF16), pe_w[i].astype(BF16))
    return h
```

```python
import functools

import numpy as np
import jax
import jax.numpy as jnp
from jax import lax
from jax.experimental import pallas as pl
from jax.experimental.pallas import tpu as pltpu

F32 = jnp.float32
BF16 = jnp.bfloat16

D_MODEL = 1024
DEPTH = 4
CHUNK = 64
PLE_DIM = 256
EPS = 1e-6
NEG_INF = -1e30
MLA_HEADS = 8
MLA_NOPE = 64
MLA_ROPE = 32
MLA_QK = MLA_NOPE + MLA_ROPE
MLA_V = 64
MLA_Q_LORA = 384
MLA_KV_LORA = 256
ROPE_THETA = 10000.0
CHK_HEADS = 8
CHK_DIM = 64
LEFT_CHUNKS = 8
REL_CLIP = 256
FOX_HEADS = 16
FOX_DIM = 64
MLA_WIDTH = MLA_HEADS * MLA_V
CHK_WIDTH = CHK_HEADS * CHK_DIM
FOX_WIDTH = FOX_HEADS * FOX_DIM

TM = 512
TQ = 512
TQF = 1024
MAX_SHIFT_BOUND = 45.0
TK = 256
TC = 256
QK_PAD = 128
V_PAD = 80
LOOKAHEAD = 2
LOG2E = 1.4426950408889634
VMEM_LIMIT = 56 * 1024 * 1024


def _nt(a, b):
    return lax.dot_general(a, b, (((1,), (1,)), ((), ())), preferred_element_type=F32)


def _nn(a, b):
    return jnp.dot(a, b, preferred_element_type=F32)


def _sigmoid(x):
    return 1.0 / (1.0 + jnp.exp(-x))


def _rms_rows(x, g):
    ms = jnp.mean(x * x, axis=-1, keepdims=True)
    return x * lax.rsqrt(ms + EPS) * g


def _split3(x):
    hi = x.astype(BF16)
    r1 = x - hi.astype(F32)
    mid = r1.astype(BF16)
    lo = (r1 - mid.astype(F32)).astype(BF16)
    return hi, mid, lo


def _store_kv_blocks(k_ref, vt_ref, hd, k_t, v_t, ones_row):
    k_rows = k_t.T.astype(BF16)
    v_aug = jnp.concatenate([v_t.astype(BF16), ones_row], axis=0)
    for c in range(TM // TK):
        k_ref[0, hd, c] = k_rows[c * TK:(c + 1) * TK]
        vt_ref[0, hd, c] = v_aug[:, c * TK:(c + 1) * TK]


def _rope_t(x, cos, sin):
    half = x.shape[0] // 2
    x1, x2 = x[:half], x[half:]
    return jnp.concatenate([x1 * cos - x2 * sin, x2 * cos + x1 * sin], axis=0)


def _fox_proj_kernel(h_ref, ng_ref, wq_ref, wk_ref, wv_ref, wg_ref, wf_ref, bf_ref, qg_ref, kg_ref,
                     tri_ref, bound_ref, qa_ref, ka_ref, vt_ref, gt_ref, carry_ref):
    t = pl.program_id(1)
    u = _rms_rows(h_ref[0], ng_ref[...]).astype(BF16)

    z = _nt(wf_ref[...], u) + bf_ref[...]
    logf = jnp.minimum(z, 0.0) - jnp.log(1.0 + jnp.exp(-jnp.abs(z)))
    parts = jnp.concatenate(_split3(logf), axis=0)
    c3 = _nn(parts, tri_ref[...])
    local = c3[0:FOX_HEADS] + c3[FOX_HEADS:2 * FOX_HEADS] + c3[2 * FOX_HEADS:3 * FOX_HEADS]

    @pl.when(t == 0)
    def _():
        carry_ref[...] = jnp.zeros_like(carry_ref)

    cum = local + carry_ref[:, 0:1]
    carry_ref[...] = jnp.broadcast_to(cum[:, TM - 1:TM], carry_ref.shape)
    c_hi, c_mid, c_lo = [p.astype(F32) for p in _split3(cum * LOG2E)]
    d_hi, d_mid, d_lo = [p.astype(F32) for p in _split3(cum * LOG2E + bound_ref[...])]

    row = lax.broadcasted_iota(jnp.int32, (16, TM), 0)
    ones_row = jnp.where(row == 0, 1.0, 0.0).astype(BF16)
    zeros48 = jnp.zeros((QK_PAD - FOX_DIM - 16, TM), F32)
    qg = qg_ref[...]
    kg = kg_ref[...]
    n_chunk = FOX_WIDTH // 256
    for c in range(n_chunk):
        rows = slice(c * 256, (c + 1) * 256)
        qc = _nt(wq_ref[rows, :], u)
        kc = _nt(wk_ref[rows, :], u)
        vc = _nt(wv_ref[rows, :], u)
        gc = _nt(wg_ref[rows, :], u)
        gt_ref[0, rows, :] = gc.astype(BF16)
        for j in range(4):
            hd = c * 4 + j
            hs = slice(j * FOX_DIM, (j + 1) * FOX_DIM)
            hi, mid, lo = c_hi[hd:hd + 1], c_mid[hd:hd + 1], c_lo[hd:hd + 1]
            khi, kmid, klo = d_hi[hd:hd + 1], d_mid[hd:hd + 1], d_lo[hd:hd + 1]
            qh = qc[hs]
            qn = qh * lax.rsqrt(jnp.mean(qh * qh, axis=0, keepdims=True) + EPS) * qg
            exq = jnp.where(row < 3, 1.0, jnp.where(row == 3, hi, jnp.where(row == 4, mid,
                            jnp.where(row == 5, lo, 0.0))))
            qa_ref[0, hd, 0:FOX_DIM, :] = qn.astype(BF16)
            qa_ref[0, hd, FOX_DIM:FOX_DIM + 16, :] = exq.astype(BF16)
            qa_ref[0, hd, FOX_DIM + 16:QK_PAD, :] = zeros48.astype(BF16)
            kh = kc[hs]
            kn = kh * lax.rsqrt(jnp.mean(kh * kh, axis=0, keepdims=True) + EPS) * kg
            exk = jnp.where(row == 0, -khi, jnp.where(row == 1, -kmid, jnp.where(row == 2, -klo,
                            jnp.where(row < 6, 1.0, 0.0))))
            kfull = jnp.concatenate([kn, exk, zeros48], axis=0)
            _store_kv_blocks(ka_ref, vt_ref, hd, kfull, vc[hs], ones_row)


def _fox_proj(h, ng, wq, wk, wv, wg, wf, bfc, qg, kg, tri, bound):
    B, S, _ = h.shape
    nt = S // TM
    full = lambda shape: pl.BlockSpec(shape, lambda b, t: (0,) * len(shape))
    return pl.pallas_call(
        _fox_proj_kernel,
        grid=(B, nt),
        in_specs=[
            pl.BlockSpec((1, TM, D_MODEL), lambda b, t: (b, t, 0)),
            full((1, D_MODEL)),
            full((FOX_WIDTH, D_MODEL)), full((FOX_WIDTH, D_MODEL)),
            full((FOX_WIDTH, D_MODEL)), full((FOX_WIDTH, D_MODEL)),
            full((FOX_HEADS, D_MODEL)), full((FOX_HEADS, 1)),
            full((FOX_DIM, 1)), full((FOX_DIM, 1)),
            full((TM, TM)), full((1, 1)),
        ],
        out_specs=[
            pl.BlockSpec((1, FOX_HEADS, QK_PAD, TM), lambda b, t: (b, 0, 0, t)),
            pl.BlockSpec((1, FOX_HEADS, TM // TK, TK, QK_PAD), lambda b, t: (b, 0, t, 0, 0)),
            pl.BlockSpec((1, FOX_HEADS, TM // TK, V_PAD, TK), lambda b, t: (b, 0, t, 0, 0)),
            pl.BlockSpec((1, FOX_WIDTH, TM), lambda b, t: (b, 0, t)),
        ],
        out_shape=[
            jax.ShapeDtypeStruct((B, FOX_HEADS, QK_PAD, S), BF16),
            jax.ShapeDtypeStruct((B, FOX_HEADS, S // TK, TK, QK_PAD), BF16),
            jax.ShapeDtypeStruct((B, FOX_HEADS, S // TK, V_PAD, TK), BF16),
            jax.ShapeDtypeStruct((B, FOX_WIDTH, S), BF16),
        ],
        scratch_shapes=[pltpu.VMEM((FOX_HEADS, 128), F32)],
        compiler_params=pltpu.CompilerParams(
            dimension_semantics=("parallel", "arbitrary"), vmem_limit_bytes=VMEM_LIMIT),
        name="fox_proj",
    )(h, ng, wq, wk, wv, wg, wf, bfc, qg, kg, tri, bound)


def _ab_proj_kernel(h_ref, ng_ref, wcq_ref, wckv_ref, wkr_ref, wg_ref, wqb_ref, wkb_ref, wvb_ref,
                    qnorm_ref, wuq_ref, kvnorm_ref, wukv_ref, qg_ref, kg_ref, cqg_ref, ckg_ref,
                    cos_ref, sin_ref, bound_ref,
                    qt_ref, k_ref, vt_ref, gt_ref, qbt_ref, kb_ref, vbt_ref):
    u = _rms_rows(h_ref[0], ng_ref[...]).astype(BF16)
    cos = cos_ref[0]
    sin = sin_ref[0]

    cqn = _rms_rows(_nn(u, wcq_ref[...]), qnorm_ref[...]).astype(BF16)
    ckvn = _rms_rows(_nn(u, wckv_ref[...]), kvnorm_ref[...]).astype(BF16)
    krt = _nt(wkr_ref[...], u)
    ss_kr = jnp.sum(krt * krt, axis=0, keepdims=True)
    qg = qg_ref[...]
    kg = kg_ref[...]
    row32 = lax.broadcasted_iota(jnp.int32, (QK_PAD - MLA_QK, TM), 0)
    q_pad = jnp.where(row32 == 0, 1.0, 0.0)
    k_pad = jnp.where(row32 == 0, -bound_ref[...], 0.0)
    ones_row =jnp.where(lax.broadcasted_iota(jnp.int32, (V_PAD - MLA_V, TM), 0) == 0, 1.0, 0.0).astype(BF16)
    for hd in range(MLA_HEADS):
        qh = _nt(wuq_ref[hd * MLA_QK:(hd + 1) * MLA_QK, :], cqn)
        qn = qh * lax.rsqrt(jnp.mean(qh * qh, axis=0, keepdims=True) + EPS) * qg
        qt_ref[0, hd, 0:MLA_NOPE, :] = qn[0:MLA_NOPE].astype(BF16)
        qt_ref[0, hd, MLA_NOPE:MLA_QK, :] = _rope_t(qn[MLA_NOPE:MLA_QK], cos, sin).astype(BF16)
        qt_ref[0, hd, MLA_QK:QK_PAD, :] = q_pad.astype(BF16)
        kv = _nt(wukv_ref[hd * 128:(hd + 1) * 128, :], ckvn)
        kn = kv[0:MLA_NOPE]
        rk = lax.rsqrt((jnp.sum(kn * kn, axis=0, keepdims=True) + ss_kr) * (1.0 / MLA_QK) + EPS)
        kfull = jnp.concatenate([kn * rk * kg[0:MLA_NOPE],
                                 _rope_t(krt * rk * kg[MLA_NOPE:MLA_QK], cos, sin),
                                 k_pad], axis=0)
        _store_kv_blocks(k_ref, vt_ref, hd, kfull, kv[MLA_NOPE:MLA_NOPE + MLA_V], ones_row)

    for c in range((MLA_WIDTH + CHK_WIDTH) // 256):
        rows = slice(c * 256, (c + 1) * 256)
        gt_ref[0, rows, :] = _nt(wg_ref[rows, :], u).astype(BF16)

    cqg = cqg_ref[...]
    ckg = ckg_ref[...]
    zeros64 = jnp.zeros((CHK_DIM, TM), F32)
    for c in range(CHK_WIDTH // 256):
        rows = slice(c * 256, (c + 1) * 256)
        qc = _nt(wqb_ref[rows, :], u)
        kc = _nt(wkb_ref[rows, :], u)
        vbt_ref[0, rows, :] = _nt(wvb_ref[rows, :], u).astype(BF16)
        kns = []
        for j in range(4):
            hd = c * 4 + j
            hs = slice(j * CHK_DIM, (j + 1) * CHK_DIM)
            qh = qc[hs]
            qn = (qh * lax.rsqrt(jnp.mean(qh * qh, axis=0, keepdims=True) + EPS) * cqg).astype(BF16)
            lo, hi = (0, CHK_DIM) if hd % 2 == 0 else (CHK_DIM, 2 * CHK_DIM)
            qbt_ref[0, hd, lo:hi, :] = qn
            qbt_ref[0, hd, CHK_DIM - lo:2 * CHK_DIM - lo, :] = zeros64.astype(BF16)
            kh = kc[hs]
            kns.append(kh * lax.rsqrt(jnp.mean(kh * kh, axis=0, keepdims=True) + EPS) * ckg)
        for pr in range(2):
            pair = jnp.concatenate([kns[2 * pr], kns[2 * pr + 1]], axis=0)
            kb_ref[0, c * 2 + pr] = pair.T.astype(BF16)


def _ab_proj(h, ng, wcq, wckv, wkr, wg, wqb, wkb, wvb, qnorm, wuq, kvnorm, wukv, qg, kg, cqg, ckg,
             cos, sin, bound):
    B, S, _ = h.shape
    nt = S // TM
    full = lambda shape: pl.BlockSpec(shape, lambda b, t: (0,) * len(shape))
    return pl.pallas_call(
        _ab_proj_kernel,
        grid=(B, nt),
        in_specs=[
            pl.BlockSpec((1, TM, D_MODEL), lambda b, t: (b, t, 0)),
            full((1, D_MODEL)),
            full((D_MODEL, MLA_Q_LORA)), full((D_MODEL, MLA_KV_LORA)), full((MLA_ROPE, D_MODEL)),
            full((MLA_WIDTH + CHK_WIDTH, D_MODEL)),
            full((CHK_WIDTH, D_MODEL)), full((CHK_WIDTH, D_MODEL)), full((CHK_WIDTH, D_MODEL)),
            full((1, MLA_Q_LORA)), full((MLA_HEADS * MLA_QK, MLA_Q_LORA)),
            full((1, MLA_KV_LORA)), full((MLA_HEADS * 128, MLA_KV_LORA)),
            full((MLA_QK, 1)), full((MLA_QK, 1)), full((CHK_DIM, 1)), full((CHK_DIM, 1)),
            pl.BlockSpec((1, MLA_ROPE // 2, TM), lambda b, t: (b, 0, t)),
            pl.BlockSpec((1, MLA_ROPE // 2, TM), lambda b, t: (b, 0, t)),
            full((1, 1)),
        ],
        out_specs=[
            pl.BlockSpec((1, MLA_HEADS, QK_PAD, TM), lambda b, t: (b, 0, 0, t)),
            pl.BlockSpec((1, MLA_HEADS, TM // TK, TK, QK_PAD), lambda b, t: (b, 0, t, 0, 0)),
            pl.BlockSpec((1, MLA_HEADS, TM // TK, V_PAD, TK), lambda b, t: (b, 0, t, 0, 0)),
            pl.BlockSpec((1, MLA_WIDTH + CHK_WIDTH, TM), lambda b, t: (b, 0, t)),
            pl.BlockSpec((1, CHK_HEADS, 2 * CHK_DIM, TM), lambda b, t: (b, 0, 0, t)),
            pl.BlockSpec((1, CHK_HEADS // 2, TM, 2 * CHK_DIM), lambda b, t: (b, 0, t, 0)),
            pl.BlockSpec((1, CHK_WIDTH, TM), lambda b, t: (b, 0, t)),
        ],
        out_shape=[
            jax.ShapeDtypeStruct((B, MLA_HEADS, QK_PAD, S), BF16),
            jax.ShapeDtypeStruct((B, MLA_HEADS, S // TK, TK, QK_PAD), BF16),
            jax.ShapeDtypeStruct((B, MLA_HEADS, S // TK, V_PAD, TK), BF16),
            jax.ShapeDtypeStruct((B, MLA_WIDTH + CHK_WIDTH, S), BF16),
            jax.ShapeDtypeStruct((B, CHK_HEADS, 2 * CHK_DIM, S), BF16),
            jax.ShapeDtypeStruct((B, CHK_HEADS // 2, S, 2 * CHK_DIM), BF16),
            jax.ShapeDtypeStruct((B, CHK_WIDTH, S), BF16),
        ],
        compiler_params=pltpu.CompilerParams(
            dimension_semantics=("parallel", "parallel"), vmem_limit_bytes=VMEM_LIMIT),
        name="ab_proj",
    )(h, ng, wcq, wckv, wkr, wg, wqb, wkb, wvb, qnorm, wuq, kvnorm, wukv, qg, kg, cqg, ckg, cos, sin,
      bound)


def _causal_attn_kernel(qt_ref, k_ref, vt_ref, o_ref, *, gran_shift):
    qi = pl.program_id(2)
    dv = o_ref.shape[1]
    n_sub = TQ // TK
    qts = [qt_ref[0, 0, :, c * TK:(c + 1) * TK] for c in range(n_sub)]

    def scores(c, j, masked):
        s = _nn(k_ref[0, 0, j], qts[c])
        if masked:
            kpos = lax.broadcasted_iota(jnp.int32, (TK, TK), 0)
            qpos = lax.broadcasted_iota(jnp.int32, (TK, TK), 1)
            s = jnp.where((kpos >> gran_shift) <= (qpos >> gran_shift), s, NEG_INF)
        return s

    def update(s, j, carry):
        m, acc = carry
        m_new = jnp.maximum(m, jnp.max(s, axis=0, keepdims=True))
        p = jnp.exp2(s - m_new)
        acc = jnp.exp2(m - m_new) * acc + _nn(vt_ref[0, 0, j], p.astype(BF16))
        return m_new, acc

    def run(work, carries):
        carries = list(carries)
        pending = [scores(*w) for w in work[:LOOKAHEAD]]
        for i, (c, j, _) in enumerate(work):
            if i + LOOKAHEAD < len(work):
                pending.append(scores(*work[i + LOOKAHEAD]))
            carries[c] = update(pending.pop(0), j, carries[c])
        return tuple(carries)

    def steps(jj, carries):
        return run([(c, jj * n_sub + u, False) for u in range(n_sub) for c in range(n_sub)], carries)

    init = (jnp.full((1, TK), NEG_INF, F32), jnp.zeros((V_PAD, TK), F32))
    carries = lax.fori_loop(0, qi, steps, (init,) * n_sub)
    carries = run([(c, qi * n_sub + u, u == c) for c in range(n_sub) for u in range(c + 1)], carries)
    for c in range(n_sub):
        acc = carries[c][1]
        o_ref[0, :, c * TK:(c + 1) * TK] = (acc[0:dv] / acc[dv:dv + 1]).astype(o_ref.dtype)


def _shifted_attn_kernel(qt_ref, k_ref, vt_ref, o_ref, *, gran_shift):
    qi = pl.program_id(2)
    dv = o_ref.shape[1]
    nb = TQF // TK
    qt = qt_ref[0, 0]

    def probs(j, lo, masked):
        s = _nn(k_ref[0, 0, j], qt[:, lo:])
        if masked:
            kpos = lax.broadcasted_iota(jnp.int32, s.shape, 0)
            qpos = lax.broadcasted_iota(jnp.int32, s.shape, 1)
            s = jnp.where((kpos >> gran_shift) <= (qpos >> gran_shift), s, NEG_INF)
        return jnp.exp2(s).astype(BF16)

    def run(work, acc):
        pending = [probs(*w) for w in work[:LOOKAHEAD]]
        for i, (j, lo, _) in enumerate(work):
            if i + LOOKAHEAD < len(work):
                pending.append(probs(*work[i + LOOKAHEAD]))
            pv = _nn(vt_ref[0, 0, j], pending.pop(0))
            acc = acc + pv if lo == 0 else jnp.concatenate([acc[:, :lo], acc[:, lo:] + pv], axis=1)
        return acc

    acc = lax.fori_loop(0, qi, lambda jj, a: run([(jj * nb + u, 0, False) for u in range(nb)], a),
                        jnp.zeros((V_PAD, TQF), F32))
    acc = run([(qi * nb + u, u * TK, True) for u in range(nb)], acc)
    o_ref[0] = (acc[0:dv] / acc[dv:dv + 1]).astype(o_ref.dtype)


def _causal_attn(qt, k, vt, gran, bound):
    B, H, _, S = qt.shape
    nk, dv = k.shape[2], FOX_DIM
    assert TQ % TK == 0 and TQF % TK == 0 and MLA_V == FOX_DIM and vt.shape[3] == V_PAD

    def call(body, tq, name):
        return pl.pallas_call(
            functools.partial(body, gran_shift=int(np.log2(gran))),
            grid=(B, H, S // tq),
            in_specs=[
                pl.BlockSpec((1, 1, QK_PAD, tq), lambda b, h, q: (b, h, 0, q)),
                pl.BlockSpec((1, 1, nk, TK, QK_PAD), lambda b, h, q: (b, h, 0, 0, 0)),
                pl.BlockSpec((1, 1, nk, V_PAD, TK), lambda b, h, q: (b, h, 0, 0, 0)),
            ],
            out_specs=pl.BlockSpec((1, dv, tq), lambda b, h, q: (b, h, q)),
            out_shape=jax.ShapeDtypeStruct((B, H * dv, S), BF16),
            compiler_params=pltpu.CompilerParams(
                dimension_semantics=("parallel", "parallel", "arbitrary"), vmem_limit_bytes=VMEM_LIMIT),
            name=name,
        )(qt, k, vt)

    return lax.cond(bound <= MAX_SHIFT_BOUND,
                    lambda: call(_shifted_attn_kernel, TQF, "shifted_attn"),
                    lambda: call(_causal_attn_kernel, TQ, "causal_attn"))


def _chunk_attn_kernel(qt_ref, k0_ref, k1_ref, k2_ref, v0_ref, v1_ref, v2_ref, bias_ref, o_ref):
    t = pl.program_id(1)
    k_refs = (k0_ref, k1_ref, k2_ref)
    v_refs = (v0_ref, v1_ref, v2_ref)
    for hd in range(CHK_HEADS):
        qt = qt_ref[0, hd]
        ss = []
        for d in range(3):
            s = _nn(k_refs[d][0, hd // 2], qt) + bias_ref[hd, d * TC:(d + 1) * TC, :]
            if d < 2:
                s = jnp.where(t + (d - 2) >= 0, s, NEG_INF)
            ss.append(s)
        m = jnp.max(jnp.maximum(jnp.maximum(ss[0], ss[1]), ss[2]), axis=0, keepdims=True)
        ps = [jnp.exp(s - m) for s in ss]
        l = jnp.sum(ps[0] + ps[1] + ps[2], axis=0, keepdims=True)
        hs = slice(hd * CHK_DIM, (hd + 1) * CHK_DIM)
        acc = _nn(v_refs[0][0, hs, :], ps[0].astype(BF16))
        acc += _nn(v_refs[1][0, hs, :], ps[1].astype(BF16))
        acc += _nn(v_refs[2][0, hs, :], ps[2].astype(BF16))
        o_ref[0, hs, :] = (acc / l).astype(o_ref.dtype)


def _chunk_attn(qbt, kb, vbt, bias_t):
    B, _, _, S = qbt.shape
    kspec = lambda d: pl.BlockSpec((1, CHK_HEADS // 2, TC, 2 * CHK_DIM),
                                   lambda b, t: (b, 0, jnp.maximum(t + (d - 2), 0), 0))
    vspec = lambda d: pl.BlockSpec((1, CHK_WIDTH, TC),
                                   lambda b, t: (b, 0, jnp.maximum(t + (d - 2), 0)))
    return pl.pallas_call(
        _chunk_attn_kernel,
        grid=(B, S // TC),
        in_specs=[
            pl.BlockSpec((1, CHK_HEADS, 2 * CHK_DIM, TC), lambda b, t: (b, 0, 0, t)),
            kspec(0), kspec(1), kspec(2), vspec(0), vspec(1), vspec(2),
            pl.BlockSpec((CHK_HEADS, 3 * TC, TC), lambda b, t: (0, 0, 0)),
        ],
        out_specs=pl.BlockSpec((1, CHK_WIDTH, TC), lambda b, t: (b, 0, t)),
        out_shape=jax.ShapeDtypeStruct((B, CHK_WIDTH, S), BF16),
        compiler_params=pltpu.CompilerParams(
            dimension_semantics=("parallel", "parallel"), vmem_limit_bytes=VMEM_LIMIT),
        name="chunk_attn",
    )(qbt, kb, kb, kb, vbt, vbt, vbt, bias_t)


def _out_ple_kernel(*refs, n_o):
    o_refs = refs[:n_o]
    gt_ref, h_ref, p_ref, wo_ref, pgn_ref, wgate_ref, pew_ref, out_ref = refs[n_o:]
    ot = jnp.concatenate([r[0] for r in o_refs], axis=0) if n_o > 1 else o_refs[0][0]
    g = gt_ref[0].astype(F32)
    og = (ot.astype(F32) * (g * _sigmoid(g))).astype(BF16)
    mixed_t = _nn(wo_ref[...], og)
    h1 = h_ref[0] + mixed_t.T
    a = _rms_rows(h1, pgn_ref[...]).astype(BF16)
    gate = _sigmoid(_nn(a, wgate_ref[...]))
    pe = _nn(p_ref[0, 0].astype(BF16), pew_ref[...])
    out_ref[0] = h1 + pe * gate


def _out_ple(o_list, gt, h, p, layer, wo_t, pgn, wgate, pew):
    B, S, _ = h.shape
    full = lambda shape: pl.BlockSpec(shape, lambda b, t: (0,) * len(shape))
    width = gt.shape[1]
    return pl.pallas_call(
        functools.partial(_out_ple_kernel, n_o=len(o_list)),
        grid=(B, S // TM),
        in_specs=[pl.BlockSpec((1, o.shape[1], TM), lambda b, t: (b, 0, t)) for o in o_list] + [
            pl.BlockSpec((1, width, TM), lambda b, t: (b, 0, t)),
            pl.BlockSpec((1, TM, D_MODEL), lambda b, t: (b, t, 0)),
            pl.BlockSpec((1, 1, TM, PLE_DIM), lambda b, t: (layer, b, t, 0)),
            full((D_MODEL, width)), full((1, D_MODEL)), full((D_MODEL, D_MODEL)),
            full((PLE_DIM, D_MODEL)),
        ],
        out_specs=pl.BlockSpec((1, TM, D_MODEL), lambda b, t: (b, t, 0)),
        out_shape=jax.ShapeDtypeStruct((B, S, D_MODEL), F32),
        compiler_params=pltpu.CompilerParams(
            dimension_semantics=("parallel", "parallel"), vmem_limit_bytes=VMEM_LIMIT),
        name="out_ple",
    )(*o_list, gt, h, p, wo_t, pgn, wgate, pew)


def _col(v):
    return v.astype(F32).reshape(-1, 1)


def _score_bound(qg, kg, d):
    return 1.02 * d * jnp.max(jnp.abs(qg)) * jnp.max(jnp.abs(kg)) + 1.0


def _chunk_bias_table(rel_bias):
    kk = np.arange(3 * TC)[:, None]
    qq = np.arange(TC)[None, :]
    qchunk = qq // CHUNK + (2 * TC) // CHUNK
    kchunk = kk // CHUNK
    valid = (kchunk <= qchunk) & (kchunk >= qchunk - LEFT_CHUNKS)
    n = 4 * TC
    h = rel_bias.shape[0]
    e = rel_bias.astype(F32)[:, np.clip(np.arange(n) - (TC - 1), -REL_CLIP, REL_CLIP) + REL_CLIP]
    win = jnp.tile(e, (1, 3 * TC + 1))[:, :3 * TC * (n + 1)].reshape(h, 3 * TC, n + 1)[:, :, :TC]
    return jnp.where(valid[None], win[:, ::-1, :], NEG_INF)


def kernel(x, p, positions, norm_g, ab_w_in, mla_q_norm, mla_w_uq, mla_kv_norm, mla_w_ukv, mla_q_gain, mla_k_gain, chk_q_gain, chk_k_gain, chk_rel_bias, ab_w_out, fox_w_in, fox_b_f, fox_q_gain, fox_k_gain, fox_w_out, pe_w, pe_gate_norm, pe_gate_w):
    B, S, _ = x.shape
    half = MLA_ROPE // 2
    inv_freq = 1.0 / (ROPE_THETA ** (jnp.arange(half, dtype=F32) / half))
    ang = positions.astype(F32)[:, None, :] * inv_freq[None, :, None]
    cos, sin = jnp.cos(ang), jnp.sin(ang)
    tri = (np.arange(TM)[:, None] <= np.arange(TM)[None, :]).astype(np.float32)
    tri = jnp.asarray(tri, BF16)

    h = x
    for i in range(DEPTH):
        l = i // 2
        ng = norm_g[i].astype(F32).reshape(1, -1)
        if i % 2 == 0:
            w = ab_w_in[l]
            o = np.cumsum((0, MLA_Q_LORA, MLA_KV_LORA, MLA_ROPE, MLA_WIDTH) + (CHK_WIDTH,) * 4)
            wcq = w[:, o[0]:o[1]].astype(BF16)
            wckv = w[:, o[1]:o[2]].astype(BF16)
            wkr = w[:, o[2]:o[3]].T.astype(BF16)
            wg = jnp.concatenate([w[:, o[3]:o[4]], w[:, o[7]:o[8]]], axis=1).T.astype(BF16)
            wqb = w[:, o[4]:o[5]].T.astype(BF16)
            wkb = w[:, o[5]:o[6]].T.astype(BF16)
            wvb = w[:, o[6]:o[7]].T.astype(BF16)
            qg, kg = _col(mla_q_gain[l]) * (MLA_QK ** -0.5 * LOG2E), _col(mla_k_gain[l])
            bound = _score_bound(qg, kg, MLA_QK)
            qt, k, vt, gt, qbt, kb, vbt = _ab_proj(
                h, ng, wcq, wckv, wkr, wg, wqb, wkb, wvb,
                mla_q_norm[l].astype(F32).reshape(1, -1), mla_w_uq[l].T.astype(BF16),
                mla_kv_norm[l].astype(F32).reshape(1, -1), mla_w_ukv[l].T.astype(BF16),
                qg, kg, _col(chk_q_gain[l]) * (CHK_DIM ** -0.5), _col(chk_k_gain[l]), cos, sin,
                bound.reshape(1, 1))
            o_a = _causal_attn(qt, k, vt, CHUNK, bound)
            o_b = _chunk_attn(qbt, kb, vbt, _chunk_bias_table(chk_rel_bias[l]))
            o_list, wo = [o_a, o_b], ab_w_out[l]
        else:
            w = fox_w_in[l]
            o = np.cumsum((0,) + (FOX_WIDTH,) * 4 + (FOX_HEADS,))
            wq, wk, wv, wgc, wf = [w[:, o[n]:o[n + 1]].T.astype(BF16) for n in range(5)]
            qg, kg = _col(fox_q_gain[l]) * (FOX_DIM ** -0.5 * LOG2E), _col(fox_k_gain[l])
            bound = _score_bound(qg, kg, FOX_DIM)
            qa, ka, vt, gt = _fox_proj(h, ng, wq, wk, wv, wgc, wf, _col(fox_b_f[l]), qg, kg, tri,
                                       bound.reshape(1, 1))
            o_list, wo = [_causal_attn(qa, ka, vt, 1, bound)], fox_w_out[l]
        h = _out_ple(o_list, gt, h, p, i, wo.T.astype(BF16),
                     pe_gate_norm[i].astype(F32).reshape(1, -1),
                     pe_gate_w[i].astype(BF16), pe_w[i].astype(BF16))
    return h
```

```python
import functools

import numpy as np
import jax
import jax.numpy as jnp
from jax import lax
from jax.experimental import pallas as pl
from jax.experimental.pallas import tpu as pltpu

F32 = jnp.float32
BF16 = jnp.bfloat16

D_MODEL = 1024
DEPTH = 4
CHUNK = 64
PLE_DIM = 256
EPS = 1e-6
NEG_INF = -1e30
MLA_HEADS = 8
MLA_NOPE = 64
MLA_ROPE = 32
MLA_QK = MLA_NOPE + MLA_ROPE
MLA_V = 64
MLA_Q_LORA = 384
MLA_KV_LORA = 256
ROPE_THETA = 10000.0
CHK_HEADS = 8
CHK_DIM = 64
LEFT_CHUNKS = 8
REL_CLIP = 256
FOX_HEADS = 16
FOX_DIM = 64
MLA_WIDTH = MLA_HEADS * MLA_V
CHK_WIDTH = CHK_HEADS * CHK_DIM
FOX_WIDTH = FOX_HEADS * FOX_DIM

TM = 512
TQ = 512
TQF = 1024
LONG_STEP = 2
MAX_SHIFT_BOUND = 45.0
TK = 256
TC = 256
QK_PAD = 128
V_PAD = 80
LOOKAHEAD = 2
LOG2E = 1.4426950408889634
VMEM_LIMIT = 56 * 1024 * 1024


def _nt(a, b):
    return lax.dot_general(a, b, (((1,), (1,)), ((), ())), preferred_element_type=F32)


def _nn(a, b):
    return jnp.dot(a, b, preferred_element_type=F32)


def _sigmoid(x):
    return 1.0 / (1.0 + jnp.exp(-x))


def _rms_rows(x, g):
    ms = jnp.mean(x * x, axis=-1, keepdims=True)
    return x * lax.rsqrt(ms + EPS) * g


def _split3(x):
    hi = x.astype(BF16)
    r1 = x - hi.astype(F32)
    mid = r1.astype(BF16)
    lo = (r1 - mid.astype(F32)).astype(BF16)
    return hi, mid, lo


def _store_kv_blocks(k_ref, vt_ref, hd, k_t, v_t, ones_row):
    k_rows = k_t.T.astype(BF16)
    v_aug = jnp.concatenate([v_t.astype(BF16), ones_row], axis=0)
    for c in range(TM // TK):
        k_ref[0, hd, c] = k_rows[c * TK:(c + 1) * TK]
        vt_ref[0, hd, c] = v_aug[:, c * TK:(c + 1) * TK]


def _rope_t(x, cos, sin):
    half = x.shape[0] // 2
    x1, x2 = x[:half], x[half:]
    return jnp.concatenate([x1 * cos - x2 * sin, x2 * cos + x1 * sin], axis=0)


def _fox_proj_kernel(h_ref, ng_ref, wq_ref, wk_ref, wv_ref, wg_ref, wf_ref, bf_ref, qg_ref, kg_ref,
                     tri_ref, bound_ref, qa_ref, ka_ref, vt_ref, gt_ref, carry_ref):
    t = pl.program_id(1)
    u = _rms_rows(h_ref[0], ng_ref[...]).astype(BF16)

    z = _nt(wf_ref[...], u) + bf_ref[...]
    logf = jnp.minimum(z, 0.0) - jnp.log(1.0 + jnp.exp(-jnp.abs(z)))
    parts = jnp.concatenate(_split3(logf), axis=0)
    c3 = _nn(parts, tri_ref[...])
    local = c3[0:FOX_HEADS] + c3[FOX_HEADS:2 * FOX_HEADS] + c3[2 * FOX_HEADS:3 * FOX_HEADS]

    @pl.when(t == 0)
    def _():
        carry_ref[...] = jnp.zeros_like(carry_ref)

    cum = local + carry_ref[:, 0:1]
    carry_ref[...] = jnp.broadcast_to(cum[:, TM - 1:TM], carry_ref.shape)
    c_hi, c_mid, c_lo = [p.astype(F32) for p in _split3(cum * LOG2E)]
    d_hi, d_mid, d_lo = [p.astype(F32) for p in _split3(cum * LOG2E + bound_ref[...])]

    row = lax.broadcasted_iota(jnp.int32, (16, TM), 0)
    ones_row = jnp.where(row == 0, 1.0, 0.0).astype(BF16)
    zeros48 = jnp.zeros((QK_PAD - FOX_DIM - 16, TM), F32)
    qg = qg_ref[...]
    kg = kg_ref[...]
    n_chunk = FOX_WIDTH // 256
    for c in range(n_chunk):
        rows = slice(c * 256, (c + 1) * 256)
        qc = _nt(wq_ref[rows, :], u)
        kc = _nt(wk_ref[rows, :], u)
        vc = _nt(wv_ref[rows, :], u)
        gc = _nt(wg_ref[rows, :], u)
        gt_ref[0, rows, :] = gc.astype(BF16)
        for j in range(4):
            hd = c * 4 + j
            hs = slice(j * FOX_DIM, (j + 1) * FOX_DIM)
            hi, mid, lo = c_hi[hd:hd + 1], c_mid[hd:hd + 1], c_lo[hd:hd + 1]
            khi, kmid, klo = d_hi[hd:hd + 1], d_mid[hd:hd + 1], d_lo[hd:hd + 1]
            qh = qc[hs]
            qn = qh * lax.rsqrt(jnp.mean(qh * qh, axis=0, keepdims=True) + EPS) * qg
            exq = jnp.where(row < 3, 1.0, jnp.where(row == 3, hi, jnp.where(row == 4, mid,
                            jnp.where(row == 5, lo, 0.0))))
            qa_ref[0, hd, 0:FOX_DIM, :] = qn.astype(BF16)
            qa_ref[0, hd, FOX_DIM:FOX_DIM + 16, :] = exq.astype(BF16)
            qa_ref[0, hd, FOX_DIM + 16:QK_PAD, :] = zeros48.astype(BF16)
            kh = kc[hs]
            kn = kh * lax.rsqrt(jnp.mean(kh * kh, axis=0, keepdims=True) + EPS) * kg
            exk = jnp.where(row == 0, -khi, jnp.where(row == 1, -kmid, jnp.where(row == 2, -klo,
                            jnp.where(row < 6, 1.0, 0.0))))
            kfull = jnp.concatenate([kn, exk, zeros48], axis=0)
            _store_kv_blocks(ka_ref, vt_ref, hd, kfull, vc[hs], ones_row)


def _fox_proj(h, ng, wq, wk, wv, wg, wf, bfc, qg, kg, tri, bound):
    B, S, _ = h.shape
    nt = S // TM
    full = lambda shape: pl.BlockSpec(shape, lambda b, t: (0,) * len(shape))
    return pl.pallas_call(
        _fox_proj_kernel,
        grid=(B, nt),
        in_specs=[
            pl.BlockSpec((1, TM, D_MODEL), lambda b, t: (b, t, 0)),
            full((1, D_MODEL)),
            full((FOX_WIDTH, D_MODEL)), full((FOX_WIDTH, D_MODEL)),
            full((FOX_WIDTH, D_MODEL)), full((FOX_WIDTH, D_MODEL)),
            full((FOX_HEADS, D_MODEL)), full((FOX_HEADS, 1)),
            full((FOX_DIM, 1)), full((FOX_DIM, 1)),
            full((TM, TM)), full((1, 1)),
        ],
        out_specs=[
            pl.BlockSpec((1, FOX_HEADS, QK_PAD, TM), lambda b, t: (b, 0, 0, t)),
            pl.BlockSpec((1, FOX_HEADS, TM // TK, TK, QK_PAD), lambda b, t: (b, 0, t, 0, 0)),
            pl.BlockSpec((1, FOX_HEADS, TM // TK, V_PAD, TK), lambda b, t: (b, 0, t, 0, 0)),
            pl.BlockSpec((1, FOX_WIDTH, TM), lambda b, t: (b, 0, t)),
        ],
        out_shape=[
            jax.ShapeDtypeStruct((B, FOX_HEADS, QK_PAD, S), BF16),
            jax.ShapeDtypeStruct((B, FOX_HEADS, S // TK, TK, QK_PAD), BF16),
            jax.ShapeDtypeStruct((B, FOX_HEADS, S // TK, V_PAD, TK), BF16),
            jax.ShapeDtypeStruct((B, FOX_WIDTH, S), BF16),
        ],
        scratch_shapes=[pltpu.VMEM((FOX_HEADS, 128), F32)],
        compiler_params=pltpu.CompilerParams(
            dimension_semantics=("parallel", "arbitrary"), vmem_limit_bytes=VMEM_LIMIT),
        name="fox_proj",
    )(h, ng, wq, wk, wv, wg, wf, bfc, qg, kg, tri, bound)


def _ab_proj_kernel(h_ref, ng_ref, wcq_ref, wckv_ref, wkr_ref, wg_ref, wqb_ref, wkb_ref, wvb_ref,
                    qnorm_ref, wuq_ref, kvnorm_ref, wukv_ref, qg_ref, kg_ref, cqg_ref, ckg_ref,
                    cos_ref, sin_ref, bound_ref,
                    qt_ref, k_ref, vt_ref, gt_ref, qbt_ref, kb_ref, vbt_ref):
    u = _rms_rows(h_ref[0], ng_ref[...]).astype(BF16)
    cos = cos_ref[0]
    sin = sin_ref[0]

    cqn = _rms_rows(_nn(u, wcq_ref[...]), qnorm_ref[...]).astype(BF16)
    ckvn = _rms_rows(_nn(u, wckv_ref[...]), kvnorm_ref[...]).astype(BF16)
    krt = _nt(wkr_ref[...], u)
    ss_kr = jnp.sum(krt * krt, axis=0, keepdims=True)
    qg = qg_ref[...]
    kg = kg_ref[...]
    row32 = lax.broadcasted_iota(jnp.int32, (QK_PAD - MLA_QK, TM), 0)
    q_pad = jnp.where(row32 == 0, 1.0, 0.0)
    k_pad = jnp.where(row32 == 0, -bound_ref[...], 0.0)
    ones_row =jnp.where(lax.broadcasted_iota(jnp.int32, (V_PAD - MLA_V, TM), 0) == 0, 1.0, 0.0).astype(BF16)
    for hd in range(MLA_HEADS):
        qh = _nt(wuq_ref[hd * MLA_QK:(hd + 1) * MLA_QK, :], cqn)
        qn = qh * lax.rsqrt(jnp.mean(qh * qh, axis=0, keepdims=True) + EPS) * qg
        qt_ref[0, hd, 0:MLA_NOPE, :] = qn[0:MLA_NOPE].astype(BF16)
        qt_ref[0, hd, MLA_NOPE:MLA_QK, :] = _rope_t(qn[MLA_NOPE:MLA_QK], cos, sin).astype(BF16)
        qt_ref[0, hd, MLA_QK:QK_PAD, :] = q_pad.astype(BF16)
        kv = _nt(wukv_ref[hd * 128:(hd + 1) * 128, :], ckvn)
        kn = kv[0:MLA_NOPE]
        rk = lax.rsqrt((jnp.sum(kn * kn, axis=0, keepdims=True) + ss_kr) * (1.0 / MLA_QK) + EPS)
        kfull = jnp.concatenate([kn * rk * kg[0:MLA_NOPE],
                                 _rope_t(krt * rk * kg[MLA_NOPE:MLA_QK], cos, sin),
                                 k_pad], axis=0)
        _store_kv_blocks(k_ref, vt_ref, hd, kfull, kv[MLA_NOPE:MLA_NOPE + MLA_V], ones_row)

    for c in range((MLA_WIDTH + CHK_WIDTH) // 256):
        rows = slice(c * 256, (c + 1) * 256)
        gt_ref[0, rows, :] = _nt(wg_ref[rows, :], u).astype(BF16)

    cqg = cqg_ref[...]
    ckg = ckg_ref[...]
    zeros64 = jnp.zeros((CHK_DIM, TM), F32)
    for c in range(CHK_WIDTH // 256):
        rows = slice(c * 256, (c + 1) * 256)
        qc = _nt(wqb_ref[rows, :], u)
        kc = _nt(wkb_ref[rows, :], u)
        vc = _nt(wvb_ref[rows, :], u).astype(BF16)
        for j in range(4):
            vbt_ref[0, c * 4 + j, 0:CHK_DIM, :] = vc[j * CHK_DIM:(j + 1) * CHK_DIM]
            vbt_ref[0, c * 4 + j, CHK_DIM:V_PAD, :] = ones_row
        kns = []
        for j in range(4):
            hd = c * 4 + j
            hs = slice(j * CHK_DIM, (j + 1) * CHK_DIM)
            qh = qc[hs]
            qn = (qh * lax.rsqrt(jnp.mean(qh * qh, axis=0, keepdims=True) + EPS) * cqg).astype(BF16)
            lo, hi = (0, CHK_DIM) if hd % 2 == 0 else (CHK_DIM, 2 * CHK_DIM)
            qbt_ref[0, hd, lo:hi, :] = qn
            qbt_ref[0, hd, CHK_DIM - lo:2 * CHK_DIM - lo, :] = zeros64.astype(BF16)
            kh = kc[hs]
            kns.append(kh * lax.rsqrt(jnp.mean(kh * kh, axis=0, keepdims=True) + EPS) * ckg)
        for pr in range(2):
            pair = jnp.concatenate([kns[2 * pr], kns[2 * pr + 1]], axis=0)
            kb_ref[0, c * 2 + pr] = pair.T.astype(BF16)


def _ab_proj(h, ng, wcq, wckv, wkr, wg, wqb, wkb, wvb, qnorm, wuq, kvnorm, wukv, qg, kg, cqg, ckg,
             cos, sin, bound):
    B, S, _ = h.shape
    nt = S // TM
    full = lambda shape: pl.BlockSpec(shape, lambda b, t: (0,) * len(shape))
    return pl.pallas_call(
        _ab_proj_kernel,
        grid=(B, nt),
        in_specs=[
            pl.BlockSpec((1, TM, D_MODEL), lambda b, t: (b, t, 0)),
            full((1, D_MODEL)),
            full((D_MODEL, MLA_Q_LORA)), full((D_MODEL, MLA_KV_LORA)), full((MLA_ROPE, D_MODEL)),
            full((MLA_WIDTH + CHK_WIDTH, D_MODEL)),
            full((CHK_WIDTH, D_MODEL)), full((CHK_WIDTH, D_MODEL)), full((CHK_WIDTH, D_MODEL)),
            full((1, MLA_Q_LORA)), full((MLA_HEADS * MLA_QK, MLA_Q_LORA)),
            full((1, MLA_KV_LORA)), full((MLA_HEADS * 128, MLA_KV_LORA)),
            full((MLA_QK, 1)), full((MLA_QK, 1)), full((CHK_DIM, 1)), full((CHK_DIM, 1)),
            pl.BlockSpec((1, MLA_ROPE // 2, TM), lambda b, t: (b, 0, t)),
            pl.BlockSpec((1, MLA_ROPE // 2, TM), lambda b, t: (b, 0, t)),
            full((1, 1)),
        ],
        out_specs=[
            pl.BlockSpec((1, MLA_HEADS, QK_PAD, TM), lambda b, t: (b, 0, 0, t)),
            pl.BlockSpec((1, MLA_HEADS, TM // TK, TK, QK_PAD), lambda b, t: (b, 0, t, 0, 0)),
            pl.BlockSpec((1, MLA_HEADS, TM // TK, V_PAD, TK), lambda b, t: (b, 0, t, 0, 0)),
            pl.BlockSpec((1, MLA_WIDTH + CHK_WIDTH, TM), lambda b, t: (b, 0, t)),
            pl.BlockSpec((1, CHK_HEADS, 2 * CHK_DIM, TM), lambda b, t: (b, 0, 0, t)),
            pl.BlockSpec((1, CHK_HEADS // 2, TM, 2 * CHK_DIM), lambda b, t: (b, 0, t, 0)),
            pl.BlockSpec((1, CHK_HEADS, V_PAD, TM), lambda b, t: (b, 0, 0, t)),
        ],
        out_shape=[
            jax.ShapeDtypeStruct((B, MLA_HEADS, QK_PAD, S), BF16),
            jax.ShapeDtypeStruct((B, MLA_HEADS, S // TK, TK, QK_PAD), BF16),
            jax.ShapeDtypeStruct((B, MLA_HEADS, S // TK, V_PAD, TK), BF16),
            jax.ShapeDtypeStruct((B, MLA_WIDTH + CHK_WIDTH, S), BF16),
            jax.ShapeDtypeStruct((B, CHK_HEADS, 2 * CHK_DIM, S), BF16),
            jax.ShapeDtypeStruct((B, CHK_HEADS // 2, S, 2 * CHK_DIM), BF16),
            jax.ShapeDtypeStruct((B, CHK_HEADS, V_PAD, S), BF16),
        ],
        compiler_params=pltpu.CompilerParams(
            dimension_semantics=("parallel", "parallel"), vmem_limit_bytes=VMEM_LIMIT),
        name="ab_proj",
    )(h, ng, wcq, wckv, wkr, wg, wqb, wkb, wvb, qnorm, wuq, kvnorm, wukv, qg, kg, cqg, ckg, cos, sin,
      bound)


def _causal_attn_kernel(qt_ref, k_ref, vt_ref, o_ref, *, gran_shift):
    qi = pl.program_id(2)
    dv = o_ref.shape[1]
    n_sub = TQ // TK
    qts = [qt_ref[0, 0, :, c * TK:(c + 1) * TK] for c in range(n_sub)]

    def scores(c, j, masked):
        s = _nn(k_ref[0, 0, j], qts[c])
        if masked:
            kpos = lax.broadcasted_iota(jnp.int32, (TK, TK), 0)
            qpos = lax.broadcasted_iota(jnp.int32, (TK, TK), 1)
            s = jnp.where((kpos >> gran_shift) <= (qpos >> gran_shift), s, NEG_INF)
        return s

    def update(s, j, carry):
        m, acc = carry
        m_new = jnp.maximum(m, jnp.max(s, axis=0, keepdims=True))
        p = jnp.exp2(s - m_new)
        acc = jnp.exp2(m - m_new) * acc + _nn(vt_ref[0, 0, j], p.astype(BF16))
        return m_new, acc

    def run(work, carries):
        carries = list(carries)
        pending = [scores(*w) for w in work[:LOOKAHEAD]]
        for i, (c, j, _) in enumerate(work):
            if i + LOOKAHEAD < len(work):
                pending.append(scores(*work[i + LOOKAHEAD]))
            carries[c] = update(pending.pop(0), j, carries[c])
        return tuple(carries)

    def steps(jj, carries):
        return run([(c, jj * n_sub + u, False) for u in range(n_sub) for c in range(n_sub)], carries)

    init = (jnp.full((1, TK), NEG_INF, F32), jnp.zeros((V_PAD, TK), F32))
    carries = lax.fori_loop(0, qi, steps, (init,) * n_sub)
    carries = run([(c, qi * n_sub + u, u == c) for c in range(n_sub) for u in range(c + 1)], carries)
    for c in range(n_sub):
        acc = carries[c][1]
        o_ref[0, :, c * TK:(c + 1) * TK] = (acc[0:dv] / acc[dv:dv + 1]).astype(o_ref.dtype)


def _shifted_attn_kernel(qt_ref, k_ref, vt_ref, o_ref, *, gran_shift):
    qi = pl.program_id(2)
    dv = o_ref.shape[1]
    nb = TQF // TK
    qt = qt_ref[0, 0]

    def probs(j, lo, masked):
        s = _nn(k_ref[0, 0, j], qt[:, lo:])
        if masked:
            kpos = lax.broadcasted_iota(jnp.int32, s.shape, 0)
            qpos = lax.broadcasted_iota(jnp.int32, s.shape, 1)
            s = jnp.where((kpos >> gran_shift) <= (qpos >> gran_shift), s, NEG_INF)
        return jnp.exp2(s).astype(BF16)

    def run(work, acc):
        pending = [probs(*w) for w in work[:LOOKAHEAD]]
        for i, (j, lo, _) in enumerate(work):
            if i + LOOKAHEAD < len(work):
                pending.append(probs(*work[i + LOOKAHEAD]))
            pv = _nn(vt_ref[0, 0, j], pending.pop(0))
            acc = acc + pv if lo == 0 else jnp.concatenate([acc[:, :lo], acc[:, lo:] + pv], axis=1)
        return acc

    def full_blocks(n_blk):
        return lambda jj, a: run([(jj * n_blk + u, 0, False) for u in range(n_blk)], a)

    n_long = qi // LONG_STEP
    acc = lax.fori_loop(0, n_long, full_blocks(LONG_STEP * nb), jnp.zeros((V_PAD, TQF), F32))
    acc = lax.fori_loop(n_long * LONG_STEP, qi, full_blocks(nb), acc)
    acc = run([(qi * nb + u, u * TK, True) for u in range(nb)], acc)
    o_ref[0] = (acc[0:dv] / acc[dv:dv + 1]).astype(o_ref.dtype)


def _causal_attn(qt, k, vt, gran, bound):
    B, H, _, S = qt.shape
    nk, dv = k.shape[2], FOX_DIM
    assert TQ % TK == 0 and TQF % TK == 0 and MLA_V == FOX_DIM and vt.shape[3] == V_PAD

    def call(body, tq, name):
        return pl.pallas_call(
            functools.partial(body, gran_shift=int(np.log2(gran))),
            grid=(B, H, S // tq),
            in_specs=[
                pl.BlockSpec((1, 1, QK_PAD, tq), lambda b, h, q: (b, h, 0, q)),
                pl.BlockSpec((1, 1, nk, TK, QK_PAD), lambda b, h, q: (b, h, 0, 0, 0)),
                pl.BlockSpec((1, 1, nk, V_PAD, TK), lambda b, h, q: (b, h, 0, 0, 0)),
            ],
            out_specs=pl.BlockSpec((1, dv, tq), lambda b, h, q: (b, h, q)),
            out_shape=jax.ShapeDtypeStruct((B, H * dv, S), BF16),
            compiler_params=pltpu.CompilerParams(
                dimension_semantics=("parallel", "parallel", "arbitrary"), vmem_limit_bytes=VMEM_LIMIT),
            name=name,
        )(qt, k, vt)

    return lax.cond(bound <= MAX_SHIFT_BOUND,
                    lambda: call(_shifted_attn_kernel, TQF, "shifted_attn"),
                    lambda: call(_causal_attn_kernel, TQ, "causal_attn"))


def _chunk_attn_kernel(qt_ref, k0_ref, k1_ref, k2_ref, v0_ref, v1_ref, v2_ref, bias_ref, o_ref, *,
                       shifted):
    t = pl.program_id(1)
    k_refs = (k0_ref, k1_ref, k2_ref)
    v_refs = (v0_ref, v1_ref, v2_ref)

    def scores(hd):
        qt = qt_ref[0, hd]
        ss = []
        for d in range(3):
            s = _nn(k_refs[d][0, hd // 2], qt) + bias_ref[hd, d * TC:(d + 1) * TC, :]
            if d < 2:
                s = jnp.where(t + (d - 2) >= 0, s, NEG_INF)
            ss.append(s)
        return ss

    def finish(hd, ss):
        if not shifted:
            m = jnp.max(jnp.maximum(jnp.maximum(ss[0], ss[1]), ss[2]), axis=0, keepdims=True)
            ss = [s - m for s in ss]
        p = jnp.concatenate([jnp.exp2(s).astype(BF16) for s in ss], axis=0)
        vt = jnp.concatenate([r[0, hd] for r in v_refs], axis=1)
        acc = _nn(vt, p)
        o_ref[0, hd * CHK_DIM:(hd + 1) * CHK_DIM, :] = (
            acc[0:CHK_DIM] / acc[CHK_DIM:CHK_DIM + 1]).astype(o_ref.dtype)

    pending = [scores(0)]
    for hd in range(CHK_HEADS):
        if hd + 1 < CHK_HEADS:
            pending.append(scores(hd + 1))
        finish(hd, pending.pop(0))


def _chunk_attn(qbt, kb, vbt, bias_t, spread):
    B, _, _, S = qbt.shape
    kspec = lambda d: pl.BlockSpec((1, CHK_HEADS // 2, TC, 2 * CHK_DIM),
                                   lambda b, t: (b, 0, jnp.maximum(t + (d - 2), 0), 0))
    vspec = lambda d: pl.BlockSpec((1, CHK_HEADS, V_PAD, TC),
                                   lambda b, t: (b, 0, 0, jnp.maximum(t + (d - 2), 0)))

    def call(shifted):
        return pl.pallas_call(
            functools.partial(_chunk_attn_kernel, shifted=shifted),
            grid=(B, S // TC),
            in_specs=[
                pl.BlockSpec((1, CHK_HEADS, 2 * CHK_DIM, TC), lambda b, t: (b, 0, 0, t)),
                kspec(0), kspec(1), kspec(2), vspec(0), vspec(1), vspec(2),
                pl.BlockSpec((CHK_HEADS, 3 * TC, TC), lambda b, t: (0, 0, 0)),
            ],
            out_specs=pl.BlockSpec((1, CHK_WIDTH, TC), lambda b, t: (b, 0, t)),
            out_shape=jax.ShapeDtypeStruct((B, CHK_WIDTH, S), BF16),
            compiler_params=pltpu.CompilerParams(
                dimension_semantics=("parallel", "parallel"), vmem_limit_bytes=VMEM_LIMIT),
            name="chunk_attn_shifted" if shifted else "chunk_attn",
        )(qbt, kb, kb, kb, vbt, vbt, vbt, bias_t)

    return lax.cond(spread <= 2 * MAX_SHIFT_BOUND, lambda: call(True), lambda: call(False))


def _out_ple_kernel(*refs, n_o):
    o_refs = refs[:n_o]
    gt_ref, h_ref, p_ref, wo_ref, pgn_ref, wgate_ref, pew_ref, out_ref = refs[n_o:]
    ot = jnp.concatenate([r[0] for r in o_refs], axis=0) if n_o > 1 else o_refs[0][0]
    g = gt_ref[0].astype(F32)
    og = (ot.astype(F32) * (g * _sigmoid(g))).astype(BF16)
    mixed_t = _nn(wo_ref[...], og)
    h1 = h_ref[0] + mixed_t.T
    a = _rms_rows(h1, pgn_ref[...]).astype(BF16)
    gate = _sigmoid(_nn(a, wgate_ref[...]))
    pe = _nn(p_ref[0, 0].astype(BF16), pew_ref[...])
    out_ref[0] = h1 + pe * gate


def _out_ple(o_list, gt, h, p, layer, wo_t, pgn, wgate, pew):
    B, S, _ = h.shape
    full = lambda shape: pl.BlockSpec(shape, lambda b, t: (0,) * len(shape))
    width = gt.shape[1]
    return pl.pallas_call(
        functools.partial(_out_ple_kernel, n_o=len(o_list)),
        grid=(B, S // TM),
        in_specs=[pl.BlockSpec((1, o.shape[1], TM), lambda b, t: (b, 0, t)) for o in o_list] + [
            pl.BlockSpec((1, width, TM), lambda b, t: (b, 0, t)),
            pl.BlockSpec((1, TM, D_MODEL), lambda b, t: (b, t, 0)),
            pl.BlockSpec((1, 1, TM, PLE_DIM), lambda b, t: (layer, b, t, 0)),
            full((D_MODEL, width)), full((1, D_MODEL)), full((D_MODEL, D_MODEL)),
            full((PLE_DIM, D_MODEL)),
        ],
        out_specs=pl.BlockSpec((1, TM, D_MODEL), lambda b, t: (b, t, 0)),
        out_shape=jax.ShapeDtypeStruct((B, S, D_MODEL), F32),
        compiler_params=pltpu.CompilerParams(
            dimension_semantics=("parallel", "parallel"), vmem_limit_bytes=VMEM_LIMIT),
        name="out_ple",
    )(*o_list, gt, h, p, wo_t, pgn, wgate, pew)


def _col(v):
    return v.astype(F32).reshape(-1, 1)


def _score_bound(qg, kg, d):
    return 1.02 * d * jnp.max(jnp.abs(qg)) * jnp.max(jnp.abs(kg)) + 1.0


def _chunk_bias_table(rel_bias, bound_qk):
    n = 4 * TC
    i = np.arange(n)
    dist = np.where(i < TC, i, i - n) + 2 * TC
    rb = rel_bias.astype(F32) * LOG2E
    spread = 2.0 * bound_qk + (jnp.max(rb) - jnp.min(rb))
    e = rb[:, np.clip(dist, -REL_CLIP, REL_CLIP) + REL_CLIP] - (bound_qk + jnp.max(rb))
    h = rel_bias.shape[0]
    return _bias_table(e.reshape(h, 1, n)), spread


def _bias_table(e):
    h, _, n = e.shape
    return pl.pallas_call(
        _bias_table_kernel,
        grid=(h,),
        in_specs=[pl.BlockSpec((1, 1, n), lambda hd: (hd, 0, 0))],
        out_specs=pl.BlockSpec((1, 3 * TC, TC), lambda hd: (hd, 0, 0)),
        out_shape=jax.ShapeDtypeStruct((h, 3 * TC, TC), F32),
        name="bias_table",
    )(e)


def _bias_table_kernel(e_ref, o_ref):
    n = e_ref.shape[2]
    rows = jnp.broadcast_to(e_ref[0], (3 * TC, n))
    skew = pltpu.roll(rows, 0, 1, stride=1, stride_axis=0)
    kchunk = lax.broadcasted_iota(jnp.int32, (3 * TC, TC), 0) // CHUNK
    qchunk = lax.broadcasted_iota(jnp.int32, (3 * TC, TC), 1) // CHUNK + (2 * TC) // CHUNK
    valid = (kchunk <= qchunk) & (kchunk >= qchunk - LEFT_CHUNKS)
    o_ref[0] = jnp.where(valid, skew[:, 0:TC], NEG_INF)


def kernel(x, p, positions, norm_g, ab_w_in, mla_q_norm, mla_w_uq, mla_kv_norm, mla_w_ukv, mla_q_gain, mla_k_gain, chk_q_gain, chk_k_gain, chk_rel_bias, ab_w_out, fox_w_in, fox_b_f, fox_q_gain, fox_k_gain, fox_w_out, pe_w, pe_gate_norm, pe_gate_w):
    B, S, _ = x.shape
    half = MLA_ROPE // 2
    inv_freq = 1.0 / (ROPE_THETA ** (jnp.arange(half, dtype=F32) / half))
    ang = positions.astype(F32)[:, None, :] * inv_freq[None, :, None]
    cos, sin = jnp.cos(ang), jnp.sin(ang)
    tri = (np.arange(TM)[:, None] <= np.arange(TM)[None, :]).astype(np.float32)
    tri = jnp.asarray(tri, BF16)

    h = x
    for i in range(DEPTH):
        l = i // 2
        ng = norm_g[i].astype(F32).reshape(1, -1)
        if i % 2 == 0:
            w = ab_w_in[l]
            o = np.cumsum((0, MLA_Q_LORA, MLA_KV_LORA, MLA_ROPE, MLA_WIDTH) + (CHK_WIDTH,) * 4)
            wcq = w[:, o[0]:o[1]].astype(BF16)
            wckv = w[:, o[1]:o[2]].astype(BF16)
            wkr = w[:, o[2]:o[3]].T.astype(BF16)
            wg = jnp.concatenate([w[:, o[3]:o[4]], w[:, o[7]:o[8]]], axis=1).T.astype(BF16)
            wqb = w[:, o[4]:o[5]].T.astype(BF16)
            wkb = w[:, o[5]:o[6]].T.astype(BF16)
            wvb = w[:, o[6]:o[7]].T.astype(BF16)
            qg, kg = _col(mla_q_gain[l]) * (MLA_QK ** -0.5 * LOG2E), _col(mla_k_gain[l])
            bound = _score_bound(qg, kg, MLA_QK)
            cqg, ckg = _col(chk_q_gain[l]) * (CHK_DIM ** -0.5 * LOG2E), _col(chk_k_gain[l])
            qt, k, vt, gt, qbt, kb, vbt = _ab_proj(
                h, ng, wcq, wckv, wkr, wg, wqb, wkb, wvb,
                mla_q_norm[l].astype(F32).reshape(1, -1), mla_w_uq[l].T.astype(BF16),
                mla_kv_norm[l].astype(F32).reshape(1, -1), mla_w_ukv[l].T.astype(BF16),
                qg, kg, cqg, ckg, cos, sin, bound.reshape(1, 1))
            o_a = _causal_attn(qt, k, vt, CHUNK, bound)
            bias_t, spread = _chunk_bias_table(chk_rel_bias[l], _score_bound(cqg, ckg, CHK_DIM))
            o_b = _chunk_attn(qbt, kb, vbt, bias_t, spread)
            o_list, wo = [o_a, o_b], ab_w_out[l]
        else:
            w = fox_w_in[l]
            o = np.cumsum((0,) + (FOX_WIDTH,) * 4 + (FOX_HEADS,))
            wq, wk, wv, wgc, wf = [w[:, o[n]:o[n + 1]].T.astype(BF16) for n in range(5)]
            qg, kg = _col(fox_q_gain[l]) * (FOX_DIM ** -0.5 * LOG2E), _col(fox_k_gain[l])
            bound = _score_bound(qg, kg, FOX_DIM)
            qa, ka, vt, gt = _fox_proj(h, ng, wq, wk, wv, wgc, wf, _col(fox_b_f[l]), qg, kg, tri,
                                       bound.reshape(1, 1))
            o_list, wo = [_causal_attn(qa, ka, vt, 1, bound)], fox_w_out[l]
        h = _out_ple(o_list, gt, h, p, i, wo.T.astype(BF16),
                     pe_gate_norm[i].astype(F32).reshape(1, -1),
                     pe_gate_w[i].astype(BF16), pe_w[i].astype(BF16))
    return h
```

```python
import functools

import numpy as np
import jax
import jax.numpy as jnp
from jax import lax
from jax.experimental import pallas as pl
from jax.experimental.pallas import tpu as pltpu

F32 = jnp.float32
BF16 = jnp.bfloat16

D_MODEL = 1024
DEPTH = 4
CHUNK = 64
PLE_DIM = 256
EPS = 1e-6
NEG_INF = -1e30
MLA_HEADS = 8
MLA_NOPE = 64
MLA_ROPE = 32
MLA_QK = MLA_NOPE + MLA_ROPE
MLA_V = 64
MLA_Q_LORA = 384
MLA_KV_LORA = 256
ROPE_THETA = 10000.0
CHK_HEADS = 8
CHK_DIM = 64
LEFT_CHUNKS = 8
REL_CLIP = 256
FOX_HEADS = 16
FOX_DIM = 64
MLA_WIDTH = MLA_HEADS * MLA_V
CHK_WIDTH = CHK_HEADS * CHK_DIM
FOX_WIDTH = FOX_HEADS * FOX_DIM

TM = 512
TQ = 512
TQF = 1024
LONG_STEP = 2
UNDERFLOW_LOG2 = 160.0
MAX_SHIFT_BOUND = 45.0
TK = 256
TC = 256
QK_PAD = 128
V_PAD = 80
LOOKAHEAD = 1
LOG2E = 1.4426950408889634
VMEM_LIMIT = 56 * 1024 * 1024


def _nt(a, b):
    return lax.dot_general(a, b, (((1,), (1,)), ((), ())), preferred_element_type=F32)


def _nn(a, b):
    return jnp.dot(a, b, preferred_element_type=F32)


def _sigmoid(x):
    return 1.0 / (1.0 + jnp.exp(-x))


def _rms_rows(x, g):
    ms = jnp.mean(x * x, axis=-1, keepdims=True)
    return x * lax.rsqrt(ms + EPS) * g


def _split3(x):
    hi = x.astype(BF16)
    r1 = x - hi.astype(F32)
    mid = r1.astype(BF16)
    lo = (r1 - mid.astype(F32)).astype(BF16)
    return hi, mid, lo


def _store_kv_blocks(k_ref, vt_ref, hd, k_t, v_t, ones_row):
    k_rows = k_t.T.astype(BF16)
    v_aug = jnp.concatenate([v_t.astype(BF16), ones_row], axis=0)
    for c in range(TM // TK):
        k_ref[0, hd, c] = k_rows[c * TK:(c + 1) * TK]
        vt_ref[0, hd, c] = v_aug[:, c * TK:(c + 1) * TK]


def _rope_t(x, cos, sin):
    half = x.shape[0] // 2
    x1, x2 = x[:half], x[half:]
    return jnp.concatenate([x1 * cos - x2 * sin, x2 * cos + x1 * sin], axis=0)


def _fox_proj_kernel(h_ref, ng_ref, wq_ref, wk_ref, wv_ref, wg_ref, wf_ref, bf_ref, qg_ref, kg_ref,
                     tri_ref, bound_ref, qa_ref, ka_ref, vt_ref, gt_ref, cum_ref, carry_ref):
    t = pl.program_id(1)
    u = _rms_rows(h_ref[0], ng_ref[...]).astype(BF16)

    z = _nt(wf_ref[...], u) + bf_ref[...]
    logf = jnp.minimum(z, 0.0) - jnp.log(1.0 + jnp.exp(-jnp.abs(z)))
    parts = jnp.concatenate(_split3(logf), axis=0)
    c3 = _nn(parts, tri_ref[...])
    local = c3[0:FOX_HEADS] + c3[FOX_HEADS:2 * FOX_HEADS] + c3[2 * FOX_HEADS:3 * FOX_HEADS]

    @pl.when(t == 0)
    def _():
        carry_ref[...] = jnp.zeros_like(carry_ref)

    cum = local + carry_ref[:, 0:1]
    carry_ref[...] = jnp.broadcast_to(cum[:, TM - 1:TM], carry_ref.shape)
    c_hi, c_mid, c_lo = [p.astype(F32) for p in _split3(cum * LOG2E)]
    cum_ref[0] = cum * LOG2E
    d_hi, d_mid, d_lo = [p.astype(F32) for p in _split3(cum * LOG2E + bound_ref[...])]

    row = lax.broadcasted_iota(jnp.int32, (16, TM), 0)
    ones_row = jnp.where(row == 0, 1.0, 0.0).astype(BF16)
    zeros48 = jnp.zeros((QK_PAD - FOX_DIM - 16, TM), F32)
    qg = qg_ref[...]
    kg = kg_ref[...]
    n_chunk = FOX_WIDTH // 256
    for c in range(n_chunk):
        rows = slice(c * 256, (c + 1) * 256)
        qc = _nt(wq_ref[rows, :], u)
        kc = _nt(wk_ref[rows, :], u)
        vc = _nt(wv_ref[rows, :], u)
        gc = _nt(wg_ref[rows, :], u)
        gt_ref[0, rows, :] = gc.astype(BF16)
        for j in range(4):
            hd = c * 4 + j
            hs = slice(j * FOX_DIM, (j + 1) * FOX_DIM)
            hi, mid, lo = c_hi[hd:hd + 1], c_mid[hd:hd + 1], c_lo[hd:hd + 1]
            khi, kmid, klo = d_hi[hd:hd + 1], d_mid[hd:hd + 1], d_lo[hd:hd + 1]
            qh = qc[hs]
            qn = qh * lax.rsqrt(jnp.mean(qh * qh, axis=0, keepdims=True) + EPS) * qg
            exq = jnp.where(row < 3, 1.0, jnp.where(row == 3, hi, jnp.where(row == 4, mid,
                            jnp.where(row == 5, lo, 0.0))))
            qa_ref[0, hd, 0:FOX_DIM, :] = qn.astype(BF16)
            qa_ref[0, hd, FOX_DIM:FOX_DIM + 16, :] = exq.astype(BF16)
            qa_ref[0, hd, FOX_DIM + 16:QK_PAD, :] = zeros48.astype(BF16)
            kh = kc[hs]
            kn = kh * lax.rsqrt(jnp.mean(kh * kh, axis=0, keepdims=True) + EPS) * kg
            exk = jnp.where(row == 0, -khi, jnp.where(row == 1, -kmid, jnp.where(row == 2, -klo,
                            jnp.where(row < 6, 1.0, 0.0))))
            kfull = jnp.concatenate([kn, exk, zeros48], axis=0)
            _store_kv_blocks(ka_ref, vt_ref, hd, kfull, vc[hs], ones_row)


def _fox_proj(h, ng, wq, wk, wv, wg, wf, bfc, qg, kg, tri, bound):
    B, S, _ = h.shape
    nt = S // TM
    full = lambda shape: pl.BlockSpec(shape, lambda b, t: (0,) * len(shape))
    return pl.pallas_call(
        _fox_proj_kernel,
        grid=(B, nt),
        in_specs=[
            pl.BlockSpec((1, TM, D_MODEL), lambda b, t: (b, t, 0)),
            full((1, D_MODEL)),
            full((FOX_WIDTH, D_MODEL)), full((FOX_WIDTH, D_MODEL)),
            full((FOX_WIDTH, D_MODEL)), full((FOX_WIDTH, D_MODEL)),
            full((FOX_HEADS, D_MODEL)), full((FOX_HEADS, 1)),
            full((FOX_DIM, 1)), full((FOX_DIM, 1)),
            full((TM, TM)), full((1, 1)),
        ],
        out_specs=[
            pl.BlockSpec((1, FOX_HEADS, QK_PAD, TM), lambda b, t: (b, 0, 0, t)),
            pl.BlockSpec((1, FOX_HEADS, TM // TK, TK, QK_PAD), lambda b, t: (b, 0, t, 0, 0)),
            pl.BlockSpec((1, FOX_HEADS, TM // TK, V_PAD, TK), lambda b, t: (b, 0, t, 0, 0)),
            pl.BlockSpec((1, FOX_WIDTH, TM), lambda b, t: (b, 0, t)),
            pl.BlockSpec((1, FOX_HEADS, TM), lambda b, t: (b, 0, t)),
        ],
        out_shape=[
            jax.ShapeDtypeStruct((B, FOX_HEADS, QK_PAD, S), BF16),
            jax.ShapeDtypeStruct((B, FOX_HEADS, S // TK, TK, QK_PAD), BF16),
            jax.ShapeDtypeStruct((B, FOX_HEADS, S // TK, V_PAD, TK), BF16),
            jax.ShapeDtypeStruct((B, FOX_WIDTH, S), BF16),
            jax.ShapeDtypeStruct((B, FOX_HEADS, S), F32),
        ],
        scratch_shapes=[pltpu.VMEM((FOX_HEADS, 128), F32)],
        compiler_params=pltpu.CompilerParams(
            dimension_semantics=("parallel", "arbitrary"), vmem_limit_bytes=VMEM_LIMIT),
        name="fox_proj",
    )(h, ng, wq, wk, wv, wg, wf, bfc, qg, kg, tri, bound)


def _ab_proj_kernel(h_ref, ng_ref, wcq_ref, wckv_ref, wkr_ref, wg_ref, wqb_ref, wkb_ref, wvb_ref,
                    qnorm_ref, wuq_ref, kvnorm_ref, wukv_ref, qg_ref, kg_ref, cqg_ref, ckg_ref,
                    cos_ref, sin_ref, bound_ref,
                    qt_ref, k_ref, vt_ref, gt_ref, qbt_ref, kb_ref, vbt_ref):
    u = _rms_rows(h_ref[0], ng_ref[...]).astype(BF16)
    cos = cos_ref[0]
    sin = sin_ref[0]

    cqn = _rms_rows(_nn(u, wcq_ref[...]), qnorm_ref[...]).astype(BF16)
    ckvn = _rms_rows(_nn(u, wckv_ref[...]), kvnorm_ref[...]).astype(BF16)
    krt = _nt(wkr_ref[...], u)
    ss_kr = jnp.sum(krt * krt, axis=0, keepdims=True)
    qg = qg_ref[...]
    kg = kg_ref[...]
    row32 = lax.broadcasted_iota(jnp.int32, (QK_PAD - MLA_QK, TM), 0)
    q_pad = jnp.where(row32 == 0, 1.0, 0.0)
    k_pad = jnp.where(row32 == 0, -bound_ref[...], 0.0)
    ones_row =jnp.where(lax.broadcasted_iota(jnp.int32, (V_PAD - MLA_V, TM), 0) == 0, 1.0, 0.0).astype(BF16)
    for hd in range(MLA_HEADS):
        qh = _nt(wuq_ref[hd * MLA_QK:(hd + 1) * MLA_QK, :], cqn)
        qn = qh * lax.rsqrt(jnp.mean(qh * qh, axis=0, keepdims=True) + EPS) * qg
        qt_ref[0, hd, 0:MLA_NOPE, :] = qn[0:MLA_NOPE].astype(BF16)
        qt_ref[0, hd, MLA_NOPE:MLA_QK, :] = _rope_t(qn[MLA_NOPE:MLA_QK], cos, sin).astype(BF16)
        qt_ref[0, hd, MLA_QK:QK_PAD, :] = q_pad.astype(BF16)
        kv = _nt(wukv_ref[hd * 128:(hd + 1) * 128, :], ckvn)
        kn = kv[0:MLA_NOPE]
        rk = lax.rsqrt((jnp.sum(kn * kn, axis=0, keepdims=True) + ss_kr) * (1.0 / MLA_QK) + EPS)
        kfull = jnp.concatenate([kn * rk * kg[0:MLA_NOPE],
                                 _rope_t(krt * rk * kg[MLA_NOPE:MLA_QK], cos, sin),
                                 k_pad], axis=0)
        _store_kv_blocks(k_ref, vt_ref, hd, kfull, kv[MLA_NOPE:MLA_NOPE + MLA_V], ones_row)

    for c in range((MLA_WIDTH + CHK_WIDTH) // 256):
        rows = slice(c * 256, (c + 1) * 256)
        gt_ref[0, rows, :] = _nt(wg_ref[rows, :], u).astype(BF16)

    cqg = cqg_ref[...]
    ckg = ckg_ref[...]
    zeros64 = jnp.zeros((CHK_DIM, TM), F32)
    for c in range(CHK_WIDTH // 256):
        rows = slice(c * 256, (c + 1) * 256)
        qc = _nt(wqb_ref[rows, :], u)
        kc = _nt(wkb_ref[rows, :], u)
        vc = _nt(wvb_ref[rows, :], u).astype(BF16)
        for j in range(4):
            vbt_ref[0, c * 4 + j, 0:CHK_DIM, :] = vc[j * CHK_DIM:(j + 1) * CHK_DIM]
            vbt_ref[0, c * 4 + j, CHK_DIM:V_PAD, :] = ones_row
        kns = []
        for j in range(4):
            hd = c * 4 + j
            hs = slice(j * CHK_DIM, (j + 1) * CHK_DIM)
            qh = qc[hs]
            qn = (qh * lax.rsqrt(jnp.mean(qh * qh, axis=0, keepdims=True) + EPS) * cqg).astype(BF16)
            lo, hi = (0, CHK_DIM) if hd % 2 == 0 else (CHK_DIM, 2 * CHK_DIM)
            qbt_ref[0, hd, lo:hi, :] = qn
            qbt_ref[0, hd, CHK_DIM - lo:2 * CHK_DIM - lo, :] = zeros64.astype(BF16)
            kh = kc[hs]
            kns.append(kh * lax.rsqrt(jnp.mean(kh * kh, axis=0, keepdims=True) + EPS) * ckg)
        for pr in range(2):
            pair = jnp.concatenate([kns[2 * pr], kns[2 * pr + 1]], axis=0)
            kb_ref[0, c * 2 + pr] = pair.T.astype(BF16)


def _ab_proj(h, ng, wcq, wckv, wkr, wg, wqb, wkb, wvb, qnorm, wuq, kvnorm, wukv, qg, kg, cqg, ckg,
             cos, sin, bound):
    B, S, _ = h.shape
    nt = S // TM
    full = lambda shape: pl.BlockSpec(shape, lambda b, t: (0,) * len(shape))
    return pl.pallas_call(
        _ab_proj_kernel,
        grid=(B, nt),
        in_specs=[
            pl.BlockSpec((1, TM, D_MODEL), lambda b, t: (b, t, 0)),
            full((1, D_MODEL)),
            full((D_MODEL, MLA_Q_LORA)), full((D_MODEL, MLA_KV_LORA)), full((MLA_ROPE, D_MODEL)),
            full((MLA_WIDTH + CHK_WIDTH, D_MODEL)),
            full((CHK_WIDTH, D_MODEL)), full((CHK_WIDTH, D_MODEL)), full((CHK_WIDTH, D_MODEL)),
            full((1, MLA_Q_LORA)), full((MLA_HEADS * MLA_QK, MLA_Q_LORA)),
            full((1, MLA_KV_LORA)), full((MLA_HEADS * 128, MLA_KV_LORA)),
            full((MLA_QK, 1)), full((MLA_QK, 1)), full((CHK_DIM, 1)), full((CHK_DIM, 1)),
            pl.BlockSpec((1, MLA_ROPE // 2, TM), lambda b, t: (b, 0, t)),
            pl.BlockSpec((1, MLA_ROPE // 2, TM), lambda b, t: (b, 0, t)),
            full((1, 1)),
        ],
        out_specs=[
            pl.BlockSpec((1, MLA_HEADS, QK_PAD, TM), lambda b, t: (b, 0, 0, t)),
            pl.BlockSpec((1, MLA_HEADS, TM // TK, TK, QK_PAD), lambda b, t: (b, 0, t, 0, 0)),
            pl.BlockSpec((1, MLA_HEADS, TM // TK, V_PAD, TK), lambda b, t: (b, 0, t, 0, 0)),
            pl.BlockSpec((1, MLA_WIDTH + CHK_WIDTH, TM), lambda b, t: (b, 0, t)),
            pl.BlockSpec((1, CHK_HEADS, 2 * CHK_DIM, TM), lambda b, t: (b, 0, 0, t)),
            pl.BlockSpec((1, CHK_HEADS // 2, TM, 2 * CHK_DIM), lambda b, t: (b, 0, t, 0)),
            pl.BlockSpec((1, CHK_HEADS, V_PAD, TM), lambda b, t: (b, 0, 0, t)),
        ],
        out_shape=[
            jax.ShapeDtypeStruct((B, MLA_HEADS, QK_PAD, S), BF16),
            jax.ShapeDtypeStruct((B, MLA_HEADS, S // TK, TK, QK_PAD), BF16),
            jax.ShapeDtypeStruct((B, MLA_HEADS, S // TK, V_PAD, TK), BF16),
            jax.ShapeDtypeStruct((B, MLA_WIDTH + CHK_WIDTH, S), BF16),
            jax.ShapeDtypeStruct((B, CHK_HEADS, 2 * CHK_DIM, S), BF16),
            jax.ShapeDtypeStruct((B, CHK_HEADS // 2, S, 2 * CHK_DIM), BF16),
            jax.ShapeDtypeStruct((B, CHK_HEADS, V_PAD, S), BF16),
        ],
        compiler_params=pltpu.CompilerParams(
            dimension_semantics=("parallel", "parallel"), vmem_limit_bytes=VMEM_LIMIT),
        name="ab_proj",
    )(h, ng, wcq, wckv, wkr, wg, wqb, wkb, wvb, qnorm, wuq, kvnorm, wukv, qg, kg, cqg, ckg, cos, sin,
      bound)


def _causal_attn_kernel(qt_ref, k_ref, vt_ref, o_ref, *, gran_shift):
    qi = pl.program_id(2)
    dv = o_ref.shape[1]
    n_sub = TQ // TK
    qts = [qt_ref[0, 0, :, c * TK:(c + 1) * TK] for c in range(n_sub)]

    def scores(c, j, masked):
        s = _nn(k_ref[0, 0, j], qts[c])
        if masked:
            kpos = lax.broadcasted_iota(jnp.int32, (TK, TK), 0)
            qpos = lax.broadcasted_iota(jnp.int32, (TK, TK), 1)
            s = jnp.where((kpos >> gran_shift) <= (qpos >> gran_shift), s, NEG_INF)
        return s

    def update(s, j, carry):
        m, acc = carry
        m_new = jnp.maximum(m, jnp.max(s, axis=0, keepdims=True))
        p = jnp.exp2(s - m_new)
        acc = jnp.exp2(m - m_new) * acc + _nn(vt_ref[0, 0, j], p.astype(BF16))
        return m_new, acc

    def run(work, carries):
        carries = list(carries)
        pending = [scores(*w) for w in work[:LOOKAHEAD]]
        for i, (c, j, _) in enumerate(work):
            if i + LOOKAHEAD < len(work):
                pending.append(scores(*work[i + LOOKAHEAD]))
            carries[c] = update(pending.pop(0), j, carries[c])
        return tuple(carries)

    def steps(jj, carries):
        return run([(c, jj * n_sub + u, False) for u in range(n_sub) for c in range(n_sub)], carries)

    init = (jnp.full((1, TK), NEG_INF, F32), jnp.zeros((V_PAD, TK), F32))
    carries = lax.fori_loop(0, qi, steps, (init,) * n_sub)
    carries = run([(c, qi * n_sub + u, u == c) for c in range(n_sub) for u in range(c + 1)], carries)
    for c in range(n_sub):
        acc = carries[c][1]
        o_ref[0, :, c * TK:(c + 1) * TK] = (acc[0:dv] / acc[dv:dv + 1]).astype(o_ref.dtype)


def _shifted_attn_kernel(first_ref, qt_ref, k_ref, vt_ref, o_ref, *, gran_shift):
    qi = pl.program_id(2)
    g0 = first_ref[(pl.program_id(0) * pl.num_programs(1) + pl.program_id(1)) * pl.num_programs(2) + qi]
    dv = o_ref.shape[1]
    nb = TQF // TK
    qt = qt_ref[0, 0]

    def probs(j, lo, masked):
        s = _nn(k_ref[0, 0, j], qt[:, lo:])
        if masked:
            kpos = lax.broadcasted_iota(jnp.int32, s.shape, 0)
            qpos = lax.broadcasted_iota(jnp.int32, s.shape, 1)
            s = jnp.where((kpos >> gran_shift) <= (qpos >> gran_shift), s, NEG_INF)
        return jnp.exp2(s).astype(BF16)

    def run(work, acc):
        pending = [probs(*w) for w in work[:LOOKAHEAD]]
        for i, (j, lo, _) in enumerate(work):
            if i + LOOKAHEAD < len(work):
                pending.append(probs(*work[i + LOOKAHEAD]))
            pv = _nn(vt_ref[0, 0, j], pending.pop(0))
            acc = acc + pv if lo == 0 else jnp.concatenate([acc[:, :lo], acc[:, lo:] + pv], axis=1)
        return acc

    def full_groups(n_grp):
        return lambda jj, a: run([((g0 + jj * n_grp) * nb + u, 0, False) for u in range(n_grp * nb)], a)

    n_loop = jnp.maximum(qi - 1 - g0, 0)
    n_long = n_loop // LONG_STEP
    acc = lax.fori_loop(0, n_long, full_groups(LONG_STEP), jnp.zeros((V_PAD, TQF), F32))
    acc = lax.fori_loop(n_long * LONG_STEP, n_loop, full_groups(1), acc)
    diag = [(qi * nb + u, u * TK, True) for u in range(nb)]
    acc = lax.cond(qi > 0,
                   lambda a: run([((qi - 1) * nb + u, 0, False) for u in range(nb)] + diag, a),
                   lambda a: run(diag, a), acc)
    o_ref[0] = (acc[0:dv] / acc[dv:dv + 1]).astype(o_ref.dtype)


def _causal_attn(qt, k, vt, gran, bound, first_group):
    B, H, _, S = qt.shape
    nk, dv = k.shape[2], FOX_DIM
    assert TQ % TK == 0 and TQF % TK == 0 and MLA_V == FOX_DIM and vt.shape[3] == V_PAD

    def call(body, tq, name, prefetch):
        grid_spec = pltpu.PrefetchScalarGridSpec(
            num_scalar_prefetch=len(prefetch),
            grid=(B, H, S // tq),
            in_specs=[
                pl.BlockSpec((1, 1, QK_PAD, tq), lambda b, h, q, *_: (b, h, 0, q)),
                pl.BlockSpec((1, 1, nk, TK, QK_PAD), lambda b, h, q, *_: (b, h, 0, 0, 0)),
                pl.BlockSpec((1, 1, nk, V_PAD, TK), lambda b, h, q, *_: (b, h, 0, 0, 0)),
            ],
            out_specs=pl.BlockSpec((1, dv, tq), lambda b, h, q, *_: (b, h, q)),
        )
        return pl.pallas_call(
            functools.partial(body, gran_shift=int(np.log2(gran))),
            grid_spec=grid_spec,
            out_shape=jax.ShapeDtypeStruct((B, H * dv, S), BF16),
            compiler_params=pltpu.CompilerParams(
                dimension_semantics=("parallel", "parallel", "arbitrary"), vmem_limit_bytes=VMEM_LIMIT),
            name=name,
        )(*prefetch, qt, k, vt)

    return lax.cond(bound <= MAX_SHIFT_BOUND,
                    lambda: call(_shifted_attn_kernel, TQF, "shifted_attn", (first_group.reshape(-1),)),
                    lambda: call(_causal_attn_kernel, TQ, "causal_attn", ()))


def _chunk_attn_kernel(qt_ref, k0_ref, k1_ref, k2_ref, v0_ref, v1_ref, v2_ref, bias_ref, o_ref, *,
                       shifted):
    t = pl.program_id(1)
    k_refs = (k0_ref, k1_ref, k2_ref)
    v_refs = (v0_ref, v1_ref, v2_ref)

    def scores(hd):
        qt = qt_ref[0, hd]
        ss = []
        for d in range(3):
            s = _nn(k_refs[d][0, hd // 2], qt) + bias_ref[hd, d * TC:(d + 1) * TC, :]
            if d < 2:
                s = jnp.where(t + (d - 2) >= 0, s, NEG_INF)
            ss.append(s)
        return ss

    def finish(hd, ss):
        if not shifted:
            m = jnp.max(jnp.maximum(jnp.maximum(ss[0], ss[1]), ss[2]), axis=0, keepdims=True)
            ss = [s - m for s in ss]
        p = jnp.concatenate([jnp.exp2(s).astype(BF16) for s in ss], axis=0)
        vt = jnp.concatenate([r[0, hd] for r in v_refs], axis=1)
        acc = _nn(vt, p)
        o_ref[0, hd * CHK_DIM:(hd + 1) * CHK_DIM, :] = (
            acc[0:CHK_DIM] / acc[CHK_DIM:CHK_DIM + 1]).astype(o_ref.dtype)

    pending = [scores(0)]
    for hd in range(CHK_HEADS):
        if hd + 1 < CHK_HEADS:
            pending.append(scores(hd + 1))
        finish(hd, pending.pop(0))


def _chunk_attn(qbt, kb, vbt, bias_t, spread):
    B, _, _, S = qbt.shape
    kspec = lambda d: pl.BlockSpec((1, CHK_HEADS // 2, TC, 2 * CHK_DIM),
                                   lambda b, t: (b, 0, jnp.maximum(t + (d - 2), 0), 0))
    vspec = lambda d: pl.BlockSpec((1, CHK_HEADS, V_PAD, TC),
                                   lambda b, t: (b, 0, 0, jnp.maximum(t + (d - 2), 0)))

    def call(shifted):
        return pl.pallas_call(
            functools.partial(_chunk_attn_kernel, shifted=shifted),
            grid=(B, S // TC),
            in_specs=[
                pl.BlockSpec((1, CHK_HEADS, 2 * CHK_DIM, TC), lambda b, t: (b, 0, 0, t)),
                kspec(0), kspec(1), kspec(2), vspec(0), vspec(1), vspec(2),
                pl.BlockSpec((CHK_HEADS, 3 * TC, TC), lambda b, t: (0, 0, 0)),
            ],
            out_specs=pl.BlockSpec((1, CHK_WIDTH, TC), lambda b, t: (b, 0, t)),
            out_shape=jax.ShapeDtypeStruct((B, CHK_WIDTH, S), BF16),
            compiler_params=pltpu.CompilerParams(
                dimension_semantics=("parallel", "parallel"), vmem_limit_bytes=VMEM_LIMIT),
            name="chunk_attn_shifted" if shifted else "chunk_attn",
        )(qbt, kb, kb, kb, vbt, vbt, vbt, bias_t)

    return lax.cond(spread <= 2 * MAX_SHIFT_BOUND, lambda: call(True), lambda: call(False))


def _out_ple_kernel(*refs, n_o):
    o_refs = refs[:n_o]
    gt_ref, h_ref, p_ref, wo_ref, pgn_ref, wgate_ref, pew_ref, out_ref = refs[n_o:]
    ot = jnp.concatenate([r[0] for r in o_refs], axis=0) if n_o > 1 else o_refs[0][0]
    g = gt_ref[0].astype(F32)
    og = (ot.astype(F32) * (g * _sigmoid(g))).astype(BF16)
    mixed_t = _nn(wo_ref[...], og)
    h1 = h_ref[0] + mixed_t.T
    a = _rms_rows(h1, pgn_ref[...]).astype(BF16)
    gate = _sigmoid(_nn(a, wgate_ref[...]))
    pe = _nn(p_ref[0, 0].astype(BF16), pew_ref[...])
    out_ref[0] = h1 + pe * gate


def _out_ple(o_list, gt, h, p, layer, wo_t, pgn, wgate, pew):
    B, S, _ = h.shape
    full = lambda shape: pl.BlockSpec(shape, lambda b, t: (0,) * len(shape))
    width = gt.shape[1]
    return pl.pallas_call(
        functools.partial(_out_ple_kernel, n_o=len(o_list)),
        grid=(B, S // TM),
        in_specs=[pl.BlockSpec((1, o.shape[1], TM), lambda b, t: (b, 0, t)) for o in o_list] + [
            pl.BlockSpec((1, width, TM), lambda b, t: (b, 0, t)),
            pl.BlockSpec((1, TM, D_MODEL), lambda b, t: (b, t, 0)),
            pl.BlockSpec((1, 1, TM, PLE_DIM), lambda b, t: (layer, b, t, 0)),
            full((D_MODEL, width)), full((1, D_MODEL)), full((D_MODEL, D_MODEL)),
            full((PLE_DIM, D_MODEL)),
        ],
        out_specs=pl.BlockSpec((1, TM, D_MODEL), lambda b, t: (b, t, 0)),
        out_shape=jax.ShapeDtypeStruct((B, S, D_MODEL), F32),
        compiler_params=pltpu.CompilerParams(
            dimension_semantics=("parallel", "parallel"), vmem_limit_bytes=VMEM_LIMIT),
        name="out_ple",
    )(*o_list, gt, h, p, wo_t, pgn, wgate, pew)


def _col(v):
    return v.astype(F32).reshape(-1, 1)


def _score_bound(qg, kg, d):
    return 1.02 * d * jnp.max(jnp.abs(qg)) * jnp.max(jnp.abs(kg)) + 1.0


def _first_live_group(cum2):
    nb = TQF // TK
    first_q = cum2[:, :, ::TQF]
    last_k = cum2[:, :, TK - 1::TK]
    dead = last_k[:, :, None, :] > first_q[:, :, :, None] + UNDERFLOW_LOG2
    return (jnp.sum(dead, axis=-1) // nb).astype(jnp.int32)


def _chunk_bias_table(rel_bias, bound_qk):
    n = 4 * TC
    i = np.arange(n)
    dist = np.where(i < TC, i, i - n) + 2 * TC
    rb = rel_bias.astype(F32) * LOG2E
    spread = 2.0 * bound_qk + (jnp.max(rb) - jnp.min(rb))
    e = rb[:, np.clip(dist, -REL_CLIP, REL_CLIP) + REL_CLIP] - (bound_qk + jnp.max(rb))
    h = rel_bias.shape[0]
    return _bias_table(e.reshape(h, 1, n)), spread


def _bias_table(e):
    h, _, n = e.shape
    return pl.pallas_call(
        _bias_table_kernel,
        grid=(h,),
        in_specs=[pl.BlockSpec((1, 1, n), lambda hd: (hd, 0, 0))],
        out_specs=pl.BlockSpec((1, 3 * TC, TC), lambda hd: (hd, 0, 0)),
        out_shape=jax.ShapeDtypeStruct((h, 3 * TC, TC), F32),
        name="bias_table",
    )(e)


def _bias_table_kernel(e_ref, o_ref):
    n = e_ref.shape[2]
    rows = jnp.broadcast_to(e_ref[0], (3 * TC, n))
    skew = pltpu.roll(rows, 0, 1, stride=1, stride_axis=0)
    kchunk = lax.broadcasted_iota(jnp.int32, (3 * TC, TC), 0) // CHUNK
    qchunk = lax.broadcasted_iota(jnp.int32, (3 * TC, TC), 1) // CHUNK + (2 * TC) // CHUNK
    valid = (kchunk <= qchunk) & (kchunk >= qchunk - LEFT_CHUNKS)
    o_ref[0] = jnp.where(valid, skew[:, 0:TC], NEG_INF)


def kernel(x, p, positions, norm_g, ab_w_in, mla_q_norm, mla_w_uq, mla_kv_norm, mla_w_ukv, mla_q_gain, mla_k_gain, chk_q_gain, chk_k_gain, chk_rel_bias, ab_w_out, fox_w_in, fox_b_f, fox_q_gain, fox_k_gain, fox_w_out, pe_w, pe_gate_norm, pe_gate_w):
    B, S, _ = x.shape
    half = MLA_ROPE // 2
    inv_freq = 1.0 / (ROPE_THETA ** (jnp.arange(half, dtype=F32) / half))
    ang = positions.astype(F32)[:, None, :] * inv_freq[None, :, None]
    cos, sin = jnp.cos(ang), jnp.sin(ang)
    tri = (np.arange(TM)[:, None] <= np.arange(TM)[None, :]).astype(np.float32)
    tri = jnp.asarray(tri, BF16)

    h = x
    for i in range(DEPTH):
        l = i // 2
        ng = norm_g[i].astype(F32).reshape(1, -1)
        if i % 2 == 0:
            w = ab_w_in[l]
            o = np.cumsum((0, MLA_Q_LORA, MLA_KV_LORA, MLA_ROPE, MLA_WIDTH) + (CHK_WIDTH,) * 4)
            wcq = w[:, o[0]:o[1]].astype(BF16)
            wckv = w[:, o[1]:o[2]].astype(BF16)
            wkr = w[:, o[2]:o[3]].T.astype(BF16)
            wg = jnp.concatenate([w[:, o[3]:o[4]], w[:, o[7]:o[8]]], axis=1).T.astype(BF16)
            wqb = w[:, o[4]:o[5]].T.astype(BF16)
            wkb = w[:, o[5]:o[6]].T.astype(BF16)
            wvb = w[:, o[6]:o[7]].T.astype(BF16)
            qg, kg = _col(mla_q_gain[l]) * (MLA_QK ** -0.5 * LOG2E), _col(mla_k_gain[l])
            bound = _score_bound(qg, kg, MLA_QK)
            cqg, ckg = _col(chk_q_gain[l]) * (CHK_DIM ** -0.5 * LOG2E), _col(chk_k_gain[l])
            qt, k, vt, gt, qbt, kb, vbt = _ab_proj(
                h, ng, wcq, wckv, wkr, wg, wqb, wkb, wvb,
                mla_q_norm[l].astype(F32).reshape(1, -1), mla_w_uq[l].T.astype(BF16),
                mla_kv_norm[l].astype(F32).reshape(1, -1), mla_w_ukv[l].T.astype(BF16),
                qg, kg, cqg, ckg, cos, sin, bound.reshape(1, 1))
            o_a = _causal_attn(qt, k, vt, CHUNK, bound,
                               jnp.zeros((B, MLA_HEADS, S // TQF), jnp.int32))
            bias_t, spread = _chunk_bias_table(chk_rel_bias[l], _score_bound(cqg, ckg, CHK_DIM))
            o_b = _chunk_attn(qbt, kb, vbt, bias_t, spread)
            o_list, wo = [o_a, o_b], ab_w_out[l]
        else:
            w = fox_w_in[l]
            o = np.cumsum((0,) + (FOX_WIDTH,) * 4 + (FOX_HEADS,))
            wq, wk, wv, wgc, wf = [w[:, o[n]:o[n + 1]].T.astype(BF16) for n in range(5)]
            qg, kg = _col(fox_q_gain[l]) * (FOX_DIM ** -0.5 * LOG2E), _col(fox_k_gain[l])
            bound = _score_bound(qg, kg, FOX_DIM)
            qa, ka, vt, gt, cum2 = _fox_proj(h, ng, wq, wk, wv, wgc, wf, _col(fox_b_f[l]), qg, kg, tri,
                                             bound.reshape(1, 1))
            o_list, wo = [_causal_attn(qa, ka, vt, 1, bound, _first_live_group(cum2))], fox_w_out[l]
        h = _out_ple(o_list, gt, h, p, i, wo.T.astype(BF16),
                     pe_gate_norm[i].astype(F32).reshape(1, -1),
                     pe_gate_w[i].astype(BF16), pe_w[i].astype(BF16))
    return h
```

```python
import functools

import numpy as np
import jax
import jax.numpy as jnp
from jax import lax
from jax.experimental import pallas as pl
from jax.experimental.pallas import tpu as pltpu

F32 = jnp.float32
BF16 = jnp.bfloat16

D_MODEL = 1024
DEPTH = 4
CHUNK = 64
PLE_DIM = 256
EPS = 1e-6
NEG_INF = -1e30
MLA_HEADS = 8
MLA_NOPE = 64
MLA_ROPE = 32
MLA_QK = MLA_NOPE + MLA_ROPE
MLA_V = 64
MLA_Q_LORA = 384
MLA_KV_LORA = 256
ROPE_THETA = 10000.0
CHK_HEADS = 8
CHK_DIM = 64
LEFT_CHUNKS = 8
REL_CLIP = 256
FOX_HEADS = 16
FOX_DIM = 64
MLA_WIDTH = MLA_HEADS * MLA_V
CHK_WIDTH = CHK_HEADS * CHK_DIM
FOX_WIDTH = FOX_HEADS * FOX_DIM

TM = 512
TQ = 512
TQF = 1024
LONG_STEP = 2
UNDERFLOW_LOG2 = 160.0
MAX_SHIFT_BOUND = 45.0
TK = 256
TC = 256
OUT_SUB = 256
PROJ_ROWS = 256
QK_PAD = 128
V_PAD = 80
LOOKAHEAD = 1
LOG2E = 1.4426950408889634
VMEM_LIMIT = 56 * 1024 * 1024


def _nt(a, b):
    return lax.dot_general(a, b, (((1,), (1,)), ((), ())), preferred_element_type=F32)


def _nn(a, b):
    return jnp.dot(a, b, preferred_element_type=F32)


def _sigmoid(x):
    return 1.0 / (1.0 + jnp.exp(-x))


def _rms_rows(x, g):
    ms = jnp.mean(x * x, axis=-1, keepdims=True)
    return x * lax.rsqrt(ms + EPS) * g


def _split3(x):
    hi = x.astype(BF16)
    r1 = x - hi.astype(F32)
    mid = r1.astype(BF16)
    lo = (r1 - mid.astype(F32)).astype(BF16)
    return hi, mid, lo


def _store_kv_blocks(k_ref, vt_ref, hd, k_t, v_t, ones_row):
    k_rows = k_t.T.astype(BF16)
    v_aug = jnp.concatenate([v_t.astype(BF16), ones_row], axis=0)
    for c in range(TM // TK):
        k_ref[0, hd, c] = k_rows[c * TK:(c + 1) * TK]
        vt_ref[0, hd, c] = v_aug[:, c * TK:(c + 1) * TK]


def _rope_t(x, cos, sin):
    half = x.shape[0] // 2
    x1, x2 = x[:half], x[half:]
    return jnp.concatenate([x1 * cos - x2 * sin, x2 * cos + x1 * sin], axis=0)


def _fox_proj_kernel(h_ref, ng_ref, wq_ref, wk_ref, wv_ref, wg_ref, bf_ref, qg_ref, kg_ref,
                     tri_ref, bound_ref, qa_ref, ka_ref, vt_ref, gt_ref, cum_ref, carry_ref):
    t = pl.program_id(1)
    u = _rms_rows(h_ref[0], ng_ref[...]).astype(BF16)

    g_last = _nt(wg_ref[FOX_WIDTH - PROJ_ROWS:FOX_WIDTH + FOX_HEADS, :], u)
    z = g_last[PROJ_ROWS:PROJ_ROWS + FOX_HEADS] + bf_ref[...]
    logf = jnp.minimum(z, 0.0) - jnp.log(1.0 + jnp.exp(-jnp.abs(z)))
    parts = jnp.concatenate(_split3(logf), axis=0)
    c3 = _nn(parts, tri_ref[...])
    local = c3[0:FOX_HEADS] + c3[FOX_HEADS:2 * FOX_HEADS] + c3[2 * FOX_HEADS:3 * FOX_HEADS]

    @pl.when(t == 0)
    def _():
        carry_ref[...] = jnp.zeros_like(carry_ref)

    cum = local + carry_ref[:, 0:1]
    carry_ref[...] = jnp.broadcast_to(cum[:, TM - 1:TM], carry_ref.shape)
    c_hi, c_mid, c_lo = [p.astype(F32) for p in _split3(cum * LOG2E)]
    cum_ref[0] = cum * LOG2E
    d_hi, d_mid, d_lo = [p.astype(F32) for p in _split3(cum * LOG2E + bound_ref[...])]

    row = lax.broadcasted_iota(jnp.int32, (16, TM), 0)
    ones_row = jnp.where(row == 0, 1.0, 0.0).astype(BF16)
    zeros48 = jnp.zeros((QK_PAD - FOX_DIM - 16, TM), F32)
    qg = qg_ref[...]
    kg = kg_ref[...]
    heads_per_chunk = PROJ_ROWS // FOX_DIM
    for c in range(FOX_WIDTH // PROJ_ROWS):
        rows = slice(c * PROJ_ROWS, (c + 1) * PROJ_ROWS)
        qc = _nt(wq_ref[rows, :], u)
        kc = _nt(wk_ref[rows, :], u)
        vc = _nt(wv_ref[rows, :], u)
        gc = g_last[0:PROJ_ROWS] if rows.stop == FOX_WIDTH else _nt(wg_ref[rows, :], u)
        gt_ref[0, rows, :] = gc.astype(BF16)
        for j in range(heads_per_chunk):
            hd = c * heads_per_chunk + j
            hs = slice(j * FOX_DIM, (j + 1) * FOX_DIM)
            hi, mid, lo = c_hi[hd:hd + 1], c_mid[hd:hd + 1], c_lo[hd:hd + 1]
            khi, kmid, klo = d_hi[hd:hd + 1], d_mid[hd:hd + 1], d_lo[hd:hd + 1]
            qh = qc[hs]
            qn = qh * lax.rsqrt(jnp.mean(qh * qh, axis=0, keepdims=True) + EPS) * qg
            exq = jnp.where(row < 3, 1.0, jnp.where(row == 3, hi, jnp.where(row == 4, mid,
                            jnp.where(row == 5, lo, 0.0))))
            qa_ref[0, hd, 0:FOX_DIM, :] = qn.astype(BF16)
            qa_ref[0, hd, FOX_DIM:FOX_DIM + 16, :] = exq.astype(BF16)
            qa_ref[0, hd, FOX_DIM + 16:QK_PAD, :] = zeros48.astype(BF16)
            kh = kc[hs]
            kn = kh * lax.rsqrt(jnp.mean(kh * kh, axis=0, keepdims=True) + EPS) * kg
            exk = jnp.where(row == 0, -khi, jnp.where(row == 1, -kmid, jnp.where(row == 2, -klo,
                            jnp.where(row < 6, 1.0, 0.0))))
            kfull = jnp.concatenate([kn, exk, zeros48], axis=0)
            _store_kv_blocks(ka_ref, vt_ref, hd, kfull, vc[hs], ones_row)


def _fox_proj(h, ng, wq, wk, wv, wg, bfc, qg, kg, tri, bound):
    B, S, _ = h.shape
    nt = S // TM
    full = lambda shape: pl.BlockSpec(shape, lambda b, t: (0,) * len(shape))
    return pl.pallas_call(
        _fox_proj_kernel,
        grid=(B, nt),
        in_specs=[
            pl.BlockSpec((1, TM, D_MODEL), lambda b, t: (b, t, 0)),
            full((1, D_MODEL)),
            full((FOX_WIDTH, D_MODEL)), full((FOX_WIDTH, D_MODEL)),
            full((FOX_WIDTH, D_MODEL)), full((FOX_WIDTH + FOX_HEADS, D_MODEL)),
            full((FOX_HEADS, 1)),
            full((FOX_DIM, 1)), full((FOX_DIM, 1)),
            full((TM, TM)), full((1, 1)),
        ],
        out_specs=[
            pl.BlockSpec((1, FOX_HEADS, QK_PAD, TM), lambda b, t: (b, 0, 0, t)),
            pl.BlockSpec((1, FOX_HEADS, TM // TK, TK, QK_PAD), lambda b, t: (b, 0, t, 0, 0)),
            pl.BlockSpec((1, FOX_HEADS, TM // TK, V_PAD, TK), lambda b, t: (b, 0, t, 0, 0)),
            pl.BlockSpec((1, FOX_WIDTH, TM), lambda b, t: (b, 0, t)),
            pl.BlockSpec((1, FOX_HEADS, TM), lambda b, t: (b, 0, t)),
        ],
        out_shape=[
            jax.ShapeDtypeStruct((B, FOX_HEADS, QK_PAD, S), BF16),
            jax.ShapeDtypeStruct((B, FOX_HEADS, S // TK, TK, QK_PAD), BF16),
            jax.ShapeDtypeStruct((B, FOX_HEADS, S // TK, V_PAD, TK), BF16),
            jax.ShapeDtypeStruct((B, FOX_WIDTH, S), BF16),
            jax.ShapeDtypeStruct((B, FOX_HEADS, S), F32),
        ],
        scratch_shapes=[pltpu.VMEM((FOX_HEADS, 128), F32)],
        compiler_params=pltpu.CompilerParams(
            dimension_semantics=("parallel", "arbitrary"), vmem_limit_bytes=VMEM_LIMIT),
        name="fox_proj",
    )(h, ng, wq, wk, wv, wg, bfc, qg, kg, tri, bound)


def _ab_proj_kernel(h_ref, ng_ref, wcq_ref, wckv_ref, wg_ref, wqb_ref, wkb_ref, wvb_ref,
                    qnorm_ref, wuq_ref, kvnorm_ref, wukv_ref, qg_ref, kg_ref, cqg_ref, ckg_ref,
                    cos_ref, sin_ref, bound_ref,
                    qt_ref, k_ref, vt_ref, gt_ref, qbt_ref, kb_ref, vbt_ref):
    u = _rms_rows(h_ref[0], ng_ref[...]).astype(BF16)
    cos = cos_ref[0]
    sin = sin_ref[0]

    cqn = _rms_rows(_nn(u, wcq_ref[...]), qnorm_ref[...]).astype(BF16)
    ckvn = _rms_rows(_nn(u, wckv_ref[...]), kvnorm_ref[...]).astype(BF16)
    n_gate = MLA_WIDTH + CHK_WIDTH
    for c in range(n_gate // 256 - 1):
        rows = slice(c * 256, (c + 1) * 256)
        gt_ref[0, rows, :] = _nt(wg_ref[rows, :], u).astype(BF16)
    last = _nt(wg_ref[n_gate - 256:n_gate + MLA_ROPE, :], u)
    gt_ref[0, n_gate - 256:n_gate, :] = last[0:256].astype(BF16)
    krt = last[256:256 + MLA_ROPE]
    ss_kr = jnp.sum(krt * krt, axis=0, keepdims=True)
    q_all = _nt(wuq_ref[...], cqn)
    kv_all = _nt(wukv_ref[...], ckvn)
    qg = qg_ref[...]
    kg = kg_ref[...]
    row32 = lax.broadcasted_iota(jnp.int32, (QK_PAD - MLA_QK, TM), 0)
    q_pad = jnp.where(row32 == 0, 1.0, 0.0)
    k_pad = jnp.where(row32 == 0, -bound_ref[...], 0.0)
    ones_row =jnp.where(lax.broadcasted_iota(jnp.int32, (V_PAD - MLA_V, TM), 0) == 0, 1.0, 0.0).astype(BF16)
    for hd in range(MLA_HEADS):
        qh = q_all[hd * MLA_QK:(hd + 1) * MLA_QK]
        qn = qh * lax.rsqrt(jnp.mean(qh * qh, axis=0, keepdims=True) + EPS) * qg
        qt_ref[0, hd, 0:MLA_NOPE, :] = qn[0:MLA_NOPE].astype(BF16)
        qt_ref[0, hd, MLA_NOPE:MLA_QK, :] = _rope_t(qn[MLA_NOPE:MLA_QK], cos, sin).astype(BF16)
        qt_ref[0, hd, MLA_QK:QK_PAD, :] = q_pad.astype(BF16)
        kv = kv_all[hd * 128:(hd + 1) * 128]
        kn = kv[0:MLA_NOPE]
        rk = lax.rsqrt((jnp.sum(kn * kn, axis=0, keepdims=True) + ss_kr) * (1.0 / MLA_QK) + EPS)
        kfull = jnp.concatenate([kn * rk * kg[0:MLA_NOPE],
                                 _rope_t(krt * rk * kg[MLA_NOPE:MLA_QK], cos, sin),
                                 k_pad], axis=0)
        _store_kv_blocks(k_ref, vt_ref, hd, kfull, kv[MLA_NOPE:MLA_NOPE + MLA_V], ones_row)

    cqg = cqg_ref[...]
    ckg = ckg_ref[...]
    zeros64 = jnp.zeros((CHK_DIM, TM), F32)
    for c in range(CHK_WIDTH // 256):
        rows = slice(c * 256, (c + 1) * 256)
        qc = _nt(wqb_ref[rows, :], u)
        kc = _nt(wkb_ref[rows, :], u)
        vc = _nt(wvb_ref[rows, :], u).astype(BF16)
        for j in range(4):
            vbt_ref[0, c * 4 + j, 0:CHK_DIM, :] = vc[j * CHK_DIM:(j + 1) * CHK_DIM]
            vbt_ref[0, c * 4 + j, CHK_DIM:V_PAD, :] = ones_row
        kns = []
        for j in range(4):
            hd = c * 4 + j
            hs = slice(j * CHK_DIM, (j + 1) * CHK_DIM)
            qh = qc[hs]
            qn = (qh * lax.rsqrt(jnp.mean(qh * qh, axis=0, keepdims=True) + EPS) * cqg).astype(BF16)
            lo, hi = (0, CHK_DIM) if hd % 2 == 0 else (CHK_DIM, 2 * CHK_DIM)
            qbt_ref[0, hd, lo:hi, :] = qn
            qbt_ref[0, hd, CHK_DIM - lo:2 * CHK_DIM - lo, :] = zeros64.astype(BF16)
            kh = kc[hs]
            kns.append(kh * lax.rsqrt(jnp.mean(kh * kh, axis=0, keepdims=True) + EPS) * ckg)
        for pr in range(2):
            pair = jnp.concatenate([kns[2 * pr], kns[2 * pr + 1]], axis=0)
            kb_ref[0, c * 2 + pr] = pair.T.astype(BF16)


def _ab_proj(h, ng, wcq, wckv, wg, wqb, wkb, wvb, qnorm, wuq, kvnorm, wukv, qg, kg, cqg, ckg,
             cos, sin, bound):
    B, S, _ = h.shape
    nt = S // TM
    full = lambda shape: pl.BlockSpec(shape, lambda b, t: (0,) * len(shape))
    return pl.pallas_call(
        _ab_proj_kernel,
        grid=(B, nt),
        in_specs=[
            pl.BlockSpec((1, TM, D_MODEL), lambda b, t: (b, t, 0)),
            full((1, D_MODEL)),
            full((D_MODEL, MLA_Q_LORA)), full((D_MODEL, MLA_KV_LORA)),
            full((MLA_WIDTH + CHK_WIDTH + MLA_ROPE, D_MODEL)),
            full((CHK_WIDTH, D_MODEL)), full((CHK_WIDTH, D_MODEL)), full((CHK_WIDTH, D_MODEL)),
            full((1, MLA_Q_LORA)), full((MLA_HEADS * MLA_QK, MLA_Q_LORA)),
            full((1, MLA_KV_LORA)), full((MLA_HEADS * 128, MLA_KV_LORA)),
            full((MLA_QK, 1)), full((MLA_QK, 1)), full((CHK_DIM, 1)), full((CHK_DIM, 1)),
            pl.BlockSpec((1, MLA_ROPE // 2, TM), lambda b, t: (b, 0, t)),
            pl.BlockSpec((1, MLA_ROPE // 2, TM), lambda b, t: (b, 0, t)),
            full((1, 1)),
        ],
        out_specs=[
            pl.BlockSpec((1, MLA_HEADS, QK_PAD, TM), lambda b, t: (b, 0, 0, t)),
            pl.BlockSpec((1, MLA_HEADS, TM // TK, TK, QK_PAD), lambda b, t: (b, 0, t, 0, 0)),
            pl.BlockSpec((1, MLA_HEADS, TM // TK, V_PAD, TK), lambda b, t: (b, 0, t, 0, 0)),
            pl.BlockSpec((1, MLA_WIDTH + CHK_WIDTH, TM), lambda b, t: (b, 0, t)),
            pl.BlockSpec((1, CHK_HEADS, 2 * CHK_DIM, TM), lambda b, t: (b, 0, 0, t)),
            pl.BlockSpec((1, CHK_HEADS // 2, TM, 2 * CHK_DIM), lambda b, t: (b, 0, t, 0)),
            pl.BlockSpec((1, CHK_HEADS, V_PAD, TM), lambda b, t: (b, 0, 0, t)),
        ],
        out_shape=[
            jax.ShapeDtypeStruct((B, MLA_HEADS, QK_PAD, S), BF16),
            jax.ShapeDtypeStruct((B, MLA_HEADS, S // TK, TK, QK_PAD), BF16),
            jax.ShapeDtypeStruct((B, MLA_HEADS, S // TK, V_PAD, TK), BF16),
            jax.ShapeDtypeStruct((B, MLA_WIDTH + CHK_WIDTH, S), BF16),
            jax.ShapeDtypeStruct((B, CHK_HEADS, 2 * CHK_DIM, S), BF16),
            jax.ShapeDtypeStruct((B, CHK_HEADS // 2, S, 2 * CHK_DIM), BF16),
            jax.ShapeDtypeStruct((B, CHK_HEADS, V_PAD, S), BF16),
        ],
        compiler_params=pltpu.CompilerParams(
            dimension_semantics=("parallel", "parallel"), vmem_limit_bytes=VMEM_LIMIT),
        name="ab_proj",
    )(h, ng, wcq, wckv, wg, wqb, wkb, wvb, qnorm, wuq, kvnorm, wukv, qg, kg, cqg, ckg, cos, sin,
      bound)


def _causal_attn_kernel(qt_ref, k_ref, vt_ref, o_ref, *, gran_shift):
    qi = pl.program_id(2)
    dv = o_ref.shape[1]
    n_sub = TQ // TK
    qts = [qt_ref[0, 0, :, c * TK:(c + 1) * TK] for c in range(n_sub)]

    def scores(c, j, masked):
        s = _nn(k_ref[0, 0, j], qts[c])
        if masked:
            kpos = lax.broadcasted_iota(jnp.int32, (TK, TK), 0)
            qpos = lax.broadcasted_iota(jnp.int32, (TK, TK), 1)
            s = jnp.where((kpos >> gran_shift) <= (qpos >> gran_shift), s, NEG_INF)
        return s

    def update(s, j, carry):
        m, acc = carry
        m_new = jnp.maximum(m, jnp.max(s, axis=0, keepdims=True))
        p = jnp.exp2(s - m_new)
        acc = jnp.exp2(m - m_new) * acc + _nn(vt_ref[0, 0, j], p.astype(BF16))
        return m_new, acc

    def run(work, carries):
        carries = list(carries)
        pending = [scores(*w) for w in work[:LOOKAHEAD]]
        for i, (c, j, _) in enumerate(work):
            if i + LOOKAHEAD < len(work):
                pending.append(scores(*work[i + LOOKAHEAD]))
            carries[c] = update(pending.pop(0), j, carries[c])
        return tuple(carries)

    def steps(jj, carries):
        return run([(c, jj * n_sub + u, False) for u in range(n_sub) for c in range(n_sub)], carries)

    init = (jnp.full((1, TK), NEG_INF, F32), jnp.zeros((V_PAD, TK), F32))
    carries = lax.fori_loop(0, qi, steps, (init,) * n_sub)
    carries = run([(c, qi * n_sub + u, u == c) for c in range(n_sub) for u in range(c + 1)], carries)
    for c in range(n_sub):
        acc = carries[c][1]
        o_ref[0, :, c * TK:(c + 1) * TK] = (acc[0:dv] / acc[dv:dv + 1]).astype(o_ref.dtype)


def _shifted_attn_kernel(first_ref, qt_ref, k_ref, vt_ref, o_ref, *, gran_shift):
    qi = pl.program_id(2)
    g0 = first_ref[(pl.program_id(0) * pl.num_programs(1) + pl.program_id(1)) * pl.num_programs(2) + qi]
    dv = o_ref.shape[1]
    nb = TQF // TK
    qt = qt_ref[0, 0]

    def probs(j, lo, masked):
        s = _nn(k_ref[0, 0, j], qt[:, lo:])
        if masked:
            kpos = lax.broadcasted_iota(jnp.int32, s.shape, 0)
            qpos = lax.broadcasted_iota(jnp.int32, s.shape, 1)
            s = jnp.where((kpos >> gran_shift) <= (qpos >> gran_shift), s, NEG_INF)
        return jnp.exp2(s).astype(BF16)

    def run(work, acc):
        pending = [probs(*w) for w in work[:LOOKAHEAD]]
        for i, (j, lo, _) in enumerate(work):
            if i + LOOKAHEAD < len(work):
                pending.append(probs(*work[i + LOOKAHEAD]))
            pv = _nn(vt_ref[0, 0, j], pending.pop(0))
            acc = acc + pv if lo == 0 else jnp.concatenate([acc[:, :lo], acc[:, lo:] + pv], axis=1)
        return acc

    def full_groups(n_grp):
        return lambda jj, a: run([((g0 + jj * n_grp) * nb + u, 0, False) for u in range(n_grp * nb)], a)

    n_loop = jnp.maximum(qi - 1 - g0, 0)
    n_long = n_loop // LONG_STEP
    acc = lax.fori_loop(0, n_long, full_groups(LONG_STEP), jnp.zeros((V_PAD, TQF), F32))
    acc = lax.fori_loop(n_long * LONG_STEP, n_loop, full_groups(1), acc)
    diag = [(qi * nb + u, u * TK, True) for u in range(nb)]
    acc = lax.cond(qi > 0,
                   lambda a: run([((qi - 1) * nb + u, 0, False) for u in range(nb)] + diag, a),
                   lambda a: run(diag, a), acc)
    o_ref[0] = (acc[0:dv] / acc[dv:dv + 1]).astype(o_ref.dtype)


def _causal_attn(qt, k, vt, gran, bound, first_group):
    B, H, _, S = qt.shape
    nk, dv = k.shape[2], FOX_DIM
    assert TQ % TK == 0 and TQF % TK == 0 and MLA_V == FOX_DIM and vt.shape[3] == V_PAD

    def call(body, tq, name, prefetch):
        grid_spec = pltpu.PrefetchScalarGridSpec(
            num_scalar_prefetch=len(prefetch),
            grid=(B, H, S // tq),
            in_specs=[
                pl.BlockSpec((1, 1, QK_PAD, tq), lambda b, h, q, *_: (b, h, 0, q)),
                pl.BlockSpec((1, 1, nk, TK, QK_PAD), lambda b, h, q, *_: (b, h, 0, 0, 0)),
                pl.BlockSpec((1, 1, nk, V_PAD, TK), lambda b, h, q, *_: (b, h, 0, 0, 0)),
            ],
            out_specs=pl.BlockSpec((1, dv, tq), lambda b, h, q, *_: (b, h, q)),
        )
        return pl.pallas_call(
            functools.partial(body, gran_shift=int(np.log2(gran))),
            grid_spec=grid_spec,
            out_shape=jax.ShapeDtypeStruct((B, H * dv, S), BF16),
            compiler_params=pltpu.CompilerParams(
                dimension_semantics=("parallel", "parallel", "arbitrary"), vmem_limit_bytes=VMEM_LIMIT),
            name=name,
        )(*prefetch, qt, k, vt)

    return lax.cond(bound <= MAX_SHIFT_BOUND,
                    lambda: call(_shifted_attn_kernel, TQF, "shifted_attn", (first_group.reshape(-1),)),
                    lambda: call(_causal_attn_kernel, TQ, "causal_attn", ()))


def _chunk_attn_kernel(qt_ref, k0_ref, k1_ref, k2_ref, v0_ref, v1_ref, v2_ref, bias_ref, o_ref, *,
                       shifted):
    t = pl.program_id(1)
    k_refs = (k0_ref, k1_ref, k2_ref)
    v_refs = (v0_ref, v1_ref, v2_ref)

    def scores(hd):
        qt = qt_ref[0, hd]
        ss = []
        for d in range(3):
            s = _nn(k_refs[d][0, hd // 2], qt) + bias_ref[hd, d * TC:(d + 1) * TC, :]
            if d < 2:
                s = jnp.where(t + (d - 2) >= 0, s, NEG_INF)
            ss.append(s)
        return ss

    def finish(hd, ss):
        if not shifted:
            m = jnp.max(jnp.maximum(jnp.maximum(ss[0], ss[1]), ss[2]), axis=0, keepdims=True)
            ss = [s - m for s in ss]
        p = jnp.concatenate([jnp.exp2(s).astype(BF16) for s in ss], axis=0)
        vt = jnp.concatenate([r[0, hd] for r in v_refs], axis=1)
        acc = _nn(vt, p)
        o_ref[0, hd * CHK_DIM:(hd + 1) * CHK_DIM, :] = (
            acc[0:CHK_DIM] / acc[CHK_DIM:CHK_DIM + 1]).astype(o_ref.dtype)

    pending = [scores(0)]
    for hd in range(CHK_HEADS):
        if hd + 1 < CHK_HEADS:
            pending.append(scores(hd + 1))
        finish(hd, pending.pop(0))


def _chunk_attn(qbt, kb, vbt, bias_t, spread):
    B, _, _, S = qbt.shape
    kspec = lambda d: pl.BlockSpec((1, CHK_HEADS // 2, TC, 2 * CHK_DIM),
                                   lambda b, t: (b, 0, jnp.maximum(t + (d - 2), 0), 0))
    vspec = lambda d: pl.BlockSpec((1, CHK_HEADS, V_PAD, TC),
                                   lambda b, t: (b, 0, 0, jnp.maximum(t + (d - 2), 0)))

    def call(shifted):
        return pl.pallas_call(
            functools.partial(_chunk_attn_kernel, shifted=shifted),
            grid=(B, S // TC),
            in_specs=[
                pl.BlockSpec((1, CHK_HEADS, 2 * CHK_DIM, TC), lambda b, t: (b, 0, 0, t)),
                kspec(0), kspec(1), kspec(2), vspec(0), vspec(1), vspec(2),
                pl.BlockSpec((CHK_HEADS, 3 * TC, TC), lambda b, t: (0, 0, 0)),
            ],
            out_specs=pl.BlockSpec((1, CHK_WIDTH, TC), lambda b, t: (b, 0, t)),
            out_shape=jax.ShapeDtypeStruct((B, CHK_WIDTH, S), BF16),
            compiler_params=pltpu.CompilerParams(
                dimension_semantics=("parallel", "parallel"), vmem_limit_bytes=VMEM_LIMIT),
            name="chunk_attn_shifted" if shifted else "chunk_attn",
        )(qbt, kb, kb, kb, vbt, vbt, vbt, bias_t)

    return lax.cond(spread <= 2 * MAX_SHIFT_BOUND, lambda: call(True), lambda: call(False))


def _out_ple_kernel(*refs, n_o):
    o_refs = refs[:n_o]
    gt_ref, h_ref, p_ref, wo_ref, pgn_ref, wgate_ref, pew_ref, out_ref = refs[n_o:]
    n_sub = TM // OUT_SUB
    subs = [slice(i * OUT_SUB, (i + 1) * OUT_SUB) for i in range(n_sub)]

    def gated(ts):
        ot = jnp.concatenate([r[0, :, ts] for r in o_refs], axis=0) if n_o > 1 else o_refs[0][0, :, ts]
        g = gt_ref[0, :, ts].astype(F32)
        return (ot.astype(F32) * (g * _sigmoid(g))).astype(BF16)

    mixed_t = [_nn(wo_ref[...], gated(ts)) for ts in subs]
    h1 = [h_ref[0, ts, :] + m.T for ts, m in zip(subs, mixed_t)]
    gate = [_nn(_rms_rows(x, pgn_ref[...]).astype(BF16), wgate_ref[...]) for x in h1]
    pe = [_nn(p_ref[0, 0, ts, :].astype(BF16), pew_ref[...]) for ts in subs]
    for ts, x, gt, e in zip(subs, h1, gate, pe):
        out_ref[0, ts, :] = x + e * _sigmoid(gt)


def _out_ple(o_list, gt, h, p, layer, wo_t, pgn, wgate, pew):
    B, S, _ = h.shape
    full = lambda shape: pl.BlockSpec(shape, lambda b, t: (0,) * len(shape))
    width = gt.shape[1]
    return pl.pallas_call(
        functools.partial(_out_ple_kernel, n_o=len(o_list)),
        grid=(B, S // TM),
        in_specs=[pl.BlockSpec((1, o.shape[1], TM), lambda b, t: (b, 0, t)) for o in o_list] + [
            pl.BlockSpec((1, width, TM), lambda b, t: (b, 0, t)),
            pl.BlockSpec((1, TM, D_MODEL), lambda b, t: (b, t, 0)),
            pl.BlockSpec((1, 1, TM, PLE_DIM), lambda b, t: (layer, b, t, 0)),
            full((D_MODEL, width)), full((1, D_MODEL)), full((D_MODEL, D_MODEL)),
            full((PLE_DIM, D_MODEL)),
        ],
        out_specs=pl.BlockSpec((1, TM, D_MODEL), lambda b, t: (b, t, 0)),
        out_shape=jax.ShapeDtypeStruct((B, S, D_MODEL), F32),
        compiler_params=pltpu.CompilerParams(
            dimension_semantics=("parallel", "parallel"), vmem_limit_bytes=VMEM_LIMIT),
        name="out_ple",
    )(*o_list, gt, h, p, wo_t, pgn, wgate, pew)


def _col(v):
    return v.astype(F32).reshape(-1, 1)


def _score_bound(qg, kg, d):
    return 1.02 * d * jnp.max(jnp.abs(qg)) * jnp.max(jnp.abs(kg)) + 1.0


def _first_live_group(cum2):
    nb = TQF // TK
    first_q = cum2[:, :, ::TQF]
    last_k = cum2[:, :, TK - 1::TK]
    dead = last_k[:, :, None, :] > first_q[:, :, :, None] + UNDERFLOW_LOG2
    return (jnp.sum(dead, axis=-1) // nb).astype(jnp.int32)


def _chunk_bias_table(rel_bias, bound_qk):
    n = 4 * TC
    i = np.arange(n)
    dist = np.where(i < TC, i, i - n) + 2 * TC
    rb = rel_bias.astype(F32) * LOG2E
    spread = 2.0 * bound_qk + (jnp.max(rb) - jnp.min(rb))
    e = rb[:, np.clip(dist, -REL_CLIP, REL_CLIP) + REL_CLIP] - (bound_qk + jnp.max(rb))
    h = rel_bias.shape[0]
    return _bias_table(e.reshape(h, 1, n)), spread


def _bias_table(e):
    h, _, n = e.shape
    return pl.pallas_call(
        _bias_table_kernel,
        grid=(h,),
        in_specs=[pl.BlockSpec((1, 1, n), lambda hd: (hd, 0, 0))],
        out_specs=pl.BlockSpec((1, 3 * TC, TC), lambda hd: (hd, 0, 0)),
        out_shape=jax.ShapeDtypeStruct((h, 3 * TC, TC), F32),
        name="bias_table",
    )(e)


def _bias_table_kernel(e_ref, o_ref):
    n = e_ref.shape[2]
    rows = jnp.broadcast_to(e_ref[0], (3 * TC, n))
    skew = pltpu.roll(rows, 0, 1, stride=1, stride_axis=0)
    kchunk = lax.broadcasted_iota(jnp.int32, (3 * TC, TC), 0) // CHUNK
    qchunk = lax.broadcasted_iota(jnp.int32, (3 * TC, TC), 1) // CHUNK + (2 * TC) // CHUNK
    valid = (kchunk <= qchunk) & (kchunk >= qchunk - LEFT_CHUNKS)
    o_ref[0] = jnp.where(valid, skew[:, 0:TC], NEG_INF)


def kernel(x, p, positions, norm_g, ab_w_in, mla_q_norm, mla_w_uq, mla_kv_norm, mla_w_ukv, mla_q_gain, mla_k_gain, chk_q_gain, chk_k_gain, chk_rel_bias, ab_w_out, fox_w_in, fox_b_f, fox_q_gain, fox_k_gain, fox_w_out, pe_w, pe_gate_norm, pe_gate_w):
    B, S, _ = x.shape
    half = MLA_ROPE // 2
    inv_freq = 1.0 / (ROPE_THETA ** (jnp.arange(half, dtype=F32) / half))
    ang = positions.astype(F32)[:, None, :] * inv_freq[None, :, None]
    cos, sin = jnp.cos(ang), jnp.sin(ang)
    tri = (np.arange(TM)[:, None] <= np.arange(TM)[None, :]).astype(np.float32)
    tri = jnp.asarray(tri, BF16)

    h = x
    for i in range(DEPTH):
        l = i // 2
        ng = norm_g[i].astype(F32).reshape(1, -1)
        if i % 2 == 0:
            w = ab_w_in[l]
            o = np.cumsum((0, MLA_Q_LORA, MLA_KV_LORA, MLA_ROPE, MLA_WIDTH) + (CHK_WIDTH,) * 4)
            wcq = w[:, o[0]:o[1]].astype(BF16)
            wckv = w[:, o[1]:o[2]].astype(BF16)
            wg = jnp.concatenate([w[:, o[3]:o[4]], w[:, o[7]:o[8]], w[:, o[2]:o[3]]], axis=1).T.astype(BF16)
            wqb = w[:, o[4]:o[5]].T.astype(BF16)
            wkb = w[:, o[5]:o[6]].T.astype(BF16)
            wvb = w[:, o[6]:o[7]].T.astype(BF16)
            qg, kg = _col(mla_q_gain[l]) * (MLA_QK ** -0.5 * LOG2E), _col(mla_k_gain[l])
            bound = _score_bound(qg, kg, MLA_QK)
            cqg, ckg = _col(chk_q_gain[l]) * (CHK_DIM ** -0.5 * LOG2E), _col(chk_k_gain[l])
            qt, k, vt, gt, qbt, kb, vbt = _ab_proj(
                h, ng, wcq, wckv, wg, wqb, wkb, wvb,
                mla_q_norm[l].astype(F32).reshape(1, -1), mla_w_uq[l].T.astype(BF16),
                mla_kv_norm[l].astype(F32).reshape(1, -1), mla_w_ukv[l].T.astype(BF16),
                qg, kg, cqg, ckg, cos, sin, bound.reshape(1, 1))
            o_a = _causal_attn(qt, k, vt, CHUNK, bound,
                               jnp.zeros((B, MLA_HEADS, S // TQF), jnp.int32))
            bias_t, spread = _chunk_bias_table(chk_rel_bias[l], _score_bound(cqg, ckg, CHK_DIM))
            o_b = _chunk_attn(qbt, kb, vbt, bias_t, spread)
            o_list, wo = [o_a, o_b], ab_w_out[l]
        else:
            w = fox_w_in[l]
            o = np.cumsum((0,) + (FOX_WIDTH,) * 4 + (FOX_HEADS,))
            wq, wk, wv = [w[:, o[n]:o[n + 1]].T.astype(BF16) for n in range(3)]
            wgc = w[:, o[3]:o[5]].T.astype(BF16)
            qg, kg = _col(fox_q_gain[l]) * (FOX_DIM ** -0.5 * LOG2E), _col(fox_k_gain[l])
            bound = _score_bound(qg, kg, FOX_DIM)
            qa, ka, vt, gt, cum2 = _fox_proj(h, ng, wq, wk, wv, wgc, _col(fox_b_f[l]), qg, kg, tri,
                                             bound.reshape(1, 1))
            o_list, wo = [_causal_attn(qa, ka, vt, 1, bound, _first_live_group(cum2))], fox_w_out[l]
        h = _out_ple(o_list, gt, h, p, i, wo.T.astype(BF16),
                     pe_gate_norm[i].astype(F32).reshape(1, -1),
                     pe_gate_w[i].astype(BF16), pe_w[i].astype(BF16))
    return h
```

```python
import functools

import numpy as np
import jax
import jax.numpy as jnp
from jax import lax
from jax.experimental import pallas as pl
from jax.experimental.pallas import tpu as pltpu

F32 = jnp.float32
BF16 = jnp.bfloat16

D_MODEL = 1024
DEPTH = 4
CHUNK = 64
PLE_DIM = 256
EPS = 1e-6
NEG_INF = -1e30
MLA_HEADS = 8
MLA_NOPE = 64
MLA_ROPE = 32
MLA_QK = MLA_NOPE + MLA_ROPE
MLA_V = 64
MLA_Q_LORA = 384
MLA_KV_LORA = 256
ROPE_THETA = 10000.0
CHK_HEADS = 8
CHK_DIM = 64
LEFT_CHUNKS = 8
REL_CLIP = 256
FOX_HEADS = 16
FOX_DIM = 64
MLA_WIDTH = MLA_HEADS * MLA_V
CHK_WIDTH = CHK_HEADS * CHK_DIM
FOX_WIDTH = FOX_HEADS * FOX_DIM

TM = 512
TQ = 512
TQ_DENSE = 1024
TQ_DECAY = 1024
UNDERFLOW_LOG2 = 160.0
MAX_SHIFT_BOUND = 45.0
TK = 256
TC = 256
OUT_SUB = 256
PROJ_ROWS = 256
QK_PAD = 128
V_PAD = 80
LOOKAHEAD = 1
LOG2E = 1.4426950408889634
VMEM_LIMIT = 56 * 1024 * 1024


def _nt(a, b):
    return lax.dot_general(a, b, (((1,), (1,)), ((), ())), preferred_element_type=F32)


def _nn(a, b):
    return jnp.dot(a, b, preferred_element_type=F32)


def _sigmoid(x):
    return 1.0 / (1.0 + jnp.exp(-x))


def _rms_rows(x, g):
    ms = jnp.mean(x * x, axis=-1, keepdims=True)
    return x * lax.rsqrt(ms + EPS) * g


def _split3(x):
    hi = x.astype(BF16)
    r1 = x - hi.astype(F32)
    mid = r1.astype(BF16)
    lo = (r1 - mid.astype(F32)).astype(BF16)
    return hi, mid, lo


def _store_kv_blocks(k_ref, vt_ref, hd, k_t, v_t, ones_row):
    k_rows = k_t.T.astype(BF16)
    v_aug = jnp.concatenate([v_t.astype(BF16), ones_row], axis=0)
    for c in range(TM // TK):
        k_ref[0, hd, c] = k_rows[c * TK:(c + 1) * TK]
        vt_ref[0, hd, c] = v_aug[:, c * TK:(c + 1) * TK]


def _rope_t(x, cos, sin):
    half = x.shape[0] // 2
    x1, x2 = x[:half], x[half:]
    return jnp.concatenate([x1 * cos - x2 * sin, x2 * cos + x1 * sin], axis=0)


def _fox_proj_kernel(h_ref, ng_ref, w_ref, bf_ref, qg_ref, kg_ref,
                     tri_ref, bound_ref, qa_ref, ka_ref, vt_ref, gt_ref, cum_ref, carry_ref):
    t = pl.program_id(1)
    u = _rms_rows(h_ref[0], ng_ref[...]).astype(BF16)

    wq0, wk0, wv0, wg0 = 0, FOX_WIDTH, 2 * FOX_WIDTH, 3 * FOX_WIDTH
    g_last = _nt(w_ref[wg0 + FOX_WIDTH - PROJ_ROWS:wg0 + FOX_WIDTH + FOX_HEADS, :], u)
    z = g_last[PROJ_ROWS:PROJ_ROWS + FOX_HEADS] + bf_ref[...]
    logf = jnp.minimum(z, 0.0) - jnp.log(1.0 + jnp.exp(-jnp.abs(z)))
    parts = jnp.concatenate(_split3(logf), axis=0)
    c3 = _nn(parts, tri_ref[...])
    local = c3[0:FOX_HEADS] + c3[FOX_HEADS:2 * FOX_HEADS] + c3[2 * FOX_HEADS:3 * FOX_HEADS]

    @pl.when(t == 0)
    def _():
        carry_ref[...] = jnp.zeros_like(carry_ref)

    cum = local + carry_ref[:, 0:1]
    carry_ref[...] = jnp.broadcast_to(cum[:, TM - 1:TM], carry_ref.shape)
    c_hi, c_mid, c_lo = [p.astype(F32) for p in _split3(cum * LOG2E)]
    cum_ref[0] = cum * LOG2E
    d_hi, d_mid, d_lo = [p.astype(F32) for p in _split3(cum * LOG2E + bound_ref[...])]

    row = lax.broadcasted_iota(jnp.int32, (16, TM), 0)
    ones_row = jnp.where(row == 0, 1.0, 0.0).astype(BF16)
    zeros48 = jnp.zeros((QK_PAD - FOX_DIM - 16, TM), F32)
    qg = qg_ref[...]
    kg = kg_ref[...]
    heads_per_chunk = PROJ_ROWS // FOX_DIM
    for c in range(FOX_WIDTH // PROJ_ROWS):
        rows = slice(c * PROJ_ROWS, (c + 1) * PROJ_ROWS)
        r0, r1 = rows.start, rows.stop
        qc = _nt(w_ref[wq0 + r0:wq0 + r1, :], u)
        kc = _nt(w_ref[wk0 + r0:wk0 + r1, :], u)
        vc = _nt(w_ref[wv0 + r0:wv0 + r1, :], u)
        gc = g_last[0:PROJ_ROWS] if r1 == FOX_WIDTH else _nt(w_ref[wg0 + r0:wg0 + r1, :], u)
        gt_ref[0, rows, :] = gc.astype(BF16)
        for j in range(heads_per_chunk):
            hd = c * heads_per_chunk + j
            hs = slice(j * FOX_DIM, (j + 1) * FOX_DIM)
            hi, mid, lo = c_hi[hd:hd + 1], c_mid[hd:hd + 1], c_lo[hd:hd + 1]
            khi, kmid, klo = d_hi[hd:hd + 1], d_mid[hd:hd + 1], d_lo[hd:hd + 1]
            qh = qc[hs]
            qn = qh * lax.rsqrt(jnp.mean(qh * qh, axis=0, keepdims=True) + EPS) * qg
            exq = jnp.where(row < 3, 1.0, jnp.where(row == 3, hi, jnp.where(row == 4, mid,
                            jnp.where(row == 5, lo, 0.0))))
            qa_ref[0, hd, 0:FOX_DIM, :] = qn.astype(BF16)
            qa_ref[0, hd, FOX_DIM:FOX_DIM + 16, :] = exq.astype(BF16)
            qa_ref[0, hd, FOX_DIM + 16:QK_PAD, :] = zeros48.astype(BF16)
            kh = kc[hs]
            kn = kh * lax.rsqrt(jnp.mean(kh * kh, axis=0, keepdims=True) + EPS) * kg
            exk = jnp.where(row == 0, -khi, jnp.where(row == 1, -kmid, jnp.where(row == 2, -klo,
                            jnp.where(row < 6, 1.0, 0.0))))
            kfull = jnp.concatenate([kn, exk, zeros48], axis=0)
            _store_kv_blocks(ka_ref, vt_ref, hd, kfull, vc[hs], ones_row)


def _fox_proj(h, ng, w_t, bfc, qg, kg, tri, bound):
    B, S, _ = h.shape
    nt = S // TM
    full = lambda shape: pl.BlockSpec(shape, lambda b, t: (0,) * len(shape))
    return pl.pallas_call(
        _fox_proj_kernel,
        grid=(B, nt),
        in_specs=[
            pl.BlockSpec((1, TM, D_MODEL), lambda b, t: (b, t, 0)),
            full((1, D_MODEL)),
            full((4 * FOX_WIDTH + FOX_HEADS, D_MODEL)),
            full((FOX_HEADS, 1)),
            full((FOX_DIM, 1)), full((FOX_DIM, 1)),
            full((TM, TM)), full((1, 1)),
        ],
        out_specs=[
            pl.BlockSpec((1, FOX_HEADS, QK_PAD, TM), lambda b, t: (b, 0, 0, t)),
            pl.BlockSpec((1, FOX_HEADS, TM // TK, TK, QK_PAD), lambda b, t: (b, 0, t, 0, 0)),
            pl.BlockSpec((1, FOX_HEADS, TM // TK, V_PAD, TK), lambda b, t: (b, 0, t, 0, 0)),
            pl.BlockSpec((1, FOX_WIDTH, TM), lambda b, t: (b, 0, t)),
            pl.BlockSpec((1, FOX_HEADS, TM), lambda b, t: (b, 0, t)),
        ],
        out_shape=[
            jax.ShapeDtypeStruct((B, FOX_HEADS, QK_PAD, S), BF16),
            jax.ShapeDtypeStruct((B, FOX_HEADS, S // TK, TK, QK_PAD), BF16),
            jax.ShapeDtypeStruct((B, FOX_HEADS, S // TK, V_PAD, TK), BF16),
            jax.ShapeDtypeStruct((B, FOX_WIDTH, S), BF16),
            jax.ShapeDtypeStruct((B, FOX_HEADS, S), F32),
        ],
        scratch_shapes=[pltpu.VMEM((FOX_HEADS, 128), F32)],
        compiler_params=pltpu.CompilerParams(
            dimension_semantics=("parallel", "arbitrary"), vmem_limit_bytes=VMEM_LIMIT),
        name="fox_proj",
    )(h, ng, w_t, bfc, qg, kg, tri, bound)


def _ab_proj_kernel(h_ref, ng_ref, w_ref, qnorm_ref, wuq_ref, kvnorm_ref, wukv_ref, qg_ref, kg_ref,
                    cqg_ref, ckg_ref,
                    cos_ref, sin_ref, bound_ref,
                    qt_ref, k_ref, vt_ref, gt_ref, qbt_ref, kb_ref, vbt_ref):
    u = _rms_rows(h_ref[0], ng_ref[...]).astype(BF16)
    cos = cos_ref[0]
    sin = sin_ref[0]

    off = np.cumsum((0, MLA_Q_LORA, MLA_KV_LORA, MLA_ROPE, MLA_WIDTH) + (CHK_WIDTH,) * 4)

    def rms_cols(x, g):
        return x * lax.rsqrt(jnp.mean(x * x, axis=0, keepdims=True) + EPS) * g

    lat = _nt(w_ref[off[0]:off[3], :], u)
    cqn = rms_cols(lat[off[0]:off[1]], qnorm_ref[...]).astype(BF16)
    ckvn = rms_cols(lat[off[1]:off[2]], kvnorm_ref[...]).astype(BF16)
    krt = lat[off[2]:off[3]]
    ss_kr = jnp.sum(krt * krt, axis=0, keepdims=True)
    q_all = _nn(wuq_ref[...], cqn)
    kv_all = _nn(wukv_ref[...], ckvn)
    for g0, w0 in ((0, off[3]), (MLA_WIDTH, off[7])):
        for c in range(MLA_WIDTH // 256):
            gt_ref[0, g0 + c * 256:g0 + (c + 1) * 256, :] = _nt(
                w_ref[w0 + c * 256:w0 + (c + 1) * 256, :], u).astype(BF16)
    qg = qg_ref[...]
    kg = kg_ref[...]
    row32 = lax.broadcasted_iota(jnp.int32, (QK_PAD - MLA_QK, TM), 0)
    q_pad = jnp.where(row32 == 0, 1.0, 0.0)
    k_pad = jnp.where(row32 == 0, -bound_ref[...], 0.0)
    ones_row =jnp.where(lax.broadcasted_iota(jnp.int32, (V_PAD - MLA_V, TM), 0) == 0, 1.0, 0.0).astype(BF16)
    for hd in range(MLA_HEADS):
        qh = q_all[hd * MLA_QK:(hd + 1) * MLA_QK]
        qn = qh * lax.rsqrt(jnp.mean(qh * qh, axis=0, keepdims=True) + EPS) * qg
        qt_ref[0, hd, 0:MLA_NOPE, :] = qn[0:MLA_NOPE].astype(BF16)
        qt_ref[0, hd, MLA_NOPE:MLA_QK, :] = _rope_t(qn[MLA_NOPE:MLA_QK], cos, sin).astype(BF16)
        qt_ref[0, hd, MLA_QK:QK_PAD, :] = q_pad.astype(BF16)
        kv = kv_all[hd * 128:(hd + 1) * 128]
        kn = kv[0:MLA_NOPE]
        rk = lax.rsqrt((jnp.sum(kn * kn, axis=0, keepdims=True) + ss_kr) * (1.0 / MLA_QK) + EPS)
        kfull = jnp.concatenate([kn * rk * kg[0:MLA_NOPE],
                                 _rope_t(krt * rk * kg[MLA_NOPE:MLA_QK], cos, sin),
                                 k_pad], axis=0)
        _store_kv_blocks(k_ref, vt_ref, hd, kfull, kv[MLA_NOPE:MLA_NOPE + MLA_V], ones_row)

    cqg = cqg_ref[...]
    ckg = ckg_ref[...]
    zeros64 = jnp.zeros((CHK_DIM, TM), F32)
    for c in range(CHK_WIDTH // 256):
        rows = slice(c * 256, (c + 1) * 256)
        qc = _nt(w_ref[off[4] + c * 256:off[4] + (c + 1) * 256, :], u)
        kc = _nt(w_ref[off[5] + c * 256:off[5] + (c + 1) * 256, :], u)
        vc = _nt(w_ref[off[6] + c * 256:off[6] + (c + 1) * 256, :], u).astype(BF16)
        for j in range(4):
            vbt_ref[0, c * 4 + j, 0:CHK_DIM, :] = vc[j * CHK_DIM:(j + 1) * CHK_DIM]
            vbt_ref[0, c * 4 + j, CHK_DIM:V_PAD, :] = ones_row
        kns = []
        for j in range(4):
            hd = c * 4 + j
            hs = slice(j * CHK_DIM, (j + 1) * CHK_DIM)
            qh = qc[hs]
            qn = (qh * lax.rsqrt(jnp.mean(qh * qh, axis=0, keepdims=True) + EPS) * cqg).astype(BF16)
            lo, hi = (0, CHK_DIM) if hd % 2 == 0 else (CHK_DIM, 2 * CHK_DIM)
            qbt_ref[0, hd, lo:hi, :] = qn
            qbt_ref[0, hd, CHK_DIM - lo:2 * CHK_DIM - lo, :] = zeros64.astype(BF16)
            kh = kc[hs]
            kns.append(kh * lax.rsqrt(jnp.mean(kh * kh, axis=0, keepdims=True) + EPS) * ckg)
        for pr in range(2):
            pair = jnp.concatenate([kns[2 * pr], kns[2 * pr + 1]], axis=0)
            kb_ref[0, c * 2 + pr] = pair.T.astype(BF16)


def _ab_proj(h, ng, w_t, qnorm, wuq, kvnorm, wukv, qg, kg, cqg, ckg, cos, sin, bound):
    B, S, _ = h.shape
    nt = S // TM
    full = lambda shape: pl.BlockSpec(shape, lambda b, t: (0,) * len(shape))
    return pl.pallas_call(
        _ab_proj_kernel,
        grid=(B, nt),
        in_specs=[
            pl.BlockSpec((1, TM, D_MODEL), lambda b, t: (b, t, 0)),
            full((1, D_MODEL)),
            full(w_t.shape),
            full((MLA_Q_LORA, 1)), full((MLA_HEADS * MLA_QK, MLA_Q_LORA)),
            full((MLA_KV_LORA, 1)), full((MLA_HEADS * 128, MLA_KV_LORA)),
            full((MLA_QK, 1)), full((MLA_QK, 1)), full((CHK_DIM, 1)), full((CHK_DIM, 1)),
            pl.BlockSpec((1, MLA_ROPE // 2, TM), lambda b, t: (b, 0, t)),
            pl.BlockSpec((1, MLA_ROPE // 2, TM), lambda b, t: (b, 0, t)),
            full((1, 1)),
        ],
        out_specs=[
            pl.BlockSpec((1, MLA_HEADS, QK_PAD, TM), lambda b, t: (b, 0, 0, t)),
            pl.BlockSpec((1, MLA_HEADS, TM // TK, TK, QK_PAD), lambda b, t: (b, 0, t, 0, 0)),
            pl.BlockSpec((1, MLA_HEADS, TM // TK, V_PAD, TK), lambda b, t: (b, 0, t, 0, 0)),
            pl.BlockSpec((1, MLA_WIDTH + CHK_WIDTH, TM), lambda b, t: (b, 0, t)),
            pl.BlockSpec((1, CHK_HEADS, 2 * CHK_DIM, TM), lambda b, t: (b, 0, 0, t)),
            pl.BlockSpec((1, CHK_HEADS // 2, TM, 2 * CHK_DIM), lambda b, t: (b, 0, t, 0)),
            pl.BlockSpec((1, CHK_HEADS, V_PAD, TM), lambda b, t: (b, 0, 0, t)),
        ],
        out_shape=[
            jax.ShapeDtypeStruct((B, MLA_HEADS, QK_PAD, S), BF16),
            jax.ShapeDtypeStruct((B, MLA_HEADS, S // TK, TK, QK_PAD), BF16),
            jax.ShapeDtypeStruct((B, MLA_HEADS, S // TK, V_PAD, TK), BF16),
            jax.ShapeDtypeStruct((B, MLA_WIDTH + CHK_WIDTH, S), BF16),
            jax.ShapeDtypeStruct((B, CHK_HEADS, 2 * CHK_DIM, S), BF16),
            jax.ShapeDtypeStruct((B, CHK_HEADS // 2, S, 2 * CHK_DIM), BF16),
            jax.ShapeDtypeStruct((B, CHK_HEADS, V_PAD, S), BF16),
        ],
        compiler_params=pltpu.CompilerParams(
            dimension_semantics=("parallel", "parallel"), vmem_limit_bytes=VMEM_LIMIT),
        name="ab_proj",
    )(h, ng, w_t, qnorm, wuq, kvnorm, wukv, qg, kg, cqg, ckg, cos, sin, bound)


def _causal_attn_kernel(qt_ref, k_ref, vt_ref, o_ref, *, gran_shift):
    qi = pl.program_id(2)
    dv = o_ref.shape[1]
    n_sub = TQ // TK
    qts = [qt_ref[0, 0, :, c * TK:(c + 1) * TK] for c in range(n_sub)]

    def scores(c, j, masked):
        s = _nn(k_ref[0, 0, j], qts[c])
        if masked:
            kpos = lax.broadcasted_iota(jnp.int32, (TK, TK), 0)
            qpos = lax.broadcasted_iota(jnp.int32, (TK, TK), 1)
            s = jnp.where((kpos >> gran_shift) <= (qpos >> gran_shift), s, NEG_INF)
        return s

    def update(s, j, carry):
        m, acc = carry
        m_new = jnp.maximum(m, jnp.max(s, axis=0, keepdims=True))
        p = jnp.exp2(s - m_new)
        acc = jnp.exp2(m - m_new) * acc + _nn(vt_ref[0, 0, j], p.astype(BF16))
        return m_new, acc

    def run(work, carries):
        carries = list(carries)
        pending = [scores(*w) for w in work[:LOOKAHEAD]]
        for i, (c, j, _) in enumerate(work):
            if i + LOOKAHEAD < len(work):
                pending.append(scores(*work[i + LOOKAHEAD]))
            carries[c] = update(pending.pop(0), j, carries[c])
        return tuple(carries)

    def steps(jj, carries):
        return run([(c, jj * n_sub + u, False) for u in range(n_sub) for c in range(n_sub)], carries)

    init = (jnp.full((1, TK), NEG_INF, F32), jnp.zeros((V_PAD, TK), F32))
    carries = lax.fori_loop(0, qi, steps, (init,) * n_sub)
    carries = run([(c, qi * n_sub + u, u == c) for c in range(n_sub) for u in range(c + 1)], carries)
    for c in range(n_sub):
        acc = carries[c][1]
        o_ref[0, :, c * TK:(c + 1) * TK] = (acc[0:dv] / acc[dv:dv + 1]).astype(o_ref.dtype)


def _shifted_attn_kernel(first_ref, qt_ref, k_ref, vt_ref, o_ref, *, gran_shift, long_step):
    qi = pl.program_id(2)
    g0 = first_ref[(pl.program_id(0) * pl.num_programs(1) + pl.program_id(1)) * pl.num_programs(2) + qi]
    dv = o_ref.shape[1]
    tq = qt_ref.shape[3]
    nb = tq // TK
    qt = qt_ref[0, 0]

    def probs(j, lo, masked):
        s = _nn(k_ref[0, 0, j], qt[:, lo:])
        if masked:
            kpos = lax.broadcasted_iota(jnp.int32, s.shape, 0)
            qpos = lax.broadcasted_iota(jnp.int32, s.shape, 1)
            s = jnp.where((kpos >> gran_shift) <= (qpos >> gran_shift), s, NEG_INF)
        return jnp.exp2(s).astype(BF16)

    def run(work, acc):
        pending = [probs(*w) for w in work[:LOOKAHEAD]]
        for i, (j, lo, _) in enumerate(work):
            if i + LOOKAHEAD < len(work):
                pending.append(probs(*work[i + LOOKAHEAD]))
            pv = _nn(vt_ref[0, 0, j], pending.pop(0))
            acc = acc + pv if lo == 0 else jnp.concatenate([acc[:, :lo], acc[:, lo:] + pv], axis=1)
        return acc

    def full_groups(n_grp):
        return lambda jj, a: run([((g0 + jj * n_grp) * nb + u, 0, False) for u in range(n_grp * nb)], a)

    n_loop = jnp.maximum(qi - 1 - g0, 0)
    n_long = n_loop // long_step
    acc = lax.fori_loop(0, n_long, full_groups(long_step), jnp.zeros((V_PAD, tq), F32))
    if long_step > 1:
        acc = lax.fori_loop(n_long * long_step, n_loop, full_groups(1), acc)
    diag = [(qi * nb + u, u * TK, True) for u in range(nb)]
    acc = lax.cond(qi > 0,
                   lambda a: run([((qi - 1) * nb + u, 0, False) for u in range(nb)] + diag, a),
                   lambda a: run(diag, a), acc)
    o_ref[0] = (acc[0:dv] / acc[dv:dv + 1]).astype(o_ref.dtype)


def _causal_attn(qt, k, vt, gran, bound, first_group, tqf, long_step):
    B, H, _, S = qt.shape
    nk, dv = k.shape[2], FOX_DIM
    assert TQ % TK == 0 and tqf % TK == 0 and MLA_V == FOX_DIM and vt.shape[3] == V_PAD

    def call(body, tq, name, prefetch, **static):
        grid_spec = pltpu.PrefetchScalarGridSpec(
            num_scalar_prefetch=len(prefetch),
            grid=(B, H, S // tq),
            in_specs=[
                pl.BlockSpec((1, 1, QK_PAD, tq), lambda b, h, q, *_: (b, h, 0, q)),
                pl.BlockSpec((1, 1, nk, TK, QK_PAD), lambda b, h, q, *_: (b, h, 0, 0, 0)),
                pl.BlockSpec((1, 1, nk, V_PAD, TK), lambda b, h, q, *_: (b, h, 0, 0, 0)),
            ],
            out_specs=pl.BlockSpec((1, dv, tq), lambda b, h, q, *_: (b, h, q)),
        )
        return pl.pallas_call(
            functools.partial(body, gran_shift=int(np.log2(gran)), **static),
            grid_spec=grid_spec,
            out_shape=jax.ShapeDtypeStruct((B, H * dv, S), BF16),
            compiler_params=pltpu.CompilerParams(
                dimension_semantics=("parallel", "parallel", "arbitrary"), vmem_limit_bytes=VMEM_LIMIT),
            name=name,
        )(*prefetch, qt, k, vt)

    return lax.cond(bound <= MAX_SHIFT_BOUND,
                    lambda: call(_shifted_attn_kernel, tqf, "shifted_attn", (first_group.reshape(-1),),
                                 long_step=long_step),
                    lambda: call(_causal_attn_kernel, TQ, "causal_attn", ()))


def _chunk_attn_kernel(qt_ref, k0_ref, k1_ref, k2_ref, v0_ref, v1_ref, v2_ref, bias_ref, o_ref, *,
                       shifted):
    t = pl.program_id(1)
    k_refs = (k0_ref, k1_ref, k2_ref)
    v_refs = (v0_ref, v1_ref, v2_ref)

    def scores(hd):
        qt = qt_ref[0, hd]
        ss = []
        for d in range(3):
            s = _nn(k_refs[d][0, hd // 2], qt) + bias_ref[hd, d * TC:(d + 1) * TC, :]
            if d < 2:
                s = jnp.where(t + (d - 2) >= 0, s, NEG_INF)
            ss.append(s)
        return ss

    def finish(hd, ss):
        if not shifted:
            m = jnp.max(jnp.maximum(jnp.maximum(ss[0], ss[1]), ss[2]), axis=0, keepdims=True)
            ss = [s - m for s in ss]
        p = jnp.concatenate([jnp.exp2(s).astype(BF16) for s in ss], axis=0)
        vt = jnp.concatenate([r[0, hd] for r in v_refs], axis=1)
        acc = _nn(vt, p)
        o_ref[0, hd * CHK_DIM:(hd + 1) * CHK_DIM, :] = (
            acc[0:CHK_DIM] / acc[CHK_DIM:CHK_DIM + 1]).astype(o_ref.dtype)

    pending = [scores(0)]
    for hd in range(CHK_HEADS):
        if hd + 1 < CHK_HEADS:
            pending.append(scores(hd + 1))
        finish(hd, pending.pop(0))


def _chunk_attn(qbt, kb, vbt, bias_t, spread):
    B, _, _, S = qbt.shape
    kspec = lambda d: pl.BlockSpec((1, CHK_HEADS // 2, TC, 2 * CHK_DIM),
                                   lambda b, t: (b, 0, jnp.maximum(t + (d - 2), 0), 0))
    vspec = lambda d: pl.BlockSpec((1, CHK_HEADS, V_PAD, TC),
                                   lambda b, t: (b, 0, 0, jnp.maximum(t + (d - 2), 0)))

    def call(shifted):
        return pl.pallas_call(
            functools.partial(_chunk_attn_kernel, shifted=shifted),
            grid=(B, S // TC),
            in_specs=[
                pl.BlockSpec((1, CHK_HEADS, 2 * CHK_DIM, TC), lambda b, t: (b, 0, 0, t)),
                kspec(0), kspec(1), kspec(2), vspec(0), vspec(1), vspec(2),
                pl.BlockSpec((CHK_HEADS, 3 * TC, TC), lambda b, t: (0, 0, 0)),
            ],
            out_specs=pl.BlockSpec((1, CHK_WIDTH, TC), lambda b, t: (b, 0, t)),
            out_shape=jax.ShapeDtypeStruct((B, CHK_WIDTH, S), BF16),
            compiler_params=pltpu.CompilerParams(
                dimension_semantics=("parallel", "parallel"), vmem_limit_bytes=VMEM_LIMIT),
            name="chunk_attn_shifted" if shifted else "chunk_attn",
        )(qbt, kb, kb, kb, vbt, vbt, vbt, bias_t)

    return lax.cond(spread <= 2 * MAX_SHIFT_BOUND, lambda: call(True), lambda: call(False))


def _out_ple_kernel(*refs, n_o):
    o_refs = refs[:n_o]
    gt_ref, h_ref, p_ref, wo_ref, pgn_ref, wgate_ref, pew_ref, out_ref = refs[n_o:]
    n_sub = TM // OUT_SUB
    subs = [slice(i * OUT_SUB, (i + 1) * OUT_SUB) for i in range(n_sub)]

    def gated(ts):
        ot = jnp.concatenate([r[0, :, ts] for r in o_refs], axis=0) if n_o > 1 else o_refs[0][0, :, ts]
        g = gt_ref[0, :, ts].astype(F32)
        return (ot.astype(F32) * (g * _sigmoid(g))).astype(BF16)

    mixed_t = [_nn(wo_ref[...], gated(ts)) for ts in subs]
    h1 = [h_ref[0, ts, :] + m.T for ts, m in zip(subs, mixed_t)]
    gate = [_nn(_rms_rows(x, pgn_ref[...]).astype(BF16), wgate_ref[...]) for x in h1]
    pe = [_nn(p_ref[0, 0, ts, :].astype(BF16), pew_ref[...]) for ts in subs]
    for ts, x, gt, e in zip(subs, h1, gate, pe):
        out_ref[0, ts, :] = x + e * _sigmoid(gt)


def _out_ple(o_list, gt, h, p, layer, wo_t, pgn, wgate, pew):
    B, S, _ = h.shape
    full = lambda shape: pl.BlockSpec(shape, lambda b, t: (0,) * len(shape))
    width = gt.shape[1]
    return pl.pallas_call(
        functools.partial(_out_ple_kernel, n_o=len(o_list)),
        grid=(B, S // TM),
        in_specs=[pl.BlockSpec((1, o.shape[1], TM), lambda b, t: (b, 0, t)) for o in o_list] + [
            pl.BlockSpec((1, width, TM), lambda b, t: (b, 0, t)),
            pl.BlockSpec((1, TM, D_MODEL), lambda b, t: (b, t, 0)),
            pl.BlockSpec((1, 1, TM, PLE_DIM), lambda b, t: (layer, b, t, 0)),
            full((D_MODEL, width)), full((1, D_MODEL)), full((D_MODEL, D_MODEL)),
            full((PLE_DIM, D_MODEL)),
        ],
        out_specs=pl.BlockSpec((1, TM, D_MODEL), lambda b, t: (b, t, 0)),
        out_shape=jax.ShapeDtypeStruct((B, S, D_MODEL), F32),
        compiler_params=pltpu.CompilerParams(
            dimension_semantics=("parallel", "parallel"), vmem_limit_bytes=VMEM_LIMIT),
        name="out_ple",
    )(*o_list, gt, h, p, wo_t, pgn, wgate, pew)


def _col(v):
    return v.astype(F32).reshape(-1, 1)


def _score_bound(qg, kg, d):
    return 1.02 * d * jnp.max(jnp.abs(qg)) * jnp.max(jnp.abs(kg)) + 1.0


def _first_live_group(cum2, tqf):
    nb = tqf // TK
    first_q = cum2[:, :, ::tqf]
    last_k = cum2[:, :, TK - 1::TK]
    dead = last_k[:, :, None, :] > first_q[:, :, :, None] + UNDERFLOW_LOG2
    return (jnp.sum(dead, axis=-1) // nb).astype(jnp.int32)


def _chunk_bias_table(rel_bias, bound_qk):
    n = 4 * TC
    i = np.arange(n)
    dist = np.where(i < TC, i, i - n) + 2 * TC
    rb = rel_bias.astype(F32) * LOG2E
    spread = 2.0 * bound_qk + (jnp.max(rb) - jnp.min(rb))
    e = rb[:, np.clip(dist, -REL_CLIP, REL_CLIP) + REL_CLIP] - (bound_qk + jnp.max(rb))
    h = rel_bias.shape[0]
    return _bias_table(e.reshape(h, 1, n)), spread


def _bias_table(e):
    h, _, n = e.shape
    return pl.pallas_call(
        _bias_table_kernel,
        grid=(h,),
        in_specs=[pl.BlockSpec((1, 1, n), lambda hd: (hd, 0, 0))],
        out_specs=pl.BlockSpec((1, 3 * TC, TC), lambda hd: (hd, 0, 0)),
        out_shape=jax.ShapeDtypeStruct((h, 3 * TC, TC), F32),
        name="bias_table",
    )(e)


def _bias_table_kernel(e_ref, o_ref):
    n = e_ref.shape[2]
    rows = jnp.broadcast_to(e_ref[0], (3 * TC, n))
    skew = pltpu.roll(rows, 0, 1, stride=1, stride_axis=0)
    kchunk = lax.broadcasted_iota(jnp.int32, (3 * TC, TC), 0) // CHUNK
    qchunk = lax.broadcasted_iota(jnp.int32, (3 * TC, TC), 1) // CHUNK + (2 * TC) // CHUNK
    valid = (kchunk <= qchunk) & (kchunk >= qchunk - LEFT_CHUNKS)
    o_ref[0] = jnp.where(valid, skew[:, 0:TC], NEG_INF)


def kernel(x, p, positions, norm_g, ab_w_in, mla_q_norm, mla_w_uq, mla_kv_norm, mla_w_ukv, mla_q_gain, mla_k_gain, chk_q_gain, chk_k_gain, chk_rel_bias, ab_w_out, fox_w_in, fox_b_f, fox_q_gain, fox_k_gain, fox_w_out, pe_w, pe_gate_norm, pe_gate_w):
    B, S, _ = x.shape
    half = MLA_ROPE // 2
    inv_freq = 1.0 / (ROPE_THETA ** (jnp.arange(half, dtype=F32) / half))
    ang = positions.astype(F32)[:, None, :] * inv_freq[None, :, None]
    cos, sin = jnp.cos(ang), jnp.sin(ang)
    tri = (np.arange(TM)[:, None] <= np.arange(TM)[None, :]).astype(np.float32)
    tri = jnp.asarray(tri, BF16)

    h = x
    for i in range(DEPTH):
        l = i // 2
        ng = norm_g[i].astype(F32).reshape(1, -1)
        if i % 2 == 0:
            qg, kg = _col(mla_q_gain[l]) * (MLA_QK ** -0.5 * LOG2E), _col(mla_k_gain[l])
            bound = _score_bound(qg, kg, MLA_QK)
            cqg, ckg = _col(chk_q_gain[l]) * (CHK_DIM ** -0.5 * LOG2E), _col(chk_k_gain[l])
            qt, k, vt, gt, qbt, kb, vbt = _ab_proj(
                h, ng, ab_w_in[l].T.astype(BF16),
                _col(mla_q_norm[l]), mla_w_uq[l].T.astype(BF16),
                _col(mla_kv_norm[l]), mla_w_ukv[l].T.astype(BF16),
                qg, kg, cqg, ckg, cos, sin, bound.reshape(1, 1))
            o_a = _causal_attn(qt, k, vt, CHUNK, bound,
                               jnp.zeros((B, MLA_HEADS, S // TQ_DENSE), jnp.int32), TQ_DENSE, 2)
            bias_t, spread = _chunk_bias_table(chk_rel_bias[l], _score_bound(cqg, ckg, CHK_DIM))
            o_b = _chunk_attn(qbt, kb, vbt, bias_t, spread)
            o_list, wo = [o_a, o_b], ab_w_out[l]
        else:
            qg, kg = _col(fox_q_gain[l]) * (FOX_DIM ** -0.5 * LOG2E), _col(fox_k_gain[l])
            bound = _score_bound(qg, kg, FOX_DIM)
            qa, ka, vt, gt, cum2 = _fox_proj(h, ng, fox_w_in[l].T.astype(BF16), _col(fox_b_f[l]), qg, kg,
                                             tri, bound.reshape(1, 1))
            o_c = _causal_attn(qa, ka, vt, 1, bound, _first_live_group(cum2, TQ_DECAY), TQ_DECAY, 2)
            o_list, wo = [o_c], fox_w_out[l]
        h = _out_ple(o_list, gt, h, p, i, wo.T.astype(BF16),
                     pe_gate_norm[i].astype(F32).reshape(1, -1),
                     pe_gate_w[i].astype(BF16), pe_w[i].astype(BF16))
    return h
```

```python
import functools

import numpy as np
import jax
import jax.numpy as jnp
from jax import lax
from jax.experimental import pallas as pl
from jax.experimental.pallas import tpu as pltpu

F32 = jnp.float32
BF16 = jnp.bfloat16

D_MODEL = 1024
DEPTH = 4
CHUNK = 64
PLE_DIM = 256
EPS = 1e-6
NEG_INF = -1e30
MLA_HEADS = 8
MLA_NOPE = 64
MLA_ROPE = 32
MLA_QK = MLA_NOPE + MLA_ROPE
MLA_V = 64
MLA_Q_LORA = 384
MLA_KV_LORA = 256
ROPE_THETA = 10000.0
CHK_HEADS = 8
CHK_DIM = 64
LEFT_CHUNKS = 8
REL_CLIP = 256
FOX_HEADS = 16
FOX_DIM = 64
MLA_WIDTH = MLA_HEADS * MLA_V
CHK_WIDTH = CHK_HEADS * CHK_DIM
FOX_WIDTH = FOX_HEADS * FOX_DIM

TM = 512
TQ = 512
TQ_DENSE = 1024
TQ_DECAY = 1024
UNDERFLOW_LOG2 = 160.0
MAX_SHIFT_BOUND = 45.0
TK = 256
TC = 256
OUT_SUB = 256
PROJ_ROWS = 256
QK_PAD = 128
V_PAD = 80
LOOKAHEAD = 1
LOG2E = 1.4426950408889634
VMEM_LIMIT = 56 * 1024 * 1024


def _nt(a, b):
    return lax.dot_general(a, b, (((1,), (1,)), ((), ())), preferred_element_type=F32)


def _nn(a, b):
    return jnp.dot(a, b, preferred_element_type=F32)


def _sigmoid(x):
    return 1.0 / (1.0 + jnp.exp(-x))


def _rms_rows(x, g):
    ms = jnp.mean(x * x, axis=-1, keepdims=True)
    return x * lax.rsqrt(ms + EPS) * g


def _split3(x):
    hi = x.astype(BF16)
    r1 = x - hi.astype(F32)
    mid = r1.astype(BF16)
    lo = (r1 - mid.astype(F32)).astype(BF16)
    return hi, mid, lo


def _store_kv_blocks(k_ref, vt_ref, hd, k_t, v_t, ones_row):
    k_rows = k_t.T.astype(BF16)
    v_aug = jnp.concatenate([v_t.astype(BF16), ones_row], axis=0)
    for c in range(TM // TK):
        k_ref[0, hd, c] = k_rows[c * TK:(c + 1) * TK]
        vt_ref[0, hd, c] = v_aug[:, c * TK:(c + 1) * TK]


def _rope_t(x, cos, sin):
    half = x.shape[0] // 2
    x1, x2 = x[:half], x[half:]
    return jnp.concatenate([x1 * cos - x2 * sin, x2 * cos + x1 * sin], axis=0)


def _fox_proj_kernel(h_ref, ng_ref, w_ref, bf_ref, qg_ref, kg_ref,
                     tri_ref, bound_ref, qa_ref, ka_ref, vt_ref, gt_ref, cum_ref, carry_ref):
    t = pl.program_id(1)
    u = _rms_rows(h_ref[0], ng_ref[...]).astype(BF16)

    wq0, wk0, wv0, wg0 = 0, FOX_WIDTH, 2 * FOX_WIDTH, 3 * FOX_WIDTH
    g_last = _nt(w_ref[wg0 + FOX_WIDTH - PROJ_ROWS:wg0 + FOX_WIDTH + FOX_HEADS, :], u)
    z = g_last[PROJ_ROWS:PROJ_ROWS + FOX_HEADS] + bf_ref[...]
    logf = jnp.minimum(z, 0.0) - jnp.log(1.0 + jnp.exp(-jnp.abs(z)))
    parts = jnp.concatenate(_split3(logf), axis=0)
    c3 = _nn(parts, tri_ref[...])
    local = c3[0:FOX_HEADS] + c3[FOX_HEADS:2 * FOX_HEADS] + c3[2 * FOX_HEADS:3 * FOX_HEADS]

    @pl.when(t == 0)
    def _():
        carry_ref[...] = jnp.zeros_like(carry_ref)

    cum = local + carry_ref[:, 0:1]
    carry_ref[...] = jnp.broadcast_to(cum[:, TM - 1:TM], carry_ref.shape)
    c_hi, c_mid, c_lo = [p.astype(F32) for p in _split3(cum * LOG2E)]
    cum_ref[0] = cum * LOG2E
    d_hi, d_mid, d_lo = [p.astype(F32) for p in _split3(cum * LOG2E + bound_ref[...])]

    row = lax.broadcasted_iota(jnp.int32, (16, TM), 0)
    ones_row = jnp.where(row == 0, 1.0, 0.0).astype(BF16)
    zeros48 = jnp.zeros((QK_PAD - FOX_DIM - 16, TM), F32)
    qg = qg_ref[...]
    kg = kg_ref[...]
    heads_per_chunk = PROJ_ROWS // FOX_DIM
    for c in range(FOX_WIDTH // PROJ_ROWS):
        rows = slice(c * PROJ_ROWS, (c + 1) * PROJ_ROWS)
        r0, r1 = rows.start, rows.stop
        qc = _nt(w_ref[wq0 + r0:wq0 + r1, :], u)
        kc = _nt(w_ref[wk0 + r0:wk0 + r1, :], u)
        vc = _nt(w_ref[wv0 + r0:wv0 + r1, :], u)
        gc = g_last[0:PROJ_ROWS] if r1 == FOX_WIDTH else _nt(w_ref[wg0 + r0:wg0 + r1, :], u)
        gt_ref[0, rows, :] = gc.astype(BF16)
        for j in range(heads_per_chunk):
            hd = c * heads_per_chunk + j
            hs = slice(j * FOX_DIM, (j + 1) * FOX_DIM)
            hi, mid, lo = c_hi[hd:hd + 1], c_mid[hd:hd + 1], c_lo[hd:hd + 1]
            khi, kmid, klo = d_hi[hd:hd + 1], d_mid[hd:hd + 1], d_lo[hd:hd + 1]
            qh = qc[hs]
            qn = qh * lax.rsqrt(jnp.mean(qh * qh, axis=0, keepdims=True) + EPS) * qg
            exq = jnp.where(row < 3, 1.0, jnp.where(row == 3, hi, jnp.where(row == 4, mid,
                            jnp.where(row == 5, lo, 0.0))))
            qa_ref[0, hd, 0:FOX_DIM, :] = qn.astype(BF16)
            qa_ref[0, hd, FOX_DIM:FOX_DIM + 16, :] = exq.astype(BF16)
            qa_ref[0, hd, FOX_DIM + 16:QK_PAD, :] = zeros48.astype(BF16)
            kh = kc[hs]
            kn = kh * lax.rsqrt(jnp.mean(kh * kh, axis=0, keepdims=True) + EPS) * kg
            exk = jnp.where(row == 0, -khi, jnp.where(row == 1, -kmid, jnp.where(row == 2, -klo,
                            jnp.where(row < 6, 1.0, 0.0))))
            kfull = jnp.concatenate([kn, exk, zeros48], axis=0)
            _store_kv_blocks(ka_ref, vt_ref, hd, kfull, vc[hs], ones_row)


def _fox_proj(h, ng, w_t, bfc, qg, kg, tri, bound):
    B, S, _ = h.shape
    nt = S // TM
    full = lambda shape: pl.BlockSpec(shape, lambda b, t: (0,) * len(shape))
    return pl.pallas_call(
        _fox_proj_kernel,
        grid=(B, nt),
        in_specs=[
            pl.BlockSpec((1, TM, D_MODEL), lambda b, t: (b, t, 0)),
            full((1, D_MODEL)),
            full((4 * FOX_WIDTH + FOX_HEADS, D_MODEL)),
            full((FOX_HEADS, 1)),
            full((FOX_DIM, 1)), full((FOX_DIM, 1)),
            full((TM, TM)), full((1, 1)),
        ],
        out_specs=[
            pl.BlockSpec((1, FOX_HEADS, QK_PAD, TM), lambda b, t: (b, 0, 0, t)),
            pl.BlockSpec((1, FOX_HEADS, TM // TK, TK, QK_PAD), lambda b, t: (b, 0, t, 0, 0)),
            pl.BlockSpec((1, FOX_HEADS, TM // TK, V_PAD, TK), lambda b, t: (b, 0, t, 0, 0)),
            pl.BlockSpec((1, FOX_WIDTH, TM), lambda b, t: (b, 0, t)),
            pl.BlockSpec((1, FOX_HEADS, TM), lambda b, t: (b, 0, t)),
        ],
        out_shape=[
            jax.ShapeDtypeStruct((B, FOX_HEADS, QK_PAD, S), BF16),
            jax.ShapeDtypeStruct((B, FOX_HEADS, S // TK, TK, QK_PAD), BF16),
            jax.ShapeDtypeStruct((B, FOX_HEADS, S // TK, V_PAD, TK), BF16),
            jax.ShapeDtypeStruct((B, FOX_WIDTH, S), BF16),
            jax.ShapeDtypeStruct((B, FOX_HEADS, S), F32),
        ],
        scratch_shapes=[pltpu.VMEM((FOX_HEADS, 128), F32)],
        compiler_params=pltpu.CompilerParams(
            dimension_semantics=("parallel", "arbitrary"), vmem_limit_bytes=VMEM_LIMIT),
        name="fox_proj",
    )(h, ng, w_t, bfc, qg, kg, tri, bound)


def _ab_proj_kernel(h_ref, ng_ref, w_ref, qnorm_ref, wuq_ref, kvnorm_ref, wukv_ref, qg_ref, kg_ref,
                    cqg_ref, ckg_ref,
                    cos_ref, sin_ref, bound_ref,
                    qt_ref, k_ref, vt_ref, gt_ref, qbt_ref, kb_ref, vbt_ref):
    u = _rms_rows(h_ref[0], ng_ref[...]).astype(BF16)
    cos = cos_ref[0]
    sin = sin_ref[0]

    off = np.cumsum((0, MLA_Q_LORA, MLA_KV_LORA, MLA_ROPE, MLA_WIDTH) + (CHK_WIDTH,) * 4)

    def rms_cols(x, g):
        return x * lax.rsqrt(jnp.mean(x * x, axis=0, keepdims=True) + EPS) * g

    lat = _nt(w_ref[off[0]:off[3], :], u)
    cqn = rms_cols(lat[off[0]:off[1]], qnorm_ref[...]).astype(BF16)
    ckvn = rms_cols(lat[off[1]:off[2]], kvnorm_ref[...]).astype(BF16)
    krt = lat[off[2]:off[3]]
    ss_kr = jnp.sum(krt * krt, axis=0, keepdims=True)
    q_all = _nn(wuq_ref[...], cqn)
    kv_all = _nn(wukv_ref[...], ckvn)
    for g0, w0 in ((0, off[3]), (MLA_WIDTH, off[7])):
        for c in range(MLA_WIDTH // 256):
            gt_ref[0, g0 + c * 256:g0 + (c + 1) * 256, :] = _nt(
                w_ref[w0 + c * 256:w0 + (c + 1) * 256, :], u).astype(BF16)
    qg = qg_ref[...]
    kg = kg_ref[...]
    row32 = lax.broadcasted_iota(jnp.int32, (QK_PAD - MLA_QK, TM), 0)
    q_pad = jnp.where(row32 == 0, 1.0, 0.0)
    k_pad = jnp.where(row32 == 0, -bound_ref[...], 0.0)
    ones_row =jnp.where(lax.broadcasted_iota(jnp.int32, (V_PAD - MLA_V, TM), 0) == 0, 1.0, 0.0).astype(BF16)
    for hd in range(MLA_HEADS):
        qh = q_all[hd * MLA_QK:(hd + 1) * MLA_QK]
        qn = qh * lax.rsqrt(jnp.mean(qh * qh, axis=0, keepdims=True) + EPS) * qg
        qt_ref[0, hd, 0:MLA_NOPE, :] = qn[0:MLA_NOPE].astype(BF16)
        qt_ref[0, hd, MLA_NOPE:MLA_QK, :] = _rope_t(qn[MLA_NOPE:MLA_QK], cos, sin).astype(BF16)
        qt_ref[0, hd, MLA_QK:QK_PAD, :] = q_pad.astype(BF16)
        kv = kv_all[hd * 128:(hd + 1) * 128]
        kn = kv[0:MLA_NOPE]
        rk = lax.rsqrt((jnp.sum(kn * kn, axis=0, keepdims=True) + ss_kr) * (1.0 / MLA_QK) + EPS)
        kfull = jnp.concatenate([kn * rk * kg[0:MLA_NOPE],
                                 _rope_t(krt * rk * kg[MLA_NOPE:MLA_QK], cos, sin),
                                 k_pad], axis=0)
        _store_kv_blocks(k_ref, vt_ref, hd, kfull, kv[MLA_NOPE:MLA_NOPE + MLA_V], ones_row)

    cqg = cqg_ref[...]
    ckg = ckg_ref[...]
    zeros64 = jnp.zeros((CHK_DIM, TM), F32)
    for c in range(CHK_WIDTH // 256):
        rows = slice(c * 256, (c + 1) * 256)
        qc = _nt(w_ref[off[4] + c * 256:off[4] + (c + 1) * 256, :], u)
        kc = _nt(w_ref[off[5] + c * 256:off[5] + (c + 1) * 256, :], u)
        vc = _nt(w_ref[off[6] + c * 256:off[6] + (c + 1) * 256, :], u).astype(BF16)
        for j in range(4):
            vbt_ref[0, c * 4 + j, 0:CHK_DIM, :] = vc[j * CHK_DIM:(j + 1) * CHK_DIM]
            vbt_ref[0, c * 4 + j, CHK_DIM:V_PAD, :] = ones_row
        kns = []
        for j in range(4):
            hd = c * 4 + j
            hs = slice(j * CHK_DIM, (j + 1) * CHK_DIM)
            qh = qc[hs]
            qn = (qh * lax.rsqrt(jnp.mean(qh * qh, axis=0, keepdims=True) + EPS) * cqg).astype(BF16)
            lo, hi = (0, CHK_DIM) if hd % 2 == 0 else (CHK_DIM, 2 * CHK_DIM)
            qbt_ref[0, hd, lo:hi, :] = qn
            qbt_ref[0, hd, CHK_DIM - lo:2 * CHK_DIM - lo, :] = zeros64.astype(BF16)
            kh = kc[hs]
            kns.append(kh * lax.rsqrt(jnp.mean(kh * kh, axis=0, keepdims=True) + EPS) * ckg)
        for pr in range(2):
            pair = jnp.concatenate([kns[2 * pr], kns[2 * pr + 1]], axis=0)
            kb_ref[0, c * 2 + pr] = pair.T.astype(BF16)


def _ab_proj(h, ng, w_t, qnorm, wuq, kvnorm, wukv, qg, kg, cqg, ckg, cos, sin, bound):
    B, S, _ = h.shape
    nt = S // TM
    full = lambda shape: pl.BlockSpec(shape, lambda b, t: (0,) * len(shape))
    return pl.pallas_call(
        _ab_proj_kernel,
        grid=(B, nt),
        in_specs=[
            pl.BlockSpec((1, TM, D_MODEL), lambda b, t: (b, t, 0)),
            full((1, D_MODEL)),
            full(w_t.shape),
            full((MLA_Q_LORA, 1)), full((MLA_HEADS * MLA_QK, MLA_Q_LORA)),
            full((MLA_KV_LORA, 1)), full((MLA_HEADS * 128, MLA_KV_LORA)),
            full((MLA_QK, 1)), full((MLA_QK, 1)), full((CHK_DIM, 1)), full((CHK_DIM, 1)),
            pl.BlockSpec((1, MLA_ROPE // 2, TM), lambda b, t: (b, 0, t)),
            pl.BlockSpec((1, MLA_ROPE // 2, TM), lambda b, t: (b, 0, t)),
            full((1, 1)),
        ],
        out_specs=[
            pl.BlockSpec((1, MLA_HEADS, QK_PAD, TM), lambda b, t: (b, 0, 0, t)),
            pl.BlockSpec((1, MLA_HEADS, TM // TK, TK, QK_PAD), lambda b, t: (b, 0, t, 0, 0)),
            pl.BlockSpec((1, MLA_HEADS, TM // TK, V_PAD, TK), lambda b, t: (b, 0, t, 0, 0)),
            pl.BlockSpec((1, MLA_WIDTH + CHK_WIDTH, TM), lambda b, t: (b, 0, t)),
            pl.BlockSpec((1, CHK_HEADS, 2 * CHK_DIM, TM), lambda b, t: (b, 0, 0, t)),
            pl.BlockSpec((1, CHK_HEADS // 2, TM, 2 * CHK_DIM), lambda b, t: (b, 0, t, 0)),
            pl.BlockSpec((1, CHK_HEADS, V_PAD, TM), lambda b, t: (b, 0, 0, t)),
        ],
        out_shape=[
            jax.ShapeDtypeStruct((B, MLA_HEADS, QK_PAD, S), BF16),
            jax.ShapeDtypeStruct((B, MLA_HEADS, S // TK, TK, QK_PAD), BF16),
            jax.ShapeDtypeStruct((B, MLA_HEADS, S // TK, V_PAD, TK), BF16),
            jax.ShapeDtypeStruct((B, MLA_WIDTH + CHK_WIDTH, S), BF16),
            jax.ShapeDtypeStruct((B, CHK_HEADS, 2 * CHK_DIM, S), BF16),
            jax.ShapeDtypeStruct((B, CHK_HEADS // 2, S, 2 * CHK_DIM), BF16),
            jax.ShapeDtypeStruct((B, CHK_HEADS, V_PAD, S), BF16),
        ],
        compiler_params=pltpu.CompilerParams(
            dimension_semantics=("parallel", "parallel"), vmem_limit_bytes=VMEM_LIMIT),
        name="ab_proj",
    )(h, ng, w_t, qnorm, wuq, kvnorm, wukv, qg, kg, cqg, ckg, cos, sin, bound)


def _causal_attn_kernel(qt_ref, k_ref, vt_ref, o_ref, *, gran_shift):
    qi = pl.program_id(2)
    dv = o_ref.shape[1]
    n_sub = TQ // TK
    qts = [qt_ref[0, 0, :, c * TK:(c + 1) * TK] for c in range(n_sub)]

    def scores(c, j, masked):
        s = _nn(k_ref[0, 0, j], qts[c])
        if masked:
            kpos = lax.broadcasted_iota(jnp.int32, (TK, TK), 0)
            qpos = lax.broadcasted_iota(jnp.int32, (TK, TK), 1)
            s = jnp.where((kpos >> gran_shift) <= (qpos >> gran_shift), s, NEG_INF)
        return s

    def update(s, j, carry):
        m, acc = carry
        m_new = jnp.maximum(m, jnp.max(s, axis=0, keepdims=True))
        p = jnp.exp2(s - m_new)
        acc = jnp.exp2(m - m_new) * acc + _nn(vt_ref[0, 0, j], p.astype(BF16))
        return m_new, acc

    def run(work, carries):
        carries = list(carries)
        pending = [scores(*w) for w in work[:LOOKAHEAD]]
        for i, (c, j, _) in enumerate(work):
            if i + LOOKAHEAD < len(work):
                pending.append(scores(*work[i + LOOKAHEAD]))
            carries[c] = update(pending.pop(0), j, carries[c])
        return tuple(carries)

    def steps(jj, carries):
        return run([(c, jj * n_sub + u, False) for u in range(n_sub) for c in range(n_sub)], carries)

    init = (jnp.full((1, TK), NEG_INF, F32), jnp.zeros((V_PAD, TK), F32))
    carries = lax.fori_loop(0, qi, steps, (init,) * n_sub)
    carries = run([(c, qi * n_sub + u, u == c) for c in range(n_sub) for u in range(c + 1)], carries)
    for c in range(n_sub):
        acc = carries[c][1]
        o_ref[0, :, c * TK:(c + 1) * TK] = (acc[0:dv] / acc[dv:dv + 1]).astype(o_ref.dtype)


def _shifted_attn_kernel(first_ref, reach_ref, qt_ref, k_ref, vt_ref, o_ref, *, gran_shift, long_step):
    qi = pl.program_id(2)
    step = (pl.program_id(0) * pl.num_programs(1) + pl.program_id(1)) * pl.num_programs(2) + qi
    g0 = first_ref[step]
    reach = reach_ref[step]
    dv = o_ref.shape[1]
    tq = qt_ref.shape[3]
    nb = tq // TK
    qt = qt_ref[0, 0]

    def probs(j, lo, hi, masked):
        s = _nn(k_ref[0, 0, j], qt[:, lo:hi])
        if masked:
            kpos = lax.broadcasted_iota(jnp.int32, s.shape, 0)
            qpos = lax.broadcasted_iota(jnp.int32, s.shape, 1)
            s = jnp.where((kpos >> gran_shift) <= (qpos >> gran_shift), s, NEG_INF)
        return jnp.exp2(s).astype(BF16)

    def run(work, acc):
        pending = [probs(*w) for w in work[:LOOKAHEAD]]
        for i, (j, lo, hi, _) in enumerate(work):
            if i + LOOKAHEAD < len(work):
                pending.append(probs(*work[i + LOOKAHEAD]))
            pv = _nn(vt_ref[0, 0, j], pending.pop(0))
            parts = [acc[:, lo:hi] + pv]
            if lo > 0:
                parts.insert(0, acc[:, :lo])
            if hi < tq:
                parts.append(acc[:, hi:])
            acc = parts[0] if len(parts) == 1 else jnp.concatenate(parts, axis=1)
        return acc

    def full_groups(n_grp):
        return lambda jj, a: run([((g0 + jj * n_grp) * nb + u, 0, tq, False) for u in range(n_grp * nb)], a)

    n_loop = jnp.maximum(qi - 1 - g0, 0)
    n_long = n_loop // long_step
    acc = lax.fori_loop(0, n_long, full_groups(long_step), jnp.zeros((V_PAD, tq), F32))
    if long_step > 1:
        acc = lax.fori_loop(n_long * long_step, n_loop, full_groups(1), acc)
    diag = [(qi * nb + u, u * TK, tq, True) for u in range(nb)]

    def tail(max_reach):
        prev = [((qi - 1) * nb + u, 0, min(max_reach - (nb - 1 - u) + 1, nb) * TK, False)
                for u in range(nb) if max_reach - (nb - 1 - u) >= 0]
        return lambda a: run(prev + diag, a)

    levels = (nb - 2, nb - 1, 2 * nb - 2)
    level = jnp.where(reach <= levels[0], 0, jnp.where(reach <= levels[1], 1, 2))
    acc = lax.switch(jnp.where(qi > 0, level, len(levels)),
                     [tail(r) for r in levels] + [lambda a: run(diag, a)], acc)
    o_ref[0] = (acc[0:dv] / acc[dv:dv + 1]).astype(o_ref.dtype)


def _causal_attn(qt, k, vt, gran, bound, first_group, reach, tqf, long_step):
    B, H, _, S = qt.shape
    nk, dv = k.shape[2], FOX_DIM
    assert TQ % TK == 0 and tqf % TK == 0 and MLA_V == FOX_DIM and vt.shape[3] == V_PAD

    def call(body, tq, name, prefetch, **static):
        grid_spec = pltpu.PrefetchScalarGridSpec(
            num_scalar_prefetch=len(prefetch),
            grid=(B, H, S // tq),
            in_specs=[
                pl.BlockSpec((1, 1, QK_PAD, tq), lambda b, h, q, *_: (b, h, 0, q)),
                pl.BlockSpec((1, 1, nk, TK, QK_PAD), lambda b, h, q, *_: (b, h, 0, 0, 0)),
                pl.BlockSpec((1, 1, nk, V_PAD, TK), lambda b, h, q, *_: (b, h, 0, 0, 0)),
            ],
            out_specs=pl.BlockSpec((1, dv, tq), lambda b, h, q, *_: (b, h, q)),
        )
        return pl.pallas_call(
            functools.partial(body, gran_shift=int(np.log2(gran)), **static),
            grid_spec=grid_spec,
            out_shape=jax.ShapeDtypeStruct((B, H * dv, S), BF16),
            compiler_params=pltpu.CompilerParams(
                dimension_semantics=("parallel", "parallel", "arbitrary"), vmem_limit_bytes=VMEM_LIMIT),
            name=name,
        )(*prefetch, qt, k, vt)

    return lax.cond(bound <= MAX_SHIFT_BOUND,
                    lambda: call(_shifted_attn_kernel, tqf, "shifted_attn",
                                 (first_group.reshape(-1), reach.reshape(-1)),
                                 long_step=long_step),
                    lambda: call(_causal_attn_kernel, TQ, "causal_attn", ()))


def _chunk_attn_kernel(qt_ref, k0_ref, k1_ref, k2_ref, v0_ref, v1_ref, v2_ref, bias_ref, o_ref, *,
                       shifted):
    t = pl.program_id(1)
    k_refs = (k0_ref, k1_ref, k2_ref)
    v_refs = (v0_ref, v1_ref, v2_ref)

    def scores(hd):
        qt = qt_ref[0, hd]
        ss = []
        for d in range(3):
            s = _nn(k_refs[d][0, hd // 2], qt) + bias_ref[hd, d * TC:(d + 1) * TC, :]
            if d < 2:
                s = jnp.where(t + (d - 2) >= 0, s, NEG_INF)
            ss.append(s)
        return ss

    def finish(hd, ss):
        if not shifted:
            m = jnp.max(jnp.maximum(jnp.maximum(ss[0], ss[1]), ss[2]), axis=0, keepdims=True)
            ss = [s - m for s in ss]
        p = jnp.concatenate([jnp.exp2(s).astype(BF16) for s in ss], axis=0)
        vt = jnp.concatenate([r[0, hd] for r in v_refs], axis=1)
        acc = _nn(vt, p)
        o_ref[0, hd * CHK_DIM:(hd + 1) * CHK_DIM, :] = (
            acc[0:CHK_DIM] / acc[CHK_DIM:CHK_DIM + 1]).astype(o_ref.dtype)

    pending = [scores(0)]
    for hd in range(CHK_HEADS):
        if hd + 1 < CHK_HEADS:
            pending.append(scores(hd + 1))
        finish(hd, pending.pop(0))


def _chunk_attn(qbt, kb, vbt, bias_t, spread):
    B, _, _, S = qbt.shape
    kspec = lambda d: pl.BlockSpec((1, CHK_HEADS // 2, TC, 2 * CHK_DIM),
                                   lambda b, t: (b, 0, jnp.maximum(t + (d - 2), 0), 0))
    vspec = lambda d: pl.BlockSpec((1, CHK_HEADS, V_PAD, TC),
                                   lambda b, t: (b, 0, 0, jnp.maximum(t + (d - 2), 0)))

    def call(shifted):
        return pl.pallas_call(
            functools.partial(_chunk_attn_kernel, shifted=shifted),
            grid=(B, S // TC),
            in_specs=[
                pl.BlockSpec((1, CHK_HEADS, 2 * CHK_DIM, TC), lambda b, t: (b, 0, 0, t)),
                kspec(0), kspec(1), kspec(2), vspec(0), vspec(1), vspec(2),
                pl.BlockSpec((CHK_HEADS, 3 * TC, TC), lambda b, t: (0, 0, 0)),
            ],
            out_specs=pl.BlockSpec((1, CHK_WIDTH, TC), lambda b, t: (b, 0, t)),
            out_shape=jax.ShapeDtypeStruct((B, CHK_WIDTH, S), BF16),
            compiler_params=pltpu.CompilerParams(
                dimension_semantics=("parallel", "parallel"), vmem_limit_bytes=VMEM_LIMIT),
            name="chunk_attn_shifted" if shifted else "chunk_attn",
        )(qbt, kb, kb, kb, vbt, vbt, vbt, bias_t)

    return lax.cond(spread <= 2 * MAX_SHIFT_BOUND, lambda: call(True), lambda: call(False))


def _out_ple_kernel(*refs, n_o):
    o_refs = refs[:n_o]
    gt_ref, h_ref, p_ref, wo_ref, pgn_ref, wgate_ref, pew_ref, out_ref = refs[n_o:]
    n_sub = TM // OUT_SUB
    subs = [slice(i * OUT_SUB, (i + 1) * OUT_SUB) for i in range(n_sub)]

    def gated(ts):
        ot = jnp.concatenate([r[0, :, ts] for r in o_refs], axis=0) if n_o > 1 else o_refs[0][0, :, ts]
        g = gt_ref[0, :, ts].astype(F32)
        return (ot.astype(F32) * (g * _sigmoid(g))).astype(BF16)

    mixed_t = [_nn(wo_ref[...], gated(ts)) for ts in subs]
    h1 = [h_ref[0, ts, :] + m.T for ts, m in zip(subs, mixed_t)]
    gate = [_nn(_rms_rows(x, pgn_ref[...]).astype(BF16), wgate_ref[...]) for x in h1]
    pe = [_nn(p_ref[0, 0, ts, :].astype(BF16), pew_ref[...]) for ts in subs]
    for ts, x, gt, e in zip(subs, h1, gate, pe):
        out_ref[0, ts, :] = x + e * _sigmoid(gt)


def _out_ple(o_list, gt, h, p, layer, wo_t, pgn, wgate, pew):
    B, S, _ = h.shape
    full = lambda shape: pl.BlockSpec(shape, lambda b, t: (0,) * len(shape))
    width = gt.shape[1]
    return pl.pallas_call(
        functools.partial(_out_ple_kernel, n_o=len(o_list)),
        grid=(B, S // TM),
        in_specs=[pl.BlockSpec((1, o.shape[1], TM), lambda b, t: (b, 0, t)) for o in o_list] + [
            pl.BlockSpec((1, width, TM), lambda b, t: (b, 0, t)),
            pl.BlockSpec((1, TM, D_MODEL), lambda b, t: (b, t, 0)),
            pl.BlockSpec((1, 1, TM, PLE_DIM), lambda b, t: (layer, b, t, 0)),
            full((D_MODEL, width)), full((1, D_MODEL)), full((D_MODEL, D_MODEL)),
            full((PLE_DIM, D_MODEL)),
        ],
        out_specs=pl.BlockSpec((1, TM, D_MODEL), lambda b, t: (b, t, 0)),
        out_shape=jax.ShapeDtypeStruct((B, S, D_MODEL), F32),
        compiler_params=pltpu.CompilerParams(
            dimension_semantics=("parallel", "parallel"), vmem_limit_bytes=VMEM_LIMIT),
        name="out_ple",
    )(*o_list, gt, h, p, wo_t, pgn, wgate, pew)


def _col(v):
    return v.astype(F32).reshape(-1, 1)


def _score_bound(qg, kg, d):
    return 1.02 * d * jnp.max(jnp.abs(qg)) * jnp.max(jnp.abs(kg)) + 1.0


def _first_live_group(cum2, tqf):
    nb = tqf // TK
    first_q = cum2[:, :, ::tqf]
    last_k = cum2[:, :, TK - 1::TK]
    dead = last_k[:, :, None, :] > first_q[:, :, :, None] + UNDERFLOW_LOG2
    return (jnp.sum(dead, axis=-1) // nb).astype(jnp.int32)


def _prev_reach(cum2, tqf):
    B, H, S = cum2.shape
    nb = tqf // TK
    first_q = cum2[:, :, ::TK].reshape(B, H, S // tqf, nb)
    last_k = jnp.roll(cum2[:, :, TK - 1::TK].reshape(B, H, S // tqf, nb), 1, axis=2)
    live = first_q[:, :, :, None, :] - last_k[:, :, :, :, None] >= -UNDERFLOW_LOG2
    gap = np.arange(nb)[None, :] + (nb - 1 - np.arange(nb))[:, None]
    return jnp.max(jnp.where(live, gap, -1), axis=(3, 4)).astype(jnp.int32)


def _chunk_bias_table(rel_bias, bound_qk):
    n = 4 * TC
    i = np.arange(n)
    dist = np.where(i < TC, i, i - n) + 2 * TC
    rb = rel_bias.astype(F32) * LOG2E
    spread = 2.0 * bound_qk + (jnp.max(rb) - jnp.min(rb))
    e = rb[:, np.clip(dist, -REL_CLIP, REL_CLIP) + REL_CLIP] - (bound_qk + jnp.max(rb))
    h = rel_bias.shape[0]
    return _bias_table(e.reshape(h, 1, n)), spread


def _bias_table(e):
    h, _, n = e.shape
    return pl.pallas_call(
        _bias_table_kernel,
        grid=(h,),
        in_specs=[pl.BlockSpec((1, 1, n), lambda hd: (hd, 0, 0))],
        out_specs=pl.BlockSpec((1, 3 * TC, TC), lambda hd: (hd, 0, 0)),
        out_shape=jax.ShapeDtypeStruct((h, 3 * TC, TC), F32),
        name="bias_table",
    )(e)


def _bias_table_kernel(e_ref, o_ref):
    n = e_ref.shape[2]
    rows = jnp.broadcast_to(e_ref[0], (3 * TC, n))
    skew = pltpu.roll(rows, 0, 1, stride=1, stride_axis=0)
    kchunk = lax.broadcasted_iota(jnp.int32, (3 * TC, TC), 0) // CHUNK
    qchunk = lax.broadcasted_iota(jnp.int32, (3 * TC, TC), 1) // CHUNK + (2 * TC) // CHUNK
    valid = (kchunk <= qchunk) & (kchunk >= qchunk - LEFT_CHUNKS)
    o_ref[0] = jnp.where(valid, skew[:, 0:TC], NEG_INF)


def kernel(x, p, positions, norm_g, ab_w_in, mla_q_norm, mla_w_uq, mla_kv_norm, mla_w_ukv, mla_q_gain, mla_k_gain, chk_q_gain, chk_k_gain, chk_rel_bias, ab_w_out, fox_w_in, fox_b_f, fox_q_gain, fox_k_gain, fox_w_out, pe_w, pe_gate_norm, pe_gate_w):
    B, S, _ = x.shape
    half = MLA_ROPE // 2
    inv_freq = 1.0 / (ROPE_THETA ** (jnp.arange(half, dtype=F32) / half))
    ang = positions.astype(F32)[:, None, :] * inv_freq[None, :, None]
    cos, sin = jnp.cos(ang), jnp.sin(ang)
    tri = (np.arange(TM)[:, None] <= np.arange(TM)[None, :]).astype(np.float32)
    tri = jnp.asarray(tri, BF16)

    h = x
    for i in range(DEPTH):
        l = i // 2
        ng = norm_g[i].astype(F32).reshape(1, -1)
        if i % 2 == 0:
            qg, kg = _col(mla_q_gain[l]) * (MLA_QK ** -0.5 * LOG2E), _col(mla_k_gain[l])
            bound = _score_bound(qg, kg, MLA_QK)
            cqg, ckg = _col(chk_q_gain[l]) * (CHK_DIM ** -0.5 * LOG2E), _col(chk_k_gain[l])
            qt, k, vt, gt, qbt, kb, vbt = _ab_proj(
                h, ng, ab_w_in[l].T.astype(BF16),
                _col(mla_q_norm[l]), mla_w_uq[l].T.astype(BF16),
                _col(mla_kv_norm[l]), mla_w_ukv[l].T.astype(BF16),
                qg, kg, cqg, ckg, cos, sin, bound.reshape(1, 1))
            dense = jnp.zeros((B, MLA_HEADS, S // TQ_DENSE), jnp.int32)
            o_a = _causal_attn(qt, k, vt, CHUNK, bound, dense, dense + 2 * (TQ_DENSE // TK), TQ_DENSE, 2)
            bias_t, spread = _chunk_bias_table(chk_rel_bias[l], _score_bound(cqg, ckg, CHK_DIM))
            o_b = _chunk_attn(qbt, kb, vbt, bias_t, spread)
            o_list, wo = [o_a, o_b], ab_w_out[l]
        else:
            qg, kg = _col(fox_q_gain[l]) * (FOX_DIM ** -0.5 * LOG2E), _col(fox_k_gain[l])
            bound = _score_bound(qg, kg, FOX_DIM)
            qa, ka, vt, gt, cum2 = _fox_proj(h, ng, fox_w_in[l].T.astype(BF16), _col(fox_b_f[l]), qg, kg,
                                             tri, bound.reshape(1, 1))
            o_c = _causal_attn(qa, ka, vt, 1, bound, _first_live_group(cum2, TQ_DECAY),
                               _prev_reach(cum2, TQ_DECAY), TQ_DECAY, 2)
            o_list, wo = [o_c], fox_w_out[l]
        h = _out_ple(o_list, gt, h, p, i, wo.T.astype(BF16),
                     pe_gate_norm[i].astype(F32).reshape(1, -1),
                     pe_gate_w[i].astype(BF16), pe_w[i].astype(BF16))
    return h
```

```python
import functools

import numpy as np
import jax
import jax.numpy as jnp
from jax import lax
from jax.experimental import pallas as pl
from jax.experimental.pallas import tpu as pltpu

F32 = jnp.float32
BF16 = jnp.bfloat16

D_MODEL = 1024
DEPTH = 4
CHUNK = 64
PLE_DIM = 256
EPS = 1e-6
NEG_INF = -1e30
MLA_HEADS = 8
MLA_NOPE = 64
MLA_ROPE = 32
MLA_QK = MLA_NOPE + MLA_ROPE
MLA_V = 64
MLA_Q_LORA = 384
MLA_KV_LORA = 256
ROPE_THETA = 10000.0
CHK_HEADS = 8
CHK_DIM = 64
LEFT_CHUNKS = 8
REL_CLIP = 256
FOX_HEADS = 16
FOX_DIM = 64
MLA_WIDTH = MLA_HEADS * MLA_V
CHK_WIDTH = CHK_HEADS * CHK_DIM
FOX_WIDTH = FOX_HEADS * FOX_DIM

TM = 512
TQ = 512
TQ_DENSE = 1024
TQ_DECAY = 1024
UNDERFLOW_LOG2 = 160.0
MAX_SHIFT_BOUND = 45.0
TK = 256
TC = 256
OUT_SUB = 256
PROJ_ROWS = 256
QK_PAD = 128
V_PAD = 80
HEADS_PER_STEP = 4
LOOKAHEAD = 1
LOG2E = 1.4426950408889634
VMEM_LIMIT = 56 * 1024 * 1024


def _nt(a, b):
    return lax.dot_general(a, b, (((1,), (1,)), ((), ())), preferred_element_type=F32)


def _nn(a, b):
    return jnp.dot(a, b, preferred_element_type=F32)


def _sigmoid(x):
    return 1.0 / (1.0 + jnp.exp(-x))


def _rms_rows(x, g):
    ms = jnp.mean(x * x, axis=-1, keepdims=True)
    return x * lax.rsqrt(ms + EPS) * g


def _split3(x):
    hi = x.astype(BF16)
    r1 = x - hi.astype(F32)
    mid = r1.astype(BF16)
    lo = (r1 - mid.astype(F32)).astype(BF16)
    return hi, mid, lo


def _store_kv_blocks(k_ref, vt_ref, hd, k_t, v_t, ones_row):
    k_rows = k_t.T.astype(BF16)
    v_aug = jnp.concatenate([v_t.astype(BF16), ones_row], axis=0)
    for c in range(TM // TK):
        k_ref[0, hd, c] = k_rows[c * TK:(c + 1) * TK]
        vt_ref[0, hd, c] = v_aug[:, c * TK:(c + 1) * TK]


def _rope_t(x, cos, sin):
    half = x.shape[0] // 2
    x1, x2 = x[:half], x[half:]
    return jnp.concatenate([x1 * cos - x2 * sin, x2 * cos + x1 * sin], axis=0)


def _fox_proj_kernel(h_ref, ng_ref, w_ref, bf_ref, qg_ref, kg_ref,
                     tri_ref, bound_ref, qa_ref, ka_ref, vt_ref, gt_ref, cum_ref, carry_ref):
    t = pl.program_id(1)
    u = _rms_rows(h_ref[0], ng_ref[...]).astype(BF16)

    wq0, wk0, wv0, wg0 = 0, FOX_WIDTH, 2 * FOX_WIDTH, 3 * FOX_WIDTH
    g_last = _nt(w_ref[wg0 + FOX_WIDTH - PROJ_ROWS:wg0 + FOX_WIDTH + FOX_HEADS, :], u)
    z = g_last[PROJ_ROWS:PROJ_ROWS + FOX_HEADS] + bf_ref[...]
    logf = jnp.minimum(z, 0.0) - jnp.log(1.0 + jnp.exp(-jnp.abs(z)))
    parts = jnp.concatenate(_split3(logf), axis=0)
    c3 = _nn(parts, tri_ref[...])
    local = c3[0:FOX_HEADS] + c3[FOX_HEADS:2 * FOX_HEADS] + c3[2 * FOX_HEADS:3 * FOX_HEADS]

    @pl.when(t == 0)
    def _():
        carry_ref[...] = jnp.zeros_like(carry_ref)

    cum = local + carry_ref[:, 0:1]
    carry_ref[...] = jnp.broadcast_to(cum[:, TM - 1:TM], carry_ref.shape)
    c_hi, c_mid, c_lo = [p.astype(F32) for p in _split3(cum * LOG2E)]
    cum_ref[0] = cum * LOG2E
    d_hi, d_mid, d_lo = [p.astype(F32) for p in _split3(cum * LOG2E + bound_ref[...])]

    row = lax.broadcasted_iota(jnp.int32, (16, TM), 0)
    ones_row = jnp.where(row == 0, 1.0, 0.0).astype(BF16)
    zeros48 = jnp.zeros((QK_PAD - FOX_DIM - 16, TM), F32)
    qg = qg_ref[...]
    kg = kg_ref[...]
    heads_per_chunk = PROJ_ROWS // FOX_DIM
    for c in range(FOX_WIDTH // PROJ_ROWS):
        rows = slice(c * PROJ_ROWS, (c + 1) * PROJ_ROWS)
        r0, r1 = rows.start, rows.stop
        qc = _nt(w_ref[wq0 + r0:wq0 + r1, :], u)
        kc = _nt(w_ref[wk0 + r0:wk0 + r1, :], u)
        vc = _nt(w_ref[wv0 + r0:wv0 + r1, :], u)
        gc = g_last[0:PROJ_ROWS] if r1 == FOX_WIDTH else _nt(w_ref[wg0 + r0:wg0 + r1, :], u)
        gt_ref[0, rows, :] = gc.astype(BF16)
        for j in range(heads_per_chunk):
            hd = c * heads_per_chunk + j
            hs = slice(j * FOX_DIM, (j + 1) * FOX_DIM)
            hi, mid, lo = c_hi[hd:hd + 1], c_mid[hd:hd + 1], c_lo[hd:hd + 1]
            khi, kmid, klo = d_hi[hd:hd + 1], d_mid[hd:hd + 1], d_lo[hd:hd + 1]
            qh = qc[hs]
            qn = qh * lax.rsqrt(jnp.mean(qh * qh, axis=0, keepdims=True) + EPS) * qg
            exq = jnp.where(row < 3, 1.0, jnp.where(row == 3, hi, jnp.where(row == 4, mid,
                            jnp.where(row == 5, lo, 0.0))))
            qa_ref[0, hd, 0:FOX_DIM, :] = qn.astype(BF16)
            qa_ref[0, hd, FOX_DIM:FOX_DIM + 16, :] = exq.astype(BF16)
            qa_ref[0, hd, FOX_DIM + 16:QK_PAD, :] = zeros48.astype(BF16)
            kh = kc[hs]
            kn = kh * lax.rsqrt(jnp.mean(kh * kh, axis=0, keepdims=True) + EPS) * kg
            exk = jnp.where(row == 0, -khi, jnp.where(row == 1, -kmid, jnp.where(row == 2, -klo,
                            jnp.where(row < 6, 1.0, 0.0))))
            kfull = jnp.concatenate([kn, exk, zeros48], axis=0)
            _store_kv_blocks(ka_ref, vt_ref, hd, kfull, vc[hs], ones_row)


def _fox_proj(h, ng, w_t, bfc, qg, kg, tri, bound):
    B, S, _ = h.shape
    nt = S // TM
    full = lambda shape: pl.BlockSpec(shape, lambda b, t: (0,) * len(shape))
    return pl.pallas_call(
        _fox_proj_kernel,
        grid=(B, nt),
        in_specs=[
            pl.BlockSpec((1, TM, D_MODEL), lambda b, t: (b, t, 0)),
            full((1, D_MODEL)),
            full((4 * FOX_WIDTH + FOX_HEADS, D_MODEL)),
            full((FOX_HEADS, 1)),
            full((FOX_DIM, 1)), full((FOX_DIM, 1)),
            full((TM, TM)), full((1, 1)),
        ],
        out_specs=[
            pl.BlockSpec((1, FOX_HEADS, QK_PAD, TM), lambda b, t: (b, 0, 0, t)),
            pl.BlockSpec((1, FOX_HEADS, TM // TK, TK, QK_PAD), lambda b, t: (b, 0, t, 0, 0)),
            pl.BlockSpec((1, FOX_HEADS, TM // TK, V_PAD, TK), lambda b, t: (b, 0, t, 0, 0)),
            pl.BlockSpec((1, FOX_WIDTH, TM), lambda b, t: (b, 0, t)),
            pl.BlockSpec((1, FOX_HEADS, TM), lambda b, t: (b, 0, t)),
        ],
        out_shape=[
            jax.ShapeDtypeStruct((B, FOX_HEADS, QK_PAD, S), BF16),
            jax.ShapeDtypeStruct((B, FOX_HEADS, S // TK, TK, QK_PAD), BF16),
            jax.ShapeDtypeStruct((B, FOX_HEADS, S // TK, V_PAD, TK), BF16),
            jax.ShapeDtypeStruct((B, FOX_WIDTH, S), BF16),
            jax.ShapeDtypeStruct((B, FOX_HEADS, S), F32),
        ],
        scratch_shapes=[pltpu.VMEM((FOX_HEADS, 128), F32)],
        compiler_params=pltpu.CompilerParams(
            dimension_semantics=("parallel", "arbitrary"), vmem_limit_bytes=VMEM_LIMIT),
        name="fox_proj",
    )(h, ng, w_t, bfc, qg, kg, tri, bound)


def _ab_proj_kernel(h_ref, ng_ref, w_ref, qnorm_ref, wuq_ref, kvnorm_ref, wukv_ref, qg_ref, kg_ref,
                    cqg_ref, ckg_ref,
                    cos_ref, sin_ref, bound_ref,
                    qt_ref, k_ref, vt_ref, gt_ref, qbt_ref, kb_ref, vbt_ref):
    u = _rms_rows(h_ref[0], ng_ref[...]).astype(BF16)
    cos = cos_ref[0]
    sin = sin_ref[0]

    off = np.cumsum((0, MLA_Q_LORA, MLA_KV_LORA, MLA_ROPE, MLA_WIDTH) + (CHK_WIDTH,) * 4)

    def rms_cols(x, g):
        return x * lax.rsqrt(jnp.mean(x * x, axis=0, keepdims=True) + EPS) * g

    lat = _nt(w_ref[off[0]:off[3], :], u)
    cqn = rms_cols(lat[off[0]:off[1]], qnorm_ref[...]).astype(BF16)
    ckvn = rms_cols(lat[off[1]:off[2]], kvnorm_ref[...]).astype(BF16)
    krt = lat[off[2]:off[3]]
    ss_kr = jnp.sum(krt * krt, axis=0, keepdims=True)
    q_all = _nn(wuq_ref[...], cqn)
    kv_all = _nn(wukv_ref[...], ckvn)
    for g0, w0 in ((0, off[3]), (MLA_WIDTH, off[7])):
        for c in range(MLA_WIDTH // 256):
            gt_ref[0, g0 + c * 256:g0 + (c + 1) * 256, :] = _nt(
                w_ref[w0 + c * 256:w0 + (c + 1) * 256, :], u).astype(BF16)
    qg = qg_ref[...]
    kg = kg_ref[...]
    row32 = lax.broadcasted_iota(jnp.int32, (QK_PAD - MLA_QK, TM), 0)
    q_pad = jnp.where(row32 == 0, 1.0, 0.0)
    k_pad = jnp.where(row32 == 0, -bound_ref[...], 0.0)
    ones_row =jnp.where(lax.broadcasted_iota(jnp.int32, (V_PAD - MLA_V, TM), 0) == 0, 1.0, 0.0).astype(BF16)
    for hd in range(MLA_HEADS):
        qh = q_all[hd * MLA_QK:(hd + 1) * MLA_QK]
        qn = qh * lax.rsqrt(jnp.mean(qh * qh, axis=0, keepdims=True) + EPS) * qg
        qt_ref[0, hd, 0:MLA_NOPE, :] = qn[0:MLA_NOPE].astype(BF16)
        qt_ref[0, hd, MLA_NOPE:MLA_QK, :] = _rope_t(qn[MLA_NOPE:MLA_QK], cos, sin).astype(BF16)
        qt_ref[0, hd, MLA_QK:QK_PAD, :] = q_pad.astype(BF16)
        kv = kv_all[hd * 128:(hd + 1) * 128]
        kn = kv[0:MLA_NOPE]
        rk = lax.rsqrt((jnp.sum(kn * kn, axis=0, keepdims=True) + ss_kr) * (1.0 / MLA_QK) + EPS)
        kfull = jnp.concatenate([kn * rk * kg[0:MLA_NOPE],
                                 _rope_t(krt * rk * kg[MLA_NOPE:MLA_QK], cos, sin),
                                 k_pad], axis=0)
        _store_kv_blocks(k_ref, vt_ref, hd, kfull, kv[MLA_NOPE:MLA_NOPE + MLA_V], ones_row)

    cqg = cqg_ref[...]
    ckg = ckg_ref[...]
    zeros64 = jnp.zeros((CHK_DIM, TM), F32)
    for c in range(CHK_WIDTH // 256):
        rows = slice(c * 256, (c + 1) * 256)
        qc = _nt(w_ref[off[4] + c * 256:off[4] + (c + 1) * 256, :], u)
        kc = _nt(w_ref[off[5] + c * 256:off[5] + (c + 1) * 256, :], u)
        vc = _nt(w_ref[off[6] + c * 256:off[6] + (c + 1) * 256, :], u).astype(BF16)
        for j in range(4):
            vbt_ref[0, c * 4 + j, 0:CHK_DIM, :] = vc[j * CHK_DIM:(j + 1) * CHK_DIM]
            vbt_ref[0, c * 4 + j, CHK_DIM:V_PAD, :] = ones_row
        kns = []
        for j in range(4):
            hd = c * 4 + j
            hs = slice(j * CHK_DIM, (j + 1) * CHK_DIM)
            qh = qc[hs]
            qn = (qh * lax.rsqrt(jnp.mean(qh * qh, axis=0, keepdims=True) + EPS) * cqg).astype(BF16)
            lo, hi = (0, CHK_DIM) if hd % 2 == 0 else (CHK_DIM, 2 * CHK_DIM)
            qbt_ref[0, hd, lo:hi, :] = qn
            qbt_ref[0, hd, CHK_DIM - lo:2 * CHK_DIM - lo, :] = zeros64.astype(BF16)
            kh = kc[hs]
            kns.append(kh * lax.rsqrt(jnp.mean(kh * kh, axis=0, keepdims=True) + EPS) * ckg)
        for pr in range(2):
            pair = jnp.concatenate([kns[2 * pr], kns[2 * pr + 1]], axis=0)
            kb_ref[0, c * 2 + pr] = pair.T.astype(BF16)


def _ab_proj(h, ng, w_t, qnorm, wuq, kvnorm, wukv, qg, kg, cqg, ckg, cos, sin, bound):
    B, S, _ = h.shape
    nt = S // TM
    full = lambda shape: pl.BlockSpec(shape, lambda b, t: (0,) * len(shape))
    return pl.pallas_call(
        _ab_proj_kernel,
        grid=(B, nt),
        in_specs=[
            pl.BlockSpec((1, TM, D_MODEL), lambda b, t: (b, t, 0)),
            full((1, D_MODEL)),
            full(w_t.shape),
            full((MLA_Q_LORA, 1)), full((MLA_HEADS * MLA_QK, MLA_Q_LORA)),
            full((MLA_KV_LORA, 1)), full((MLA_HEADS * 128, MLA_KV_LORA)),
            full((MLA_QK, 1)), full((MLA_QK, 1)), full((CHK_DIM, 1)), full((CHK_DIM, 1)),
            pl.BlockSpec((1, MLA_ROPE // 2, TM), lambda b, t: (b, 0, t)),
            pl.BlockSpec((1, MLA_ROPE // 2, TM), lambda b, t: (b, 0, t)),
            full((1, 1)),
        ],
        out_specs=[
            pl.BlockSpec((1, MLA_HEADS, QK_PAD, TM), lambda b, t: (b, 0, 0, t)),
            pl.BlockSpec((1, MLA_HEADS, TM // TK, TK, QK_PAD), lambda b, t: (b, 0, t, 0, 0)),
            pl.BlockSpec((1, MLA_HEADS, TM // TK, V_PAD, TK), lambda b, t: (b, 0, t, 0, 0)),
            pl.BlockSpec((1, MLA_WIDTH + CHK_WIDTH, TM), lambda b, t: (b, 0, t)),
            pl.BlockSpec((1, CHK_HEADS, 2 * CHK_DIM, TM), lambda b, t: (b, 0, 0, t)),
            pl.BlockSpec((1, CHK_HEADS // 2, TM, 2 * CHK_DIM), lambda b, t: (b, 0, t, 0)),
            pl.BlockSpec((1, CHK_HEADS, V_PAD, TM), lambda b, t: (b, 0, 0, t)),
        ],
        out_shape=[
            jax.ShapeDtypeStruct((B, MLA_HEADS, QK_PAD, S), BF16),
            jax.ShapeDtypeStruct((B, MLA_HEADS, S // TK, TK, QK_PAD), BF16),
            jax.ShapeDtypeStruct((B, MLA_HEADS, S // TK, V_PAD, TK), BF16),
            jax.ShapeDtypeStruct((B, MLA_WIDTH + CHK_WIDTH, S), BF16),
            jax.ShapeDtypeStruct((B, CHK_HEADS, 2 * CHK_DIM, S), BF16),
            jax.ShapeDtypeStruct((B, CHK_HEADS // 2, S, 2 * CHK_DIM), BF16),
            jax.ShapeDtypeStruct((B, CHK_HEADS, V_PAD, S), BF16),
        ],
        compiler_params=pltpu.CompilerParams(
            dimension_semantics=("parallel", "parallel"), vmem_limit_bytes=VMEM_LIMIT),
        name="ab_proj",
    )(h, ng, w_t, qnorm, wuq, kvnorm, wukv, qg, kg, cqg, ckg, cos, sin, bound)


def _causal_attn_kernel(qt_ref, k_ref, vt_ref, o_ref, *, gran_shift):
    lax.fori_loop(0, qt_ref.shape[1], lambda hh, c: _causal_attn_head(
        hh, qt_ref, k_ref, vt_ref, o_ref, gran_shift=gran_shift), 0)


def _causal_attn_head(hh, qt_ref, k_ref, vt_ref, o_ref, *, gran_shift):
    qi = pl.program_id(2)
    dv = o_ref.shape[1] // qt_ref.shape[1]
    n_sub = TQ // TK
    qts = [qt_ref[0, hh, :, c * TK:(c + 1) * TK] for c in range(n_sub)]

    def scores(c, j, masked):
        s = _nn(k_ref[0, hh, j], qts[c])
        if masked:
            kpos = lax.broadcasted_iota(jnp.int32, (TK, TK), 0)
            qpos = lax.broadcasted_iota(jnp.int32, (TK, TK), 1)
            s = jnp.where((kpos >> gran_shift) <= (qpos >> gran_shift), s, NEG_INF)
        return s

    def update(s, j, carry):
        m, acc = carry
        m_new = jnp.maximum(m, jnp.max(s, axis=0, keepdims=True))
        p = jnp.exp2(s - m_new)
        acc = jnp.exp2(m - m_new) * acc + _nn(vt_ref[0, hh, j], p.astype(BF16))
        return m_new, acc

    def run(work, carries):
        carries = list(carries)
        pending = [scores(*w) for w in work[:LOOKAHEAD]]
        for i, (c, j, _) in enumerate(work):
            if i + LOOKAHEAD < len(work):
                pending.append(scores(*work[i + LOOKAHEAD]))
            carries[c] = update(pending.pop(0), j, carries[c])
        return tuple(carries)

    def steps(jj, carries):
        return run([(c, jj * n_sub + u, False) for u in range(n_sub) for c in range(n_sub)], carries)

    init = (jnp.full((1, TK), NEG_INF, F32), jnp.zeros((V_PAD, TK), F32))
    carries = lax.fori_loop(0, qi, steps, (init,) * n_sub)
    carries = run([(c, qi * n_sub + u, u == c) for c in range(n_sub) for u in range(c + 1)], carries)
    for c in range(n_sub):
        acc = carries[c][1]
        o_ref[0, pl.ds(hh * dv, dv), c * TK:(c + 1) * TK] = (acc[0:dv] / acc[dv:dv + 1]).astype(o_ref.dtype)
    return 0


def _shifted_attn_kernel(first_ref, reach_ref, qt_ref, k_ref, vt_ref, o_ref, *, gran_shift, long_steps):
    lax.fori_loop(0, qt_ref.shape[1], lambda hh, c: _shifted_attn_head(
        hh, first_ref, reach_ref, qt_ref, k_ref, vt_ref, o_ref, gran_shift=gran_shift,
        long_steps=long_steps), 0)


def _shifted_attn_head(hh, first_ref, reach_ref, qt_ref, k_ref, vt_ref, o_ref, *, gran_shift,
                       long_steps):
    qi = pl.program_id(2)
    hpb = qt_ref.shape[1]
    head = pl.program_id(1) * hpb + hh
    step = (pl.program_id(0) * pl.num_programs(1) * hpb + head) * pl.num_programs(2) + qi
    g0 = first_ref[step]
    reach = reach_ref[step]
    dv = o_ref.shape[1] // hpb
    tq = qt_ref.shape[3]
    nb = tq // TK
    qt = qt_ref[0, hh]

    def probs(j, lo, hi, masked):
        s = _nn(k_ref[0, hh, j], qt[:, lo:hi])
        if masked:
            kpos = lax.broadcasted_iota(jnp.int32, s.shape, 0)
            qpos = lax.broadcasted_iota(jnp.int32, s.shape, 1)
            s = jnp.where((kpos >> gran_shift) <= (qpos >> gran_shift), s, NEG_INF)
        return jnp.exp2(s).astype(BF16)

    def run(work, acc):
        pending = [probs(*w) for w in work[:LOOKAHEAD]]
        for i, (j, lo, hi, _) in enumerate(work):
            if i + LOOKAHEAD < len(work):
                pending.append(probs(*work[i + LOOKAHEAD]))
            pv = _nn(vt_ref[0, hh, j], pending.pop(0))
            parts = [acc[:, lo:hi] + pv]
            if lo > 0:
                parts.insert(0, acc[:, :lo])
            if hi < tq:
                parts.append(acc[:, hi:])
            acc = parts[0] if len(parts) == 1 else jnp.concatenate(parts, axis=1)
        return acc

    def full_groups(n_grp, base):
        return lambda jj, a: run([((base + jj * n_grp) * nb + u, 0, tq, False) for u in range(n_grp * nb)], a)

    n_loop = jnp.maximum(qi - 1 - g0, 0)
    acc = jnp.zeros((V_PAD, tq), F32)
    done = 0
    for n_grp in long_steps:
        n_it = (n_loop - done) // n_grp
        acc = lax.fori_loop(0, n_it, full_groups(n_grp, g0 + done), acc)
        done = done + n_it * n_grp
    diag = [(qi * nb + u, u * TK, tq, True) for u in range(nb)]

    def tail(max_reach):
        prev = [((qi - 1) * nb + u, 0, min(max_reach - (nb - 1 - u) + 1, nb) * TK, False)
                for u in range(nb) if max_reach - (nb - 1 - u) >= 0]
        return lambda a: run(prev + diag, a)

    levels = (nb - 2, nb - 1, 2 * nb - 2)
    level = jnp.where(reach <= levels[0], 0, jnp.where(reach <= levels[1], 1, 2))
    acc = lax.switch(jnp.where(qi > 0, level, len(levels)),
                     [tail(r) for r in levels] + [lambda a: run(diag, a)], acc)
    o_ref[0, pl.ds(hh * dv, dv), :] = (acc[0:dv] / acc[dv:dv + 1]).astype(o_ref.dtype)
    return 0


def _causal_attn(qt, k, vt, gran, bound, first_group, reach, tqf, long_steps):
    B, H, _, S = qt.shape
    nk, dv, hpb = k.shape[2], FOX_DIM, HEADS_PER_STEP
    assert TQ % TK == 0 and tqf % TK == 0 and MLA_V == FOX_DIM and vt.shape[3] == V_PAD and H % hpb == 0

    def call(body, tq, name, prefetch, **static):
        grid_spec = pltpu.PrefetchScalarGridSpec(
            num_scalar_prefetch=len(prefetch),
            grid=(B, H // hpb, S // tq),
            in_specs=[
                pl.BlockSpec((1, hpb, QK_PAD, tq), lambda b, h, q, *_: (b, h, 0, q)),
                pl.BlockSpec((1, hpb, nk, TK, QK_PAD), lambda b, h, q, *_: (b, h, 0, 0, 0)),
                pl.BlockSpec((1, hpb, nk, V_PAD, TK), lambda b, h, q, *_: (b, h, 0, 0, 0)),
            ],
            out_specs=pl.BlockSpec((1, hpb * dv, tq), lambda b, h, q, *_: (b, h, q)),
        )
        return pl.pallas_call(
            functools.partial(body, gran_shift=int(np.log2(gran)), **static),
            grid_spec=grid_spec,
            out_shape=jax.ShapeDtypeStruct((B, H * dv, S), BF16),
            compiler_params=pltpu.CompilerParams(
                dimension_semantics=("parallel", "parallel", "arbitrary"), vmem_limit_bytes=VMEM_LIMIT),
            name=name,
        )(*prefetch, qt, k, vt)

    return lax.cond(bound <= MAX_SHIFT_BOUND,
                    lambda: call(_shifted_attn_kernel, tqf, "shifted_attn",
                                 (first_group.reshape(-1), reach.reshape(-1)),
                                 long_steps=long_steps),
                    lambda: call(_causal_attn_kernel, TQ, "causal_attn", ()))


def _chunk_attn_kernel(qt_ref, k0_ref, k1_ref, k2_ref, v0_ref, v1_ref, v2_ref, bias_ref, o_ref, *,
                       shifted):
    t = pl.program_id(1)
    k_refs = (k0_ref, k1_ref, k2_ref)
    v_refs = (v0_ref, v1_ref, v2_ref)

    def scores(hd):
        qt = qt_ref[0, hd]
        ss = []
        for d in range(3):
            s = _nn(k_refs[d][0, hd // 2], qt) + bias_ref[hd, d * TC:(d + 1) * TC, :]
            if d < 2:
                s = jnp.where(t + (d - 2) >= 0, s, NEG_INF)
            ss.append(s)
        return ss

    def finish(hd, ss):
        if not shifted:
            m = jnp.max(jnp.maximum(jnp.maximum(ss[0], ss[1]), ss[2]), axis=0, keepdims=True)
            ss = [s - m for s in ss]
        p = jnp.concatenate([jnp.exp2(s).astype(BF16) for s in ss], axis=0)
        vt = jnp.concatenate([r[0, hd] for r in v_refs], axis=1)
        acc = _nn(vt, p)
        o_ref[0, hd * CHK_DIM:(hd + 1) * CHK_DIM, :] = (
            acc[0:CHK_DIM] / acc[CHK_DIM:CHK_DIM + 1]).astype(o_ref.dtype)

    pending = [scores(0)]
    for hd in range(CHK_HEADS):
        if hd + 1 < CHK_HEADS:
            pending.append(scores(hd + 1))
        finish(hd, pending.pop(0))


def _chunk_attn(qbt, kb, vbt, bias_t, spread):
    B, _, _, S = qbt.shape
    kspec = lambda d: pl.BlockSpec((1, CHK_HEADS // 2, TC, 2 * CHK_DIM),
                                   lambda b, t: (b, 0, jnp.maximum(t + (d - 2), 0), 0))
    vspec = lambda d: pl.BlockSpec((1, CHK_HEADS, V_PAD, TC),
                                   lambda b, t: (b, 0, 0, jnp.maximum(t + (d - 2), 0)))

    def call(shifted):
        return pl.pallas_call(
            functools.partial(_chunk_attn_kernel, shifted=shifted),
            grid=(B, S // TC),
            in_specs=[
                pl.BlockSpec((1, CHK_HEADS, 2 * CHK_DIM, TC), lambda b, t: (b, 0, 0, t)),
                kspec(0), kspec(1), kspec(2), vspec(0), vspec(1), vspec(2),
                pl.BlockSpec((CHK_HEADS, 3 * TC, TC), lambda b, t: (0, 0, 0)),
            ],
            out_specs=pl.BlockSpec((1, CHK_WIDTH, TC), lambda b, t: (b, 0, t)),
            out_shape=jax.ShapeDtypeStruct((B, CHK_WIDTH, S), BF16),
            compiler_params=pltpu.CompilerParams(
                dimension_semantics=("parallel", "parallel"), vmem_limit_bytes=VMEM_LIMIT),
            name="chunk_attn_shifted" if shifted else "chunk_attn",
        )(qbt, kb, kb, kb, vbt, vbt, vbt, bias_t)

    return lax.cond(spread <= 2 * MAX_SHIFT_BOUND, lambda: call(True), lambda: call(False))


def _out_ple_kernel(*refs, n_o):
    o_refs = refs[:n_o]
    gt_ref, h_ref, p_ref, wo_ref, pgn_ref, wgate_ref, pew_ref, out_ref = refs[n_o:]
    n_sub = TM // OUT_SUB
    subs = [slice(i * OUT_SUB, (i + 1) * OUT_SUB) for i in range(n_sub)]

    def gated(ts):
        ot = jnp.concatenate([r[0, :, ts] for r in o_refs], axis=0) if n_o > 1 else o_refs[0][0, :, ts]
        g = gt_ref[0, :, ts].astype(F32)
        return (ot.astype(F32) * (g * _sigmoid(g))).astype(BF16)

    mixed_t = [_nn(wo_ref[...], gated(ts)) for ts in subs]
    h1 = [h_ref[0, ts, :] + m.T for ts, m in zip(subs, mixed_t)]
    gate = [_nn(_rms_rows(x, pgn_ref[...]).astype(BF16), wgate_ref[...]) for x in h1]
    pe = [_nn(p_ref[0, 0, ts, :].astype(BF16), pew_ref[...]) for ts in subs]
    for ts, x, gt, e in zip(subs, h1, gate, pe):
        out_ref[0, ts, :] = x + e * _sigmoid(gt)


def _out_ple(o_list, gt, h, p, layer, wo_t, pgn, wgate, pew):
    B, S, _ = h.shape
    full = lambda shape: pl.BlockSpec(shape, lambda b, t: (0,) * len(shape))
    width = gt.shape[1]
    return pl.pallas_call(
        functools.partial(_out_ple_kernel, n_o=len(o_list)),
        grid=(B, S // TM),
        in_specs=[pl.BlockSpec((1, o.shape[1], TM), lambda b, t: (b, 0, t)) for o in o_list] + [
            pl.BlockSpec((1, width, TM), lambda b, t: (b, 0, t)),
            pl.BlockSpec((1, TM, D_MODEL), lambda b, t: (b, t, 0)),
            pl.BlockSpec((1, 1, TM, PLE_DIM), lambda b, t: (layer, b, t, 0)),
            full((D_MODEL, width)), full((1, D_MODEL)), full((D_MODEL, D_MODEL)),
            full((PLE_DIM, D_MODEL)),
        ],
        out_specs=pl.BlockSpec((1, TM, D_MODEL), lambda b, t: (b, t, 0)),
        out_shape=jax.ShapeDtypeStruct((B, S, D_MODEL), F32),
        compiler_params=pltpu.CompilerParams(
            dimension_semantics=("parallel", "parallel"), vmem_limit_bytes=VMEM_LIMIT),
        name="out_ple",
    )(*o_list, gt, h, p, wo_t, pgn, wgate, pew)


def _col(v):
    return v.astype(F32).reshape(-1, 1)


def _score_bound(qg, kg, d):
    return 1.02 * d * jnp.max(jnp.abs(qg)) * jnp.max(jnp.abs(kg)) + 1.0


def _first_live_group(cum2, tqf):
    nb = tqf // TK
    first_q = cum2[:, :, ::tqf]
    last_k = cum2[:, :, TK - 1::TK]
    dead = last_k[:, :, None, :] > first_q[:, :, :, None] + UNDERFLOW_LOG2
    return (jnp.sum(dead, axis=-1) // nb).astype(jnp.int32)


def _prev_reach(cum2, tqf):
    B, H, S = cum2.shape
    nb = tqf // TK
    first_q = cum2[:, :, ::TK].reshape(B, H, S // tqf, nb)
    last_k = jnp.roll(cum2[:, :, TK - 1::TK].reshape(B, H, S // tqf, nb), 1, axis=2)
    live = first_q[:, :, :, None, :] - last_k[:, :, :, :, None] >= -UNDERFLOW_LOG2
    gap = np.arange(nb)[None, :] + (nb - 1 - np.arange(nb))[:, None]
    return jnp.max(jnp.where(live, gap, -1), axis=(3, 4)).astype(jnp.int32)


def _chunk_bias_table(rel_bias, bound_qk):
    n = 4 * TC
    i = np.arange(n)
    dist = np.where(i < TC, i, i - n) + 2 * TC
    rb = rel_bias.astype(F32) * LOG2E
    spread = 2.0 * bound_qk + (jnp.max(rb) - jnp.min(rb))
    e = rb[:, np.clip(dist, -REL_CLIP, REL_CLIP) + REL_CLIP] - (bound_qk + jnp.max(rb))
    h = rel_bias.shape[0]
    return _bias_table(e.reshape(h, 1, n)), spread


def _bias_table(e):
    h, _, n = e.shape
    return pl.pallas_call(
        _bias_table_kernel,
        grid=(h,),
        in_specs=[pl.BlockSpec((1, 1, n), lambda hd: (hd, 0, 0))],
        out_specs=pl.BlockSpec((1, 3 * TC, TC), lambda hd: (hd, 0, 0)),
        out_shape=jax.ShapeDtypeStruct((h, 3 * TC, TC), F32),
        name="bias_table",
    )(e)


def _bias_table_kernel(e_ref, o_ref):
    n = e_ref.shape[2]
    rows = jnp.broadcast_to(e_ref[0], (3 * TC, n))
    skew = pltpu.roll(rows, 0, 1, stride=1, stride_axis=0)
    kchunk = lax.broadcasted_iota(jnp.int32, (3 * TC, TC), 0) // CHUNK
    qchunk = lax.broadcasted_iota(jnp.int32, (3 * TC, TC), 1) // CHUNK + (2 * TC) // CHUNK
    valid = (kchunk <= qchunk) & (kchunk >= qchunk - LEFT_CHUNKS)
    o_ref[0] = jnp.where(valid, skew[:, 0:TC], NEG_INF)


def kernel(x, p, positions, norm_g, ab_w_in, mla_q_norm, mla_w_uq, mla_kv_norm, mla_w_ukv, mla_q_gain, mla_k_gain, chk_q_gain, chk_k_gain, chk_rel_bias, ab_w_out, fox_w_in, fox_b_f, fox_q_gain, fox_k_gain, fox_w_out, pe_w, pe_gate_norm, pe_gate_w):
    B, S, _ = x.shape
    half = MLA_ROPE // 2
    inv_freq = 1.0 / (ROPE_THETA ** (jnp.arange(half, dtype=F32) / half))
    ang = positions.astype(F32)[:, None, :] * inv_freq[None, :, None]
    cos, sin = jnp.cos(ang), jnp.sin(ang)
    tri = (np.arange(TM)[:, None] <= np.arange(TM)[None, :]).astype(np.float32)
    tri = jnp.asarray(tri, BF16)

    h = x
    for i in range(DEPTH):
        l = i // 2
        ng = norm_g[i].astype(F32).reshape(1, -1)
        if i % 2 == 0:
            qg, kg = _col(mla_q_gain[l]) * (MLA_QK ** -0.5 * LOG2E), _col(mla_k_gain[l])
            bound = _score_bound(qg, kg, MLA_QK)
            cqg, ckg = _col(chk_q_gain[l]) * (CHK_DIM ** -0.5 * LOG2E), _col(chk_k_gain[l])
            qt, k, vt, gt, qbt, kb, vbt = _ab_proj(
                h, ng, ab_w_in[l].T.astype(BF16),
                _col(mla_q_norm[l]), mla_w_uq[l].T.astype(BF16),
                _col(mla_kv_norm[l]), mla_w_ukv[l].T.astype(BF16),
                qg, kg, cqg, ckg, cos, sin, bound.reshape(1, 1))
            dense = jnp.zeros((B, MLA_HEADS, S // TQ_DENSE), jnp.int32)
            o_a = _causal_attn(qt, k, vt, CHUNK, bound, dense, dense + 2 * (TQ_DENSE // TK), TQ_DENSE, (4, 2, 1))
            bias_t, spread = _chunk_bias_table(chk_rel_bias[l], _score_bound(cqg, ckg, CHK_DIM))
            o_b = _chunk_attn(qbt, kb, vbt, bias_t, spread)
            o_list, wo = [o_a, o_b], ab_w_out[l]
        else:
            qg, kg = _col(fox_q_gain[l]) * (FOX_DIM ** -0.5 * LOG2E), _col(fox_k_gain[l])
            bound = _score_bound(qg, kg, FOX_DIM)
            qa, ka, vt, gt, cum2 = _fox_proj(h, ng, fox_w_in[l].T.astype(BF16), _col(fox_b_f[l]), qg, kg,
                                             tri, bound.reshape(1, 1))
            o_c = _causal_attn(qa, ka, vt, 1, bound, _first_live_group(cum2, TQ_DECAY),
                               _prev_reach(cum2, TQ_DECAY), TQ_DECAY, (2, 1))
            o_list, wo = [o_c], fox_w_out[l]
        h = _out_ple(o_list, gt, h, p, i, wo.T.astype(BF16),
                     pe_gate_norm[i].astype(F32).reshape(1, -1),
                     pe_gate_w[i].astype(BF16), pe_w[i].astype(BF16))
    return h
```

```python
import functools

import numpy as np
import jax
import jax.numpy as jnp
from jax import lax
from jax.experimental import pallas as pl
from jax.experimental.pallas import tpu as pltpu

F32 = jnp.float32
BF16 = jnp.bfloat16

D_MODEL = 1024
DEPTH = 4
CHUNK = 64
PLE_DIM = 256
EPS = 1e-6
NEG_INF = -1e30
MLA_HEADS = 8
MLA_NOPE = 64
MLA_ROPE = 32
MLA_QK = MLA_NOPE + MLA_ROPE
MLA_V = 64
MLA_Q_LORA = 384
MLA_KV_LORA = 256
ROPE_THETA = 10000.0
CHK_HEADS = 8
CHK_DIM = 64
LEFT_CHUNKS = 8
REL_CLIP = 256
FOX_HEADS = 16
FOX_DIM = 64
MLA_WIDTH = MLA_HEADS * MLA_V
CHK_WIDTH = CHK_HEADS * CHK_DIM
FOX_WIDTH = FOX_HEADS * FOX_DIM

TM = 512
TQ = 512
TQ_DENSE = 1024
TQ_DECAY = 1024
UNDERFLOW_LOG2 = 160.0
MAX_SHIFT_BOUND = 45.0
TK = 256
TC = 256
OUT_SUB = 256
PROJ_ROWS = 256
QK_PAD = 128
V_PAD = 80
HEADS_PER_STEP = 4
LOOKAHEAD = 2
LOG2E = 1.4426950408889634
VMEM_LIMIT = 56 * 1024 * 1024


def _nt(a, b):
    return lax.dot_general(a, b, (((1,), (1,)), ((), ())), preferred_element_type=F32)


def _nn(a, b):
    return jnp.dot(a, b, preferred_element_type=F32)


def _sigmoid(x):
    return 1.0 / (1.0 + jnp.exp(-x))


def _rms_rows(x, g):
    ms = jnp.mean(x * x, axis=-1, keepdims=True)
    return x * lax.rsqrt(ms + EPS) * g


def _split3(x):
    hi = x.astype(BF16)
    r1 = x - hi.astype(F32)
    mid = r1.astype(BF16)
    lo = (r1 - mid.astype(F32)).astype(BF16)
    return hi, mid, lo


def _store_kv_blocks(k_ref, vt_ref, hd, k_t, v_t, ones_row):
    k_rows = k_t.T.astype(BF16)
    v_aug = jnp.concatenate([v_t.astype(BF16), ones_row], axis=0)
    for c in range(TM // TK):
        k_ref[0, hd, c] = k_rows[c * TK:(c + 1) * TK]
        vt_ref[0, hd, c] = v_aug[:, c * TK:(c + 1) * TK]


def _rope_t(x, cos, sin):
    half = x.shape[0] // 2
    x1, x2 = x[:half], x[half:]
    return jnp.concatenate([x1 * cos - x2 * sin, x2 * cos + x1 * sin], axis=0)


def _fox_proj_kernel(h_ref, ng_ref, w_ref, bf_ref, qg_ref, kg_ref,
                     tri_ref, bound_ref, qa_ref, ka_ref, vt_ref, gt_ref, cum_ref, carry_ref):
    t = pl.program_id(1)
    u = _rms_rows(h_ref[0], ng_ref[...]).astype(BF16)

    wq0, wk0, wv0, wg0 = 0, FOX_WIDTH, 2 * FOX_WIDTH, 3 * FOX_WIDTH
    g_last = _nt(w_ref[wg0 + FOX_WIDTH - PROJ_ROWS:wg0 + FOX_WIDTH + FOX_HEADS, :], u)
    z = g_last[PROJ_ROWS:PROJ_ROWS + FOX_HEADS] + bf_ref[...]
    logf = jnp.minimum(z, 0.0) - jnp.log(1.0 + jnp.exp(-jnp.abs(z)))
    parts = jnp.concatenate(_split3(logf), axis=0)
    c3 = _nn(parts, tri_ref[...])
    local = c3[0:FOX_HEADS] + c3[FOX_HEADS:2 * FOX_HEADS] + c3[2 * FOX_HEADS:3 * FOX_HEADS]

    @pl.when(t == 0)
    def _():
        carry_ref[...] = jnp.zeros_like(carry_ref)

    cum = local + carry_ref[:, 0:1]
    carry_ref[...] = jnp.broadcast_to(cum[:, TM - 1:TM], carry_ref.shape)
    c_hi, c_mid, c_lo = [p.astype(F32) for p in _split3(cum * LOG2E)]
    cum_ref[0] = cum * LOG2E
    d_hi, d_mid, d_lo = [p.astype(F32) for p in _split3(cum * LOG2E + bound_ref[...])]

    row = lax.broadcasted_iota(jnp.int32, (16, TM), 0)
    ones_row = jnp.where(row == 0, 1.0, 0.0).astype(BF16)
    zeros48 = jnp.zeros((QK_PAD - FOX_DIM - 16, TM), F32)
    qg = qg_ref[...]
    kg = kg_ref[...]
    heads_per_chunk = PROJ_ROWS // FOX_DIM
    for c in range(FOX_WIDTH // PROJ_ROWS):
        rows = slice(c * PROJ_ROWS, (c + 1) * PROJ_ROWS)
        r0, r1 = rows.start, rows.stop
        qc = _nt(w_ref[wq0 + r0:wq0 + r1, :], u)
        kc = _nt(w_ref[wk0 + r0:wk0 + r1, :], u)
        vc = _nt(w_ref[wv0 + r0:wv0 + r1, :], u)
        gc = g_last[0:PROJ_ROWS] if r1 == FOX_WIDTH else _nt(w_ref[wg0 + r0:wg0 + r1, :], u)
        gt_ref[0, rows, :] = gc.astype(BF16)
        for j in range(heads_per_chunk):
            hd = c * heads_per_chunk + j
            hs = slice(j * FOX_DIM, (j + 1) * FOX_DIM)
            hi, mid, lo = c_hi[hd:hd + 1], c_mid[hd:hd + 1], c_lo[hd:hd + 1]
            khi, kmid, klo = d_hi[hd:hd + 1], d_mid[hd:hd + 1], d_lo[hd:hd + 1]
            qh = qc[hs]
            qn = qh * lax.rsqrt(jnp.mean(qh * qh, axis=0, keepdims=True) + EPS) * qg
            exq = jnp.where(row < 3, 1.0, jnp.where(row == 3, hi, jnp.where(row == 4, mid,
                            jnp.where(row == 5, lo, 0.0))))
            qa_ref[0, hd, 0:FOX_DIM, :] = qn.astype(BF16)
            qa_ref[0, hd, FOX_DIM:FOX_DIM + 16, :] = exq.astype(BF16)
            qa_ref[0, hd, FOX_DIM + 16:QK_PAD, :] = zeros48.astype(BF16)
            kh = kc[hs]
            kn = kh * lax.rsqrt(jnp.mean(kh * kh, axis=0, keepdims=True) + EPS) * kg
            exk = jnp.where(row == 0, -khi, jnp.where(row == 1, -kmid, jnp.where(row == 2, -klo,
                            jnp.where(row < 6, 1.0, 0.0))))
            kfull = jnp.concatenate([kn, exk, zeros48], axis=0)
            _store_kv_blocks(ka_ref, vt_ref, hd, kfull, vc[hs], ones_row)


def _fox_proj(h, ng, w_t, bfc, qg, kg, tri, bound):
    B, S, _ = h.shape
    nt = S // TM
    full = lambda shape: pl.BlockSpec(shape, lambda b, t: (0,) * len(shape))
    return pl.pallas_call(
        _fox_proj_kernel,
        grid=(B, nt),
        in_specs=[
            pl.BlockSpec((1, TM, D_MODEL), lambda b, t: (b, t, 0)),
            full((1, D_MODEL)),
            full((4 * FOX_WIDTH + FOX_HEADS, D_MODEL)),
            full((FOX_HEADS, 1)),
            full((FOX_DIM, 1)), full((FOX_DIM, 1)),
            full((TM, TM)), full((1, 1)),
        ],
        out_specs=[
            pl.BlockSpec((1, FOX_HEADS, QK_PAD, TM), lambda b, t: (b, 0, 0, t)),
            pl.BlockSpec((1, FOX_HEADS, TM // TK, TK, QK_PAD), lambda b, t: (b, 0, t, 0, 0)),
            pl.BlockSpec((1, FOX_HEADS, TM // TK, V_PAD, TK), lambda b, t: (b, 0, t, 0, 0)),
            pl.BlockSpec((1, FOX_WIDTH, TM), lambda b, t: (b, 0, t)),
            pl.BlockSpec((1, FOX_HEADS, TM), lambda b, t: (b, 0, t)),
        ],
        out_shape=[
            jax.ShapeDtypeStruct((B, FOX_HEADS, QK_PAD, S), BF16),
            jax.ShapeDtypeStruct((B, FOX_HEADS, S // TK, TK, QK_PAD), BF16),
            jax.ShapeDtypeStruct((B, FOX_HEADS, S // TK, V_PAD, TK), BF16),
            jax.ShapeDtypeStruct((B, FOX_WIDTH, S), BF16),
            jax.ShapeDtypeStruct((B, FOX_HEADS, S), F32),
        ],
        scratch_shapes=[pltpu.VMEM((FOX_HEADS, 128), F32)],
        compiler_params=pltpu.CompilerParams(
            dimension_semantics=("parallel", "arbitrary"), vmem_limit_bytes=VMEM_LIMIT),
        name="fox_proj",
    )(h, ng, w_t, bfc, qg, kg, tri, bound)


def _ab_proj_kernel(h_ref, ng_ref, w_ref, qnorm_ref, wuq_ref, kvnorm_ref, wukv_ref, qg_ref, kg_ref,
                    cqg_ref, ckg_ref,
                    cos_ref, sin_ref, bound_ref,
                    qt_ref, k_ref, vt_ref, gt_ref, qbt_ref, kb_ref, vbt_ref):
    u = _rms_rows(h_ref[0], ng_ref[...]).astype(BF16)
    cos = cos_ref[0]
    sin = sin_ref[0]

    off = np.cumsum((0, MLA_Q_LORA, MLA_KV_LORA, MLA_ROPE, MLA_WIDTH) + (CHK_WIDTH,) * 4)

    def rms_cols(x, g):
        return x * lax.rsqrt(jnp.mean(x * x, axis=0, keepdims=True) + EPS) * g

    lat = _nt(w_ref[off[0]:off[3], :], u)
    cqn = rms_cols(lat[off[0]:off[1]], qnorm_ref[...]).astype(BF16)
    ckvn = rms_cols(lat[off[1]:off[2]], kvnorm_ref[...]).astype(BF16)
    krt = lat[off[2]:off[3]]
    ss_kr = jnp.sum(krt * krt, axis=0, keepdims=True)
    q_all = _nn(wuq_ref[...], cqn)
    kv_all = _nn(wukv_ref[...], ckvn)
    for g0, w0 in ((0, off[3]), (MLA_WIDTH, off[7])):
        for c in range(MLA_WIDTH // 256):
            gt_ref[0, g0 + c * 256:g0 + (c + 1) * 256, :] = _nt(
                w_ref[w0 + c * 256:w0 + (c + 1) * 256, :], u).astype(BF16)
    qg = qg_ref[...]
    kg = kg_ref[...]
    row32 = lax.broadcasted_iota(jnp.int32, (QK_PAD - MLA_QK, TM), 0)
    q_pad = jnp.where(row32 == 0, 1.0, 0.0)
    k_pad = jnp.where(row32 == 0, -bound_ref[...], 0.0)
    ones_row =jnp.where(lax.broadcasted_iota(jnp.int32, (V_PAD - MLA_V, TM), 0) == 0, 1.0, 0.0).astype(BF16)
    for hd in range(MLA_HEADS):
        qh = q_all[hd * MLA_QK:(hd + 1) * MLA_QK]
        qn = qh * lax.rsqrt(jnp.mean(qh * qh, axis=0, keepdims=True) + EPS) * qg
        qt_ref[0, hd, 0:MLA_NOPE, :] = qn[0:MLA_NOPE].astype(BF16)
        qt_ref[0, hd, MLA_NOPE:MLA_QK, :] = _rope_t(qn[MLA_NOPE:MLA_QK], cos, sin).astype(BF16)
        qt_ref[0, hd, MLA_QK:QK_PAD, :] = q_pad.astype(BF16)
        kv = kv_all[hd * 128:(hd + 1) * 128]
        kn = kv[0:MLA_NOPE]
        rk = lax.rsqrt((jnp.sum(kn * kn, axis=0, keepdims=True) + ss_kr) * (1.0 / MLA_QK) + EPS)
        kfull = jnp.concatenate([kn * rk * kg[0:MLA_NOPE],
                                 _rope_t(krt * rk * kg[MLA_NOPE:MLA_QK], cos, sin),
                                 k_pad], axis=0)
        _store_kv_blocks(k_ref, vt_ref, hd, kfull, kv[MLA_NOPE:MLA_NOPE + MLA_V], ones_row)

    cqg = cqg_ref[...]
    ckg = ckg_ref[...]
    zeros64 = jnp.zeros((CHK_DIM, TM), F32)
    for c in range(CHK_WIDTH // 256):
        rows = slice(c * 256, (c + 1) * 256)
        qc = _nt(w_ref[off[4] + c * 256:off[4] + (c + 1) * 256, :], u)
        kc = _nt(w_ref[off[5] + c * 256:off[5] + (c + 1) * 256, :], u)
        vc = _nt(w_ref[off[6] + c * 256:off[6] + (c + 1) * 256, :], u).astype(BF16)
        for j in range(4):
            vbt_ref[0, c * 4 + j, 0:CHK_DIM, :] = vc[j * CHK_DIM:(j + 1) * CHK_DIM]
            vbt_ref[0, c * 4 + j, CHK_DIM:V_PAD, :] = ones_row
        kns = []
        for j in range(4):
            hd = c * 4 + j
            hs = slice(j * CHK_DIM, (j + 1) * CHK_DIM)
            qh = qc[hs]
            qn = (qh * lax.rsqrt(jnp.mean(qh * qh, axis=0, keepdims=True) + EPS) * cqg).astype(BF16)
            lo, hi = (0, CHK_DIM) if hd % 2 == 0 else (CHK_DIM, 2 * CHK_DIM)
            qbt_ref[0, hd, lo:hi, :] = qn
            qbt_ref[0, hd, CHK_DIM - lo:2 * CHK_DIM - lo, :] = zeros64.astype(BF16)
            kh = kc[hs]
            kns.append(kh * lax.rsqrt(jnp.mean(kh * kh, axis=0, keepdims=True) + EPS) * ckg)
        for pr in range(2):
            pair = jnp.concatenate([kns[2 * pr], kns[2 * pr + 1]], axis=0)
            kb_ref[0, c * 2 + pr] = pair.T.astype(BF16)


def _ab_proj(h, ng, w_t, qnorm, wuq, kvnorm, wukv, qg, kg, cqg, ckg, cos, sin, bound):
    B, S, _ = h.shape
    nt = S // TM
    full = lambda shape: pl.BlockSpec(shape, lambda b, t: (0,) * len(shape))
    return pl.pallas_call(
        _ab_proj_kernel,
        grid=(B, nt),
        in_specs=[
            pl.BlockSpec((1, TM, D_MODEL), lambda b, t: (b, t, 0)),
            full((1, D_MODEL)),
            full(w_t.shape),
            full((MLA_Q_LORA, 1)), full((MLA_HEADS * MLA_QK, MLA_Q_LORA)),
            full((MLA_KV_LORA, 1)), full((MLA_HEADS * 128, MLA_KV_LORA)),
            full((MLA_QK, 1)), full((MLA_QK, 1)), full((CHK_DIM, 1)), full((CHK_DIM, 1)),
            pl.BlockSpec((1, MLA_ROPE // 2, TM), lambda b, t: (b, 0, t)),
            pl.BlockSpec((1, MLA_ROPE // 2, TM), lambda b, t: (b, 0, t)),
            full((1, 1)),
        ],
        out_specs=[
            pl.BlockSpec((1, MLA_HEADS, QK_PAD, TM), lambda b, t: (b, 0, 0, t)),
            pl.BlockSpec((1, MLA_HEADS, TM // TK, TK, QK_PAD), lambda b, t: (b, 0, t, 0, 0)),
            pl.BlockSpec((1, MLA_HEADS, TM // TK, V_PAD, TK), lambda b, t: (b, 0, t, 0, 0)),
            pl.BlockSpec((1, MLA_WIDTH + CHK_WIDTH, TM), lambda b, t: (b, 0, t)),
            pl.BlockSpec((1, CHK_HEADS, 2 * CHK_DIM, TM), lambda b, t: (b, 0, 0, t)),
            pl.BlockSpec((1, CHK_HEADS // 2, TM, 2 * CHK_DIM), lambda b, t: (b, 0, t, 0)),
            pl.BlockSpec((1, CHK_HEADS, V_PAD, TM), lambda b, t: (b, 0, 0, t)),
        ],
        out_shape=[
            jax.ShapeDtypeStruct((B, MLA_HEADS, QK_PAD, S), BF16),
            jax.ShapeDtypeStruct((B, MLA_HEADS, S // TK, TK, QK_PAD), BF16),
            jax.ShapeDtypeStruct((B, MLA_HEADS, S // TK, V_PAD, TK), BF16),
            jax.ShapeDtypeStruct((B, MLA_WIDTH + CHK_WIDTH, S), BF16),
            jax.ShapeDtypeStruct((B, CHK_HEADS, 2 * CHK_DIM, S), BF16),
            jax.ShapeDtypeStruct((B, CHK_HEADS // 2, S, 2 * CHK_DIM), BF16),
            jax.ShapeDtypeStruct((B, CHK_HEADS, V_PAD, S), BF16),
        ],
        compiler_params=pltpu.CompilerParams(
            dimension_semantics=("parallel", "parallel"), vmem_limit_bytes=VMEM_LIMIT),
        name="ab_proj",
    )(h, ng, w_t, qnorm, wuq, kvnorm, wukv, qg, kg, cqg, ckg, cos, sin, bound)


def _causal_attn_kernel(qt_ref, k_ref, vt_ref, o_ref, *, gran_shift):
    lax.fori_loop(0, qt_ref.shape[1], lambda hh, c: _causal_attn_head(
        hh, qt_ref, k_ref, vt_ref, o_ref, gran_shift=gran_shift), 0)


def _causal_attn_head(hh, qt_ref, k_ref, vt_ref, o_ref, *, gran_shift):
    qi = pl.program_id(2)
    dv = o_ref.shape[1] // qt_ref.shape[1]
    n_sub = TQ // TK
    qts = [qt_ref[0, hh, :, c * TK:(c + 1) * TK] for c in range(n_sub)]

    def scores(c, j, masked):
        s = _nn(k_ref[0, hh, j], qts[c])
        if masked:
            kpos = lax.broadcasted_iota(jnp.int32, (TK, TK), 0)
            qpos = lax.broadcasted_iota(jnp.int32, (TK, TK), 1)
            s = jnp.where((kpos >> gran_shift) <= (qpos >> gran_shift), s, NEG_INF)
        return s

    def update(s, j, carry):
        m, acc = carry
        m_new = jnp.maximum(m, jnp.max(s, axis=0, keepdims=True))
        p = jnp.exp2(s - m_new)
        acc = jnp.exp2(m - m_new) * acc + _nn(vt_ref[0, hh, j], p.astype(BF16))
        return m_new, acc

    def run(work, carries):
        carries = list(carries)
        pending = [scores(*w) for w in work[:LOOKAHEAD]]
        for i, (c, j, _) in enumerate(work):
            if i + LOOKAHEAD < len(work):
                pending.append(scores(*work[i + LOOKAHEAD]))
            carries[c] = update(pending.pop(0), j, carries[c])
        return tuple(carries)

    def steps(jj, carries):
        return run([(c, jj * n_sub + u, False) for u in range(n_sub) for c in range(n_sub)], carries)

    init = (jnp.full((1, TK), NEG_INF, F32), jnp.zeros((V_PAD, TK), F32))
    carries = lax.fori_loop(0, qi, steps, (init,) * n_sub)
    carries = run([(c, qi * n_sub + u, u == c) for c in range(n_sub) for u in range(c + 1)], carries)
    for c in range(n_sub):
        acc = carries[c][1]
        o_ref[0, pl.ds(hh * dv, dv), c * TK:(c + 1) * TK] = (acc[0:dv] / acc[dv:dv + 1]).astype(o_ref.dtype)
    return 0


def _shifted_attn_kernel(first_ref, reach_ref, qt_ref, k_ref, vt_ref, o_ref, *, gran_shift, long_steps):
    lax.fori_loop(0, qt_ref.shape[1], lambda hh, c: _shifted_attn_head(
        hh, first_ref, reach_ref, qt_ref, k_ref, vt_ref, o_ref, gran_shift=gran_shift,
        long_steps=long_steps), 0)


def _shifted_attn_head(hh, first_ref, reach_ref, qt_ref, k_ref, vt_ref, o_ref, *, gran_shift,
                       long_steps):
    qi = pl.program_id(2)
    hpb = qt_ref.shape[1]
    head = pl.program_id(1) * hpb + hh
    step = (pl.program_id(0) * pl.num_programs(1) * hpb + head) * pl.num_programs(2) + qi
    g0 = first_ref[step]
    reach = reach_ref[step]
    dv = o_ref.shape[1] // hpb
    tq = qt_ref.shape[3]
    nb = tq // TK
    qt = qt_ref[0, hh]

    def probs(j, lo, hi, masked):
        s = _nn(k_ref[0, hh, j], qt[:, lo:hi])
        if masked:
            kpos = lax.broadcasted_iota(jnp.int32, s.shape, 0)
            qpos = lax.broadcasted_iota(jnp.int32, s.shape, 1)
            s = jnp.where((kpos >> gran_shift) <= (qpos >> gran_shift), s, NEG_INF)
        return jnp.exp2(s).astype(BF16)

    def run(work, acc):
        pending = [probs(*w) for w in work[:LOOKAHEAD]]
        for i, (j, lo, hi, _) in enumerate(work):
            if i + LOOKAHEAD < len(work):
                pending.append(probs(*work[i + LOOKAHEAD]))
            pv = _nn(vt_ref[0, hh, j], pending.pop(0))
            parts = [acc[:, lo:hi] + pv]
            if lo > 0:
                parts.insert(0, acc[:, :lo])
            if hi < tq:
                parts.append(acc[:, hi:])
            acc = parts[0] if len(parts) == 1 else jnp.concatenate(parts, axis=1)
        return acc

    def full_groups(n_grp, base):
        return lambda jj, a: run([((base + jj * n_grp) * nb + u, 0, tq, False) for u in range(n_grp * nb)], a)

    n_loop = jnp.maximum(qi - 1 - g0, 0)
    acc = jnp.zeros((V_PAD, tq), F32)
    done = 0
    for n_grp in long_steps:
        n_it = (n_loop - done) // n_grp
        acc = lax.fori_loop(0, n_it, full_groups(n_grp, g0 + done), acc)
        done = done + n_it * n_grp
    diag = [(qi * nb + u, u * TK, tq, True) for u in range(nb)]

    def tail(max_reach):
        prev = [((qi - 1) * nb + u, 0, min(max_reach - (nb - 1 - u) + 1, nb) * TK, False)
                for u in range(nb) if max_reach - (nb - 1 - u) >= 0]
        return lambda a: run(prev + diag, a)

    levels = (nb - 2, nb - 1, 2 * nb - 2)
    level = jnp.where(reach <= levels[0], 0, jnp.where(reach <= levels[1], 1, 2))
    acc = lax.switch(jnp.where(qi > 0, level, len(levels)),
                     [tail(r) for r in levels] + [lambda a: run(diag, a)], acc)
    o_ref[0, pl.ds(hh * dv, dv), :] = (acc[0:dv] / acc[dv:dv + 1]).astype(o_ref.dtype)
    return 0


def _causal_attn(qt, k, vt, gran, bound, first_group, reach, tqf, long_steps):
    B, H, _, S = qt.shape
    nk, dv, hpb = k.shape[2], FOX_DIM, HEADS_PER_STEP
    assert TQ % TK == 0 and tqf % TK == 0 and MLA_V == FOX_DIM and vt.shape[3] == V_PAD and H % hpb == 0

    def call(body, tq, name, prefetch, **static):
        grid_spec = pltpu.PrefetchScalarGridSpec(
            num_scalar_prefetch=len(prefetch),
            grid=(B, H // hpb, S // tq),
            in_specs=[
                pl.BlockSpec((1, hpb, QK_PAD, tq), lambda b, h, q, *_: (b, h, 0, q)),
                pl.BlockSpec((1, hpb, nk, TK, QK_PAD), lambda b, h, q, *_: (b, h, 0, 0, 0)),
                pl.BlockSpec((1, hpb, nk, V_PAD, TK), lambda b, h, q, *_: (b, h, 0, 0, 0)),
            ],
            out_specs=pl.BlockSpec((1, hpb * dv, tq), lambda b, h, q, *_: (b, h, q)),
        )
        return pl.pallas_call(
            functools.partial(body, gran_shift=int(np.log2(gran)), **static),
            grid_spec=grid_spec,
            out_shape=jax.ShapeDtypeStruct((B, H * dv, S), BF16),
            compiler_params=pltpu.CompilerParams(
                dimension_semantics=("parallel", "parallel", "arbitrary"), vmem_limit_bytes=VMEM_LIMIT),
            name=name,
        )(*prefetch, qt, k, vt)

    return lax.cond(bound <= MAX_SHIFT_BOUND,
                    lambda: call(_shifted_attn_kernel, tqf, "shifted_attn",
                                 (first_group.reshape(-1), reach.reshape(-1)),
                                 long_steps=long_steps),
                    lambda: call(_causal_attn_kernel, TQ, "causal_attn", ()))


def _chunk_attn_kernel(qt_ref, k0_ref, k1_ref, k2_ref, v0_ref, v1_ref, v2_ref, bias_ref, o_ref, *,
                       shifted):
    t = pl.program_id(1)
    k_refs = (k0_ref, k1_ref, k2_ref)
    v_refs = (v0_ref, v1_ref, v2_ref)

    def scores(hd):
        qt = qt_ref[0, hd]
        ss = []
        for d in range(3):
            s = _nn(k_refs[d][0, hd // 2], qt) + bias_ref[hd, d * TC:(d + 1) * TC, :]
            if d < 2:
                s = jnp.where(t + (d - 2) >= 0, s, NEG_INF)
            ss.append(s)
        return ss

    def finish(hd, ss):
        if not shifted:
            m = jnp.max(jnp.maximum(jnp.maximum(ss[0], ss[1]), ss[2]), axis=0, keepdims=True)
            ss = [s - m for s in ss]
        p = jnp.concatenate([jnp.exp2(s).astype(BF16) for s in ss], axis=0)
        vt = jnp.concatenate([r[0, hd] for r in v_refs], axis=1)
        acc = _nn(vt, p)
        o_ref[0, hd * CHK_DIM:(hd + 1) * CHK_DIM, :] = (
            acc[0:CHK_DIM] / acc[CHK_DIM:CHK_DIM + 1]).astype(o_ref.dtype)

    pending = [scores(0)]
    for hd in range(CHK_HEADS):
        if hd + 1 < CHK_HEADS:
            pending.append(scores(hd + 1))
        finish(hd, pending.pop(0))


def _chunk_attn(qbt, kb, vbt, bias_t, spread):
    B, _, _, S = qbt.shape
    kspec = lambda d: pl.BlockSpec((1, CHK_HEADS // 2, TC, 2 * CHK_DIM),
                                   lambda b, t: (b, 0, jnp.maximum(t + (d - 2), 0), 0))
    vspec = lambda d: pl.BlockSpec((1, CHK_HEADS, V_PAD, TC),
                                   lambda b, t: (b, 0, 0, jnp.maximum(t + (d - 2), 0)))

    def call(shifted):
        return pl.pallas_call(
            functools.partial(_chunk_attn_kernel, shifted=shifted),
            grid=(B, S // TC),
            in_specs=[
                pl.BlockSpec((1, CHK_HEADS, 2 * CHK_DIM, TC), lambda b, t: (b, 0, 0, t)),
                kspec(0), kspec(1), kspec(2), vspec(0), vspec(1), vspec(2),
                pl.BlockSpec((CHK_HEADS, 3 * TC, TC), lambda b, t: (0, 0, 0)),
            ],
            out_specs=pl.BlockSpec((1, CHK_WIDTH, TC), lambda b, t: (b, 0, t)),
            out_shape=jax.ShapeDtypeStruct((B, CHK_WIDTH, S), BF16),
            compiler_params=pltpu.CompilerParams(
                dimension_semantics=("parallel", "parallel"), vmem_limit_bytes=VMEM_LIMIT),
            name="chunk_attn_shifted" if shifted else "chunk_attn",
        )(qbt, kb, kb, kb, vbt, vbt, vbt, bias_t)

    return lax.cond(spread <= 2 * MAX_SHIFT_BOUND, lambda: call(True), lambda: call(False))


def _out_ple_kernel(*refs, n_o):
    o_refs = refs[:n_o]
    gt_ref, h_ref, p_ref, wo_ref, pgn_ref, wgate_ref, pew_ref, out_ref = refs[n_o:]
    n_sub = TM // OUT_SUB
    subs = [slice(i * OUT_SUB, (i + 1) * OUT_SUB) for i in range(n_sub)]

    def gated(ts):
        ot = jnp.concatenate([r[0, :, ts] for r in o_refs], axis=0) if n_o > 1 else o_refs[0][0, :, ts]
        g = gt_ref[0, :, ts].astype(F32)
        return (ot.astype(F32) * (g * _sigmoid(g))).astype(BF16)

    mixed_t = [_nn(wo_ref[...], gated(ts)) for ts in subs]
    h1 = [h_ref[0, ts, :] + m.T for ts, m in zip(subs, mixed_t)]
    gate = [_nn(_rms_rows(x, pgn_ref[...]).astype(BF16), wgate_ref[...]) for x in h1]
    pe = [_nn(p_ref[0, 0, ts, :].astype(BF16), pew_ref[...]) for ts in subs]
    for ts, x, gt, e in zip(subs, h1, gate, pe):
        out_ref[0, ts, :] = x + e * _sigmoid(gt)


def _out_ple(o_list, gt, h, p, layer, wo_t, pgn, wgate, pew):
    B, S, _ = h.shape
    full = lambda shape: pl.BlockSpec(shape, lambda b, t: (0,) * len(shape))
    width = gt.shape[1]
    return pl.pallas_call(
        functools.partial(_out_ple_kernel, n_o=len(o_list)),
        grid=(B, S // TM),
        in_specs=[pl.BlockSpec((1, o.shape[1], TM), lambda b, t: (b, 0, t)) for o in o_list] + [
            pl.BlockSpec((1, width, TM), lambda b, t: (b, 0, t)),
            pl.BlockSpec((1, TM, D_MODEL), lambda b, t: (b, t, 0)),
            pl.BlockSpec((1, 1, TM, PLE_DIM), lambda b, t: (layer, b, t, 0)),
            full((D_MODEL, width)), full((1, D_MODEL)), full((D_MODEL, D_MODEL)),
            full((PLE_DIM, D_MODEL)),
        ],
        out_specs=pl.BlockSpec((1, TM, D_MODEL), lambda b, t: (b, t, 0)),
        out_shape=jax.ShapeDtypeStruct((B, S, D_MODEL), F32),
        compiler_params=pltpu.CompilerParams(
            dimension_semantics=("parallel", "parallel"), vmem_limit_bytes=VMEM_LIMIT),
        name="out_ple",
    )(*o_list, gt, h, p, wo_t, pgn, wgate, pew)


def _col(v):
    return v.astype(F32).reshape(-1, 1)


def _score_bound(qg, kg, d):
    return 1.02 * d * jnp.max(jnp.abs(qg)) * jnp.max(jnp.abs(kg)) + 1.0


def _first_live_group(cum2, tqf):
    nb = tqf // TK
    first_q = cum2[:, :, ::tqf]
    last_k = cum2[:, :, TK - 1::TK]
    dead = last_k[:, :, None, :] > first_q[:, :, :, None] + UNDERFLOW_LOG2
    return (jnp.sum(dead, axis=-1) // nb).astype(jnp.int32)


def _prev_reach(cum2, tqf):
    B, H, S = cum2.shape
    nb = tqf // TK
    first_q = cum2[:, :, ::TK].reshape(B, H, S // tqf, nb)
    last_k = jnp.roll(cum2[:, :, TK - 1::TK].reshape(B, H, S // tqf, nb), 1, axis=2)
    live = first_q[:, :, :, None, :] - last_k[:, :, :, :, None] >= -UNDERFLOW_LOG2
    gap = np.arange(nb)[None, :] + (nb - 1 - np.arange(nb))[:, None]
    return jnp.max(jnp.where(live, gap, -1), axis=(3, 4)).astype(jnp.int32)


def _chunk_bias_table(rel_bias, bound_qk):
    n = 4 * TC
    i = np.arange(n)
    dist = np.where(i < TC, i, i - n) + 2 * TC
    rb = rel_bias.astype(F32) * LOG2E
    spread = 2.0 * bound_qk + (jnp.max(rb) - jnp.min(rb))
    e = rb[:, np.clip(dist, -REL_CLIP, REL_CLIP) + REL_CLIP] - (bound_qk + jnp.max(rb))
    h = rel_bias.shape[0]
    return _bias_table(e.reshape(h, 1, n)), spread


def _bias_table(e):
    h, _, n = e.shape
    return pl.pallas_call(
        _bias_table_kernel,
        grid=(h,),
        in_specs=[pl.BlockSpec((1, 1, n), lambda hd: (hd, 0, 0))],
        out_specs=pl.BlockSpec((1, 3 * TC, TC), lambda hd: (hd, 0, 0)),
        out_shape=jax.ShapeDtypeStruct((h, 3 * TC, TC), F32),
        name="bias_table",
    )(e)


def _bias_table_kernel(e_ref, o_ref):
    n = e_ref.shape[2]
    rows = jnp.broadcast_to(e_ref[0], (3 * TC, n))
    skew = pltpu.roll(rows, 0, 1, stride=1, stride_axis=0)
    kchunk = lax.broadcasted_iota(jnp.int32, (3 * TC, TC), 0) // CHUNK
    qchunk = lax.broadcasted_iota(jnp.int32, (3 * TC, TC), 1) // CHUNK + (2 * TC) // CHUNK
    valid = (kchunk <= qchunk) & (kchunk >= qchunk - LEFT_CHUNKS)
    o_ref[0] = jnp.where(valid, skew[:, 0:TC], NEG_INF)


def kernel(x, p, positions, norm_g, ab_w_in, mla_q_norm, mla_w_uq, mla_kv_norm, mla_w_ukv, mla_q_gain, mla_k_gain, chk_q_gain, chk_k_gain, chk_rel_bias, ab_w_out, fox_w_in, fox_b_f, fox_q_gain, fox_k_gain, fox_w_out, pe_w, pe_gate_norm, pe_gate_w):
    B, S, _ = x.shape
    half = MLA_ROPE // 2
    inv_freq = 1.0 / (ROPE_THETA ** (jnp.arange(half, dtype=F32) / half))
    ang = positions.astype(F32)[:, None, :] * inv_freq[None, :, None]
    cos, sin = jnp.cos(ang), jnp.sin(ang)
    tri = (np.arange(TM)[:, None] <= np.arange(TM)[None, :]).astype(np.float32)
    tri = jnp.asarray(tri, BF16)

    h = x
    for i in range(DEPTH):
        l = i // 2
        ng = norm_g[i].astype(F32).reshape(1, -1)
        if i % 2 == 0:
            qg, kg = _col(mla_q_gain[l]) * (MLA_QK ** -0.5 * LOG2E), _col(mla_k_gain[l])
            bound = _score_bound(qg, kg, MLA_QK)
            cqg, ckg = _col(chk_q_gain[l]) * (CHK_DIM ** -0.5 * LOG2E), _col(chk_k_gain[l])
            qt, k, vt, gt, qbt, kb, vbt = _ab_proj(
                h, ng, ab_w_in[l].T.astype(BF16),
                _col(mla_q_norm[l]), mla_w_uq[l].T.astype(BF16),
                _col(mla_kv_norm[l]), mla_w_ukv[l].T.astype(BF16),
                qg, kg, cqg, ckg, cos, sin, bound.reshape(1, 1))
            dense = jnp.zeros((B, MLA_HEADS, S // TQ_DENSE), jnp.int32)
            o_a = _causal_attn(qt, k, vt, CHUNK, bound, dense, dense + 2 * (TQ_DENSE // TK), TQ_DENSE, (4, 2, 1))
            bias_t, spread = _chunk_bias_table(chk_rel_bias[l], _score_bound(cqg, ckg, CHK_DIM))
            o_b = _chunk_attn(qbt, kb, vbt, bias_t, spread)
            o_list, wo = [o_a, o_b], ab_w_out[l]
        else:
            qg, kg = _col(fox_q_gain[l]) * (FOX_DIM ** -0.5 * LOG2E), _col(fox_k_gain[l])
            bound = _score_bound(qg, kg, FOX_DIM)
            qa, ka, vt, gt, cum2 = _fox_proj(h, ng, fox_w_in[l].T.astype(BF16), _col(fox_b_f[l]), qg, kg,
                                             tri, bound.reshape(1, 1))
            o_c = _causal_attn(qa, ka, vt, 1, bound, _first_live_group(cum2, TQ_DECAY),
                               _prev_reach(cum2, TQ_DECAY), TQ_DECAY, (2, 1))
            o_list, wo = [o_c], fox_w_out[l]
        h = _out_ple(o_list, gt, h, p, i, wo.T.astype(BF16),
                     pe_gate_norm[i].astype(F32).reshape(1, -1),
                     pe_gate_w[i].astype(BF16), pe_w[i].astype(BF16))
    return h
```

```python
import functools

import numpy as np
import jax
import jax.numpy as jnp
from jax import lax
from jax.experimental import pallas as pl
from jax.experimental.pallas import tpu as pltpu

F32 = jnp.float32
BF16 = jnp.bfloat16

D_MODEL = 1024
DEPTH = 4
CHUNK = 64
PLE_DIM = 256
EPS = 1e-6
NEG_INF = -1e30
MLA_HEADS = 8
MLA_NOPE = 64
MLA_ROPE = 32
MLA_QK = MLA_NOPE + MLA_ROPE
MLA_V = 64
MLA_Q_LORA = 384
MLA_KV_LORA = 256
ROPE_THETA = 10000.0
CHK_HEADS = 8
CHK_DIM = 64
LEFT_CHUNKS = 8
REL_CLIP = 256
FOX_HEADS = 16
FOX_DIM = 64
MLA_WIDTH = MLA_HEADS * MLA_V
CHK_WIDTH = CHK_HEADS * CHK_DIM
FOX_WIDTH = FOX_HEADS * FOX_DIM

TM = 512
TQ = 512
TQ_DENSE = 1024
TQ_DECAY = 1024
UNDERFLOW_LOG2 = 160.0
MAX_SHIFT_BOUND = 45.0
TK = 256
TC = 256
OUT_SUB = 256
PROJ_ROWS = 256
QK_PAD = 128
V_PAD = 80
HEADS_PER_STEP = 4
CHUNK_LOOKAHEAD = 2
LOOKAHEAD = 3
LOG2E = 1.4426950408889634
VMEM_LIMIT = 56 * 1024 * 1024


def _nt(a, b):
    return lax.dot_general(a, b, (((1,), (1,)), ((), ())), preferred_element_type=F32)


def _nn(a, b):
    return jnp.dot(a, b, preferred_element_type=F32)


def _sigmoid(x):
    return 1.0 / (1.0 + jnp.exp(-x))


def _rms_rows(x, g):
    ms = jnp.mean(x * x, axis=-1, keepdims=True)
    return x * lax.rsqrt(ms + EPS) * g


def _split3(x):
    hi = x.astype(BF16)
    r1 = x - hi.astype(F32)
    mid = r1.astype(BF16)
    lo = (r1 - mid.astype(F32)).astype(BF16)
    return hi, mid, lo


def _store_kv_blocks(k_ref, vt_ref, hd, k_t, v_t, ones_row):
    k_rows = k_t.T.astype(BF16)
    v_aug = jnp.concatenate([v_t.astype(BF16), ones_row], axis=0)
    for c in range(TM // TK):
        k_ref[0, hd, c] = k_rows[c * TK:(c + 1) * TK]
        vt_ref[0, hd, c] = v_aug[:, c * TK:(c + 1) * TK]


def _rope_t(x, cos, sin):
    half = x.shape[0] // 2
    x1, x2 = x[:half], x[half:]
    return jnp.concatenate([x1 * cos - x2 * sin, x2 * cos + x1 * sin], axis=0)


def _fox_proj_kernel(h_ref, ng_ref, w_ref, bf_ref, qg_ref, kg_ref,
                     tri_ref, bound_ref, qa_ref, ka_ref, vt_ref, gt_ref, cum_ref, carry_ref):
    t = pl.program_id(1)
    u = _rms_rows(h_ref[0], ng_ref[...]).astype(BF16)

    wq0, wk0, wv0, wg0 = 0, FOX_WIDTH, 2 * FOX_WIDTH, 3 * FOX_WIDTH
    g_last = _nt(w_ref[wg0 + FOX_WIDTH - PROJ_ROWS:wg0 + FOX_WIDTH + FOX_HEADS, :], u)
    z = g_last[PROJ_ROWS:PROJ_ROWS + FOX_HEADS] + bf_ref[...]
    logf = jnp.minimum(z, 0.0) - jnp.log(1.0 + jnp.exp(-jnp.abs(z)))
    parts = jnp.concatenate(_split3(logf), axis=0)
    c3 = _nn(parts, tri_ref[...])
    local = c3[0:FOX_HEADS] + c3[FOX_HEADS:2 * FOX_HEADS] + c3[2 * FOX_HEADS:3 * FOX_HEADS]

    @pl.when(t == 0)
    def _():
        carry_ref[...] = jnp.zeros_like(carry_ref)

    cum = local + carry_ref[:, 0:1]
    carry_ref[...] = jnp.broadcast_to(cum[:, TM - 1:TM], carry_ref.shape)
    c_hi, c_mid, c_lo = [p.astype(F32) for p in _split3(cum * LOG2E)]
    cum_ref[0] = cum * LOG2E
    d_hi, d_mid, d_lo = [p.astype(F32) for p in _split3(cum * LOG2E + bound_ref[...])]

    row = lax.broadcasted_iota(jnp.int32, (16, TM), 0)
    ones_row = jnp.where(row == 0, 1.0, 0.0).astype(BF16)
    zeros48 = jnp.zeros((QK_PAD - FOX_DIM - 16, TM), F32)
    qg = qg_ref[...]
    kg = kg_ref[...]
    heads_per_chunk = PROJ_ROWS // FOX_DIM
    for c in range(FOX_WIDTH // PROJ_ROWS):
        rows = slice(c * PROJ_ROWS, (c + 1) * PROJ_ROWS)
        r0, r1 = rows.start, rows.stop
        qc = _nt(w_ref[wq0 + r0:wq0 + r1, :], u)
        kc = _nt(w_ref[wk0 + r0:wk0 + r1, :], u)
        vc = _nt(w_ref[wv0 + r0:wv0 + r1, :], u)
        gc = g_last[0:PROJ_ROWS] if r1 == FOX_WIDTH else _nt(w_ref[wg0 + r0:wg0 + r1, :], u)
        gt_ref[0, rows, :] = gc.astype(BF16)
        for j in range(heads_per_chunk):
            hd = c * heads_per_chunk + j
            hs = slice(j * FOX_DIM, (j + 1) * FOX_DIM)
            hi, mid, lo = c_hi[hd:hd + 1], c_mid[hd:hd + 1], c_lo[hd:hd + 1]
            khi, kmid, klo = d_hi[hd:hd + 1], d_mid[hd:hd + 1], d_lo[hd:hd + 1]
            qh = qc[hs]
            qn = qh * lax.rsqrt(jnp.mean(qh * qh, axis=0, keepdims=True) + EPS) * qg
            exq = jnp.where(row < 3, 1.0, jnp.where(row == 3, hi, jnp.where(row == 4, mid,
                            jnp.where(row == 5, lo, 0.0))))
            qa_ref[0, hd, 0:FOX_DIM, :] = qn.astype(BF16)
            qa_ref[0, hd, FOX_DIM:FOX_DIM + 16, :] = exq.astype(BF16)
            qa_ref[0, hd, FOX_DIM + 16:QK_PAD, :] = zeros48.astype(BF16)
            kh = kc[hs]
            kn = kh * lax.rsqrt(jnp.mean(kh * kh, axis=0, keepdims=True) + EPS) * kg
            exk = jnp.where(row == 0, -khi, jnp.where(row == 1, -kmid, jnp.where(row == 2, -klo,
                            jnp.where(row < 6, 1.0, 0.0))))
            kfull = jnp.concatenate([kn, exk, zeros48], axis=0)
            _store_kv_blocks(ka_ref, vt_ref, hd, kfull, vc[hs], ones_row)


def _fox_proj(h, ng, w_t, bfc, qg, kg, tri, bound):
    B, S, _ = h.shape
    nt = S // TM
    full = lambda shape: pl.BlockSpec(shape, lambda b, t: (0,) * len(shape))
    return pl.pallas_call(
        _fox_proj_kernel,
        grid=(B, nt),
        in_specs=[
            pl.BlockSpec((1, TM, D_MODEL), lambda b, t: (b, t, 0)),
            full((1, D_MODEL)),
            full((4 * FOX_WIDTH + FOX_HEADS, D_MODEL)),
            full((FOX_HEADS, 1)),
            full((FOX_DIM, 1)), full((FOX_DIM, 1)),
            full((TM, TM)), full((1, 1)),
        ],
        out_specs=[
            pl.BlockSpec((1, FOX_HEADS, QK_PAD, TM), lambda b, t: (b, 0, 0, t)),
            pl.BlockSpec((1, FOX_HEADS, TM // TK, TK, QK_PAD), lambda b, t: (b, 0, t, 0, 0)),
            pl.BlockSpec((1, FOX_HEADS, TM // TK, V_PAD, TK), lambda b, t: (b, 0, t, 0, 0)),
            pl.BlockSpec((1, FOX_WIDTH, TM), lambda b, t: (b, 0, t)),
            pl.BlockSpec((1, FOX_HEADS, TM), lambda b, t: (b, 0, t)),
        ],
        out_shape=[
            jax.ShapeDtypeStruct((B, FOX_HEADS, QK_PAD, S), BF16),
            jax.ShapeDtypeStruct((B, FOX_HEADS, S // TK, TK, QK_PAD), BF16),
            jax.ShapeDtypeStruct((B, FOX_HEADS, S // TK, V_PAD, TK), BF16),
            jax.ShapeDtypeStruct((B, FOX_WIDTH, S), BF16),
            jax.ShapeDtypeStruct((B, FOX_HEADS, S), F32),
        ],
        scratch_shapes=[pltpu.VMEM((FOX_HEADS, 128), F32)],
        compiler_params=pltpu.CompilerParams(
            dimension_semantics=("parallel", "arbitrary"), vmem_limit_bytes=VMEM_LIMIT),
        name="fox_proj",
    )(h, ng, w_t, bfc, qg, kg, tri, bound)


def _ab_proj_kernel(h_ref, ng_ref, w_ref, qnorm_ref, wuq_ref, kvnorm_ref, wukv_ref, qg_ref, kg_ref,
                    cqg_ref, ckg_ref,
                    cos_ref, sin_ref, bound_ref,
                    qt_ref, k_ref, vt_ref, gt_ref, qbt_ref, kb_ref, vbt_ref):
    u = _rms_rows(h_ref[0], ng_ref[...]).astype(BF16)
    cos = cos_ref[0]
    sin = sin_ref[0]

    off = np.cumsum((0, MLA_Q_LORA, MLA_KV_LORA, MLA_ROPE, MLA_WIDTH) + (CHK_WIDTH,) * 4)

    def rms_cols(x, g):
        return x * lax.rsqrt(jnp.mean(x * x, axis=0, keepdims=True) + EPS) * g

    lat = _nt(w_ref[off[0]:off[3], :], u)
    cqn = rms_cols(lat[off[0]:off[1]], qnorm_ref[...]).astype(BF16)
    ckvn = rms_cols(lat[off[1]:off[2]], kvnorm_ref[...]).astype(BF16)
    krt = lat[off[2]:off[3]]
    ss_kr = jnp.sum(krt * krt, axis=0, keepdims=True)
    q_all = _nn(wuq_ref[...], cqn)
    kv_all = _nn(wukv_ref[...], ckvn)
    for g0, w0 in ((0, off[3]), (MLA_WIDTH, off[7])):
        for c in range(MLA_WIDTH // 256):
            gt_ref[0, g0 + c * 256:g0 + (c + 1) * 256, :] = _nt(
                w_ref[w0 + c * 256:w0 + (c + 1) * 256, :], u).astype(BF16)
    qg = qg_ref[...]
    kg = kg_ref[...]
    row32 = lax.broadcasted_iota(jnp.int32, (QK_PAD - MLA_QK, TM), 0)
    q_pad = jnp.where(row32 == 0, 1.0, 0.0)
    k_pad = jnp.where(row32 == 0, -bound_ref[...], 0.0)
    ones_row =jnp.where(lax.broadcasted_iota(jnp.int32, (V_PAD - MLA_V, TM), 0) == 0, 1.0, 0.0).astype(BF16)
    for hd in range(MLA_HEADS):
        qh = q_all[hd * MLA_QK:(hd + 1) * MLA_QK]
        qn = qh * lax.rsqrt(jnp.mean(qh * qh, axis=0, keepdims=True) + EPS) * qg
        qt_ref[0, hd, 0:MLA_NOPE, :] = qn[0:MLA_NOPE].astype(BF16)
        qt_ref[0, hd, MLA_NOPE:MLA_QK, :] = _rope_t(qn[MLA_NOPE:MLA_QK], cos, sin).astype(BF16)
        qt_ref[0, hd, MLA_QK:QK_PAD, :] = q_pad.astype(BF16)
        kv = kv_all[hd * 128:(hd + 1) * 128]
        kn = kv[0:MLA_NOPE]
        rk = lax.rsqrt((jnp.sum(kn * kn, axis=0, keepdims=True) + ss_kr) * (1.0 / MLA_QK) + EPS)
        kfull = jnp.concatenate([kn * rk * kg[0:MLA_NOPE],
                                 _rope_t(krt * rk * kg[MLA_NOPE:MLA_QK], cos, sin),
                                 k_pad], axis=0)
        _store_kv_blocks(k_ref, vt_ref, hd, kfull, kv[MLA_NOPE:MLA_NOPE + MLA_V], ones_row)

    cqg = cqg_ref[...]
    ckg = ckg_ref[...]
    zeros64 = jnp.zeros((CHK_DIM, TM), F32)
    for c in range(CHK_WIDTH // 256):
        rows = slice(c * 256, (c + 1) * 256)
        qc = _nt(w_ref[off[4] + c * 256:off[4] + (c + 1) * 256, :], u)
        kc = _nt(w_ref[off[5] + c * 256:off[5] + (c + 1) * 256, :], u)
        vc = _nt(w_ref[off[6] + c * 256:off[6] + (c + 1) * 256, :], u).astype(BF16)
        for j in range(4):
            vbt_ref[0, c * 4 + j, 0:CHK_DIM, :] = vc[j * CHK_DIM:(j + 1) * CHK_DIM]
            vbt_ref[0, c * 4 + j, CHK_DIM:V_PAD, :] = ones_row
        kns = []
        for j in range(4):
            hd = c * 4 + j
            hs = slice(j * CHK_DIM, (j + 1) * CHK_DIM)
            qh = qc[hs]
            qn = (qh * lax.rsqrt(jnp.mean(qh * qh, axis=0, keepdims=True) + EPS) * cqg).astype(BF16)
            lo, hi = (0, CHK_DIM) if hd % 2 == 0 else (CHK_DIM, 2 * CHK_DIM)
            qbt_ref[0, hd, lo:hi, :] = qn
            qbt_ref[0, hd, CHK_DIM - lo:2 * CHK_DIM - lo, :] = zeros64.astype(BF16)
            kh = kc[hs]
            kns.append(kh * lax.rsqrt(jnp.mean(kh * kh, axis=0, keepdims=True) + EPS) * ckg)
        for pr in range(2):
            pair = jnp.concatenate([kns[2 * pr], kns[2 * pr + 1]], axis=0)
            kb_ref[0, c * 2 + pr] = pair.T.astype(BF16)


def _ab_proj(h, ng, w_t, qnorm, wuq, kvnorm, wukv, qg, kg, cqg, ckg, cos, sin, bound):
    B, S, _ = h.shape
    nt = S // TM
    full = lambda shape: pl.BlockSpec(shape, lambda b, t: (0,) * len(shape))
    return pl.pallas_call(
        _ab_proj_kernel,
        grid=(B, nt),
        in_specs=[
            pl.BlockSpec((1, TM, D_MODEL), lambda b, t: (b, t, 0)),
            full((1, D_MODEL)),
            full(w_t.shape),
            full((MLA_Q_LORA, 1)), full((MLA_HEADS * MLA_QK, MLA_Q_LORA)),
            full((MLA_KV_LORA, 1)), full((MLA_HEADS * 128, MLA_KV_LORA)),
            full((MLA_QK, 1)), full((MLA_QK, 1)), full((CHK_DIM, 1)), full((CHK_DIM, 1)),
            pl.BlockSpec((1, MLA_ROPE // 2, TM), lambda b, t: (b, 0, t)),
            pl.BlockSpec((1, MLA_ROPE // 2, TM), lambda b, t: (b, 0, t)),
            full((1, 1)),
        ],
        out_specs=[
            pl.BlockSpec((1, MLA_HEADS, QK_PAD, TM), lambda b, t: (b, 0, 0, t)),
            pl.BlockSpec((1, MLA_HEADS, TM // TK, TK, QK_PAD), lambda b, t: (b, 0, t, 0, 0)),
            pl.BlockSpec((1, MLA_HEADS, TM // TK, V_PAD, TK), lambda b, t: (b, 0, t, 0, 0)),
            pl.BlockSpec((1, MLA_WIDTH + CHK_WIDTH, TM), lambda b, t: (b, 0, t)),
            pl.BlockSpec((1, CHK_HEADS, 2 * CHK_DIM, TM), lambda b, t: (b, 0, 0, t)),
            pl.BlockSpec((1, CHK_HEADS // 2, TM, 2 * CHK_DIM), lambda b, t: (b, 0, t, 0)),
            pl.BlockSpec((1, CHK_HEADS, V_PAD, TM), lambda b, t: (b, 0, 0, t)),
        ],
        out_shape=[
            jax.ShapeDtypeStruct((B, MLA_HEADS, QK_PAD, S), BF16),
            jax.ShapeDtypeStruct((B, MLA_HEADS, S // TK, TK, QK_PAD), BF16),
            jax.ShapeDtypeStruct((B, MLA_HEADS, S // TK, V_PAD, TK), BF16),
            jax.ShapeDtypeStruct((B, MLA_WIDTH + CHK_WIDTH, S), BF16),
            jax.ShapeDtypeStruct((B, CHK_HEADS, 2 * CHK_DIM, S), BF16),
            jax.ShapeDtypeStruct((B, CHK_HEADS // 2, S, 2 * CHK_DIM), BF16),
            jax.ShapeDtypeStruct((B, CHK_HEADS, V_PAD, S), BF16),
        ],
        compiler_params=pltpu.CompilerParams(
            dimension_semantics=("parallel", "parallel"), vmem_limit_bytes=VMEM_LIMIT),
        name="ab_proj",
    )(h, ng, w_t, qnorm, wuq, kvnorm, wukv, qg, kg, cqg, ckg, cos, sin, bound)


def _causal_attn_kernel(qt_ref, k_ref, vt_ref, o_ref, *, gran_shift):
    lax.fori_loop(0, qt_ref.shape[1], lambda hh, c: _causal_attn_head(
        hh, qt_ref, k_ref, vt_ref, o_ref, gran_shift=gran_shift), 0)


def _causal_attn_head(hh, qt_ref, k_ref, vt_ref, o_ref, *, gran_shift):
    qi = pl.program_id(2)
    dv = o_ref.shape[1] // qt_ref.shape[1]
    n_sub = TQ // TK
    qts = [qt_ref[0, hh, :, c * TK:(c + 1) * TK] for c in range(n_sub)]

    def scores(c, j, masked):
        s = _nn(k_ref[0, hh, j], qts[c])
        if masked:
            kpos = lax.broadcasted_iota(jnp.int32, (TK, TK), 0)
            qpos = lax.broadcasted_iota(jnp.int32, (TK, TK), 1)
            s = jnp.where((kpos >> gran_shift) <= (qpos >> gran_shift), s, NEG_INF)
        return s

    def update(s, j, carry):
        m, acc = carry
        m_new = jnp.maximum(m, jnp.max(s, axis=0, keepdims=True))
        p = jnp.exp2(s - m_new)
        acc = jnp.exp2(m - m_new) * acc + _nn(vt_ref[0, hh, j], p.astype(BF16))
        return m_new, acc

    def run(work, carries):
        carries = list(carries)
        pending = [scores(*w) for w in work[:LOOKAHEAD]]
        for i, (c, j, _) in enumerate(work):
            if i + LOOKAHEAD < len(work):
                pending.append(scores(*work[i + LOOKAHEAD]))
            carries[c] = update(pending.pop(0), j, carries[c])
        return tuple(carries)

    def steps(jj, carries):
        return run([(c, jj * n_sub + u, False) for u in range(n_sub) for c in range(n_sub)], carries)

    init = (jnp.full((1, TK), NEG_INF, F32), jnp.zeros((V_PAD, TK), F32))
    carries = lax.fori_loop(0, qi, steps, (init,) * n_sub)
    carries = run([(c, qi * n_sub + u, u == c) for c in range(n_sub) for u in range(c + 1)], carries)
    for c in range(n_sub):
        acc = carries[c][1]
        o_ref[0, pl.ds(hh * dv, dv), c * TK:(c + 1) * TK] = (acc[0:dv] / acc[dv:dv + 1]).astype(o_ref.dtype)
    return 0


def _shifted_attn_kernel(first_ref, reach_ref, qt_ref, k_ref, vt_ref, o_ref, *, gran_shift, long_steps):
    lax.fori_loop(0, qt_ref.shape[1], lambda hh, c: _shifted_attn_head(
        hh, first_ref, reach_ref, qt_ref, k_ref, vt_ref, o_ref, gran_shift=gran_shift,
        long_steps=long_steps), 0)


def _shifted_attn_head(hh, first_ref, reach_ref, qt_ref, k_ref, vt_ref, o_ref, *, gran_shift,
                       long_steps):
    qi = pl.program_id(2)
    hpb = qt_ref.shape[1]
    head = pl.program_id(1) * hpb + hh
    step = (pl.program_id(0) * pl.num_programs(1) * hpb + head) * pl.num_programs(2) + qi
    g0 = first_ref[step]
    reach = reach_ref[step]
    dv = o_ref.shape[1] // hpb
    tq = qt_ref.shape[3]
    nb = tq // TK
    qt = qt_ref[0, hh]

    def probs(j, lo, hi, masked):
        s = _nn(k_ref[0, hh, j], qt[:, lo:hi])
        if masked:
            kpos = lax.broadcasted_iota(jnp.int32, s.shape, 0)
            qpos = lax.broadcasted_iota(jnp.int32, s.shape, 1)
            s = jnp.where((kpos >> gran_shift) <= (qpos >> gran_shift), s, NEG_INF)
        return jnp.exp2(s).astype(BF16)

    def run(work, acc):
        pending = [probs(*w) for w in work[:LOOKAHEAD]]
        for i, (j, lo, hi, _) in enumerate(work):
            if i + LOOKAHEAD < len(work):
                pending.append(probs(*work[i + LOOKAHEAD]))
            pv = _nn(vt_ref[0, hh, j], pending.pop(0))
            parts = [acc[:, lo:hi] + pv]
            if lo > 0:
                parts.insert(0, acc[:, :lo])
            if hi < tq:
                parts.append(acc[:, hi:])
            acc = parts[0] if len(parts) == 1 else jnp.concatenate(parts, axis=1)
        return acc

    def full_groups(n_grp, base):
        return lambda jj, a: run([((base + jj * n_grp) * nb + u, 0, tq, False) for u in range(n_grp * nb)], a)

    n_loop = jnp.maximum(qi - 1 - g0, 0)
    acc = jnp.zeros((V_PAD, tq), F32)
    done = 0
    for n_grp in long_steps:
        n_it = (n_loop - done) // n_grp
        acc = lax.fori_loop(0, n_it, full_groups(n_grp, g0 + done), acc)
        done = done + n_it * n_grp
    diag = [(qi * nb + u, u * TK, tq, True) for u in range(nb)]

    def tail(max_reach):
        prev = [((qi - 1) * nb + u, 0, min(max_reach - (nb - 1 - u) + 1, nb) * TK, False)
                for u in range(nb) if max_reach - (nb - 1 - u) >= 0]
        return lambda a: run(prev + diag, a)

    levels = (nb - 2, nb - 1, 2 * nb - 2)
    level = jnp.where(reach <= levels[0], 0, jnp.where(reach <= levels[1], 1, 2))
    acc = lax.switch(jnp.where(qi > 0, level, len(levels)),
                     [tail(r) for r in levels] + [lambda a: run(diag, a)], acc)
    o_ref[0, pl.ds(hh * dv, dv), :] = (acc[0:dv] / acc[dv:dv + 1]).astype(o_ref.dtype)
    return 0


def _causal_attn(qt, k, vt, gran, bound, first_group, reach, tqf, long_steps):
    B, H, _, S = qt.shape
    nk, dv, hpb = k.shape[2], FOX_DIM, HEADS_PER_STEP
    assert TQ % TK == 0 and tqf % TK == 0 and MLA_V == FOX_DIM and vt.shape[3] == V_PAD and H % hpb == 0

    def call(body, tq, name, prefetch, **static):
        grid_spec = pltpu.PrefetchScalarGridSpec(
            num_scalar_prefetch=len(prefetch),
            grid=(B, H // hpb, S // tq),
            in_specs=[
                pl.BlockSpec((1, hpb, QK_PAD, tq), lambda b, h, q, *_: (b, h, 0, q)),
                pl.BlockSpec((1, hpb, nk, TK, QK_PAD), lambda b, h, q, *_: (b, h, 0, 0, 0)),
                pl.BlockSpec((1, hpb, nk, V_PAD, TK), lambda b, h, q, *_: (b, h, 0, 0, 0)),
            ],
            out_specs=pl.BlockSpec((1, hpb * dv, tq), lambda b, h, q, *_: (b, h, q)),
        )
        return pl.pallas_call(
            functools.partial(body, gran_shift=int(np.log2(gran)), **static),
            grid_spec=grid_spec,
            out_shape=jax.ShapeDtypeStruct((B, H * dv, S), BF16),
            compiler_params=pltpu.CompilerParams(
                dimension_semantics=("parallel", "parallel", "arbitrary"), vmem_limit_bytes=VMEM_LIMIT),
            name=name,
        )(*prefetch, qt, k, vt)

    return lax.cond(bound <= MAX_SHIFT_BOUND,
                    lambda: call(_shifted_attn_kernel, tqf, "shifted_attn",
                                 (first_group.reshape(-1), reach.reshape(-1)),
                                 long_steps=long_steps),
                    lambda: call(_causal_attn_kernel, TQ, "causal_attn", ()))


def _chunk_attn_kernel(qt_ref, k0_ref, k1_ref, k2_ref, v0_ref, v1_ref, v2_ref, bias_ref, o_ref, *,
                       shifted):
    t = pl.program_id(1)
    k_refs = (k0_ref, k1_ref, k2_ref)
    v_refs = (v0_ref, v1_ref, v2_ref)

    def scores(hd):
        qt = qt_ref[0, hd]
        ss = []
        for d in range(3):
            s = _nn(k_refs[d][0, hd // 2], qt) + bias_ref[hd, d * TC:(d + 1) * TC, :]
            if d < 2:
                s = jnp.where(t + (d - 2) >= 0, s, NEG_INF)
            ss.append(s)
        return ss

    def finish(hd, ss):
        if not shifted:
            m = jnp.max(jnp.maximum(jnp.maximum(ss[0], ss[1]), ss[2]), axis=0, keepdims=True)
            ss = [s - m for s in ss]
        p = jnp.concatenate([jnp.exp2(s).astype(BF16) for s in ss], axis=0)
        vt = jnp.concatenate([r[0, hd] for r in v_refs], axis=1)
        acc = _nn(vt, p)
        o_ref[0, hd * CHK_DIM:(hd + 1) * CHK_DIM, :] = (
            acc[0:CHK_DIM] / acc[CHK_DIM:CHK_DIM + 1]).astype(o_ref.dtype)

    pending = [scores(hd) for hd in range(CHUNK_LOOKAHEAD)]
    for hd in range(CHK_HEADS):
        if hd + CHUNK_LOOKAHEAD < CHK_HEADS:
            pending.append(scores(hd + CHUNK_LOOKAHEAD))
        finish(hd, pending.pop(0))


def _chunk_attn(qbt, kb, vbt, bias_t, spread):
    B, _, _, S = qbt.shape
    kspec = lambda d: pl.BlockSpec((1, CHK_HEADS // 2, TC, 2 * CHK_DIM),
                                   lambda b, t: (b, 0, jnp.maximum(t + (d - 2), 0), 0))
    vspec = lambda d: pl.BlockSpec((1, CHK_HEADS, V_PAD, TC),
                                   lambda b, t: (b, 0, 0, jnp.maximum(t + (d - 2), 0)))

    def call(shifted):
        return pl.pallas_call(
            functools.partial(_chunk_attn_kernel, shifted=shifted),
            grid=(B, S // TC),
            in_specs=[
                pl.BlockSpec((1, CHK_HEADS, 2 * CHK_DIM, TC), lambda b, t: (b, 0, 0, t)),
                kspec(0), kspec(1), kspec(2), vspec(0), vspec(1), vspec(2),
                pl.BlockSpec((CHK_HEADS, 3 * TC, TC), lambda b, t: (0, 0, 0)),
            ],
            out_specs=pl.BlockSpec((1, CHK_WIDTH, TC), lambda b, t: (b, 0, t)),
            out_shape=jax.ShapeDtypeStruct((B, CHK_WIDTH, S), BF16),
            compiler_params=pltpu.CompilerParams(
                dimension_semantics=("parallel", "parallel"), vmem_limit_bytes=VMEM_LIMIT),
            name="chunk_attn_shifted" if shifted else "chunk_attn",
        )(qbt, kb, kb, kb, vbt, vbt, vbt, bias_t)

    return lax.cond(spread <= 2 * MAX_SHIFT_BOUND, lambda: call(True), lambda: call(False))


def _out_ple_kernel(*refs, n_o):
    o_refs = refs[:n_o]
    gt_ref, h_ref, p_ref, wo_ref, pgn_ref, wgate_ref, pew_ref, out_ref = refs[n_o:]
    n_sub = TM // OUT_SUB
    subs = [slice(i * OUT_SUB, (i + 1) * OUT_SUB) for i in range(n_sub)]

    def gated(ts):
        ot = jnp.concatenate([r[0, :, ts] for r in o_refs], axis=0) if n_o > 1 else o_refs[0][0, :, ts]
        g = gt_ref[0, :, ts].astype(F32)
        return (ot.astype(F32) * (g * _sigmoid(g))).astype(BF16)

    mixed_t = [_nn(wo_ref[...], gated(ts)) for ts in subs]
    h1 = [h_ref[0, ts, :] + m.T for ts, m in zip(subs, mixed_t)]
    gate = [_nn(_rms_rows(x, pgn_ref[...]).astype(BF16), wgate_ref[...]) for x in h1]
    pe = [_nn(p_ref[0, 0, ts, :].astype(BF16), pew_ref[...]) for ts in subs]
    for ts, x, gt, e in zip(subs, h1, gate, pe):
        out_ref[0, ts, :] = x + e * _sigmoid(gt)


def _out_ple(o_list, gt, h, p, layer, wo_t, pgn, wgate, pew):
    B, S, _ = h.shape
    full = lambda shape: pl.BlockSpec(shape, lambda b, t: (0,) * len(shape))
    width = gt.shape[1]
    return pl.pallas_call(
        functools.partial(_out_ple_kernel, n_o=len(o_list)),
        grid=(B, S // TM),
        in_specs=[pl.BlockSpec((1, o.shape[1], TM), lambda b, t: (b, 0, t)) for o in o_list] + [
            pl.BlockSpec((1, width, TM), lambda b, t: (b, 0, t)),
            pl.BlockSpec((1, TM, D_MODEL), lambda b, t: (b, t, 0)),
            pl.BlockSpec((1, 1, TM, PLE_DIM), lambda b, t: (layer, b, t, 0)),
            full((D_MODEL, width)), full((1, D_MODEL)), full((D_MODEL, D_MODEL)),
            full((PLE_DIM, D_MODEL)),
        ],
        out_specs=pl.BlockSpec((1, TM, D_MODEL), lambda b, t: (b, t, 0)),
        out_shape=jax.ShapeDtypeStruct((B, S, D_MODEL), F32),
        compiler_params=pltpu.CompilerParams(
            dimension_semantics=("parallel", "parallel"), vmem_limit_bytes=VMEM_LIMIT),
        name="out_ple",
    )(*o_list, gt, h, p, wo_t, pgn, wgate, pew)


def _col(v):
    return v.astype(F32).reshape(-1, 1)


def _score_bound(qg, kg, d):
    return 1.02 * d * jnp.max(jnp.abs(qg)) * jnp.max(jnp.abs(kg)) + 1.0


def _first_live_group(cum2, tqf):
    nb = tqf // TK
    first_q = cum2[:, :, ::tqf]
    last_k = cum2[:, :, TK - 1::TK]
    dead = last_k[:, :, None, :] > first_q[:, :, :, None] + UNDERFLOW_LOG2
    return (jnp.sum(dead, axis=-1) // nb).astype(jnp.int32)


def _prev_reach(cum2, tqf):
    B, H, S = cum2.shape
    nb = tqf // TK
    first_q = cum2[:, :, ::TK].reshape(B, H, S // tqf, nb)
    last_k = jnp.roll(cum2[:, :, TK - 1::TK].reshape(B, H, S // tqf, nb), 1, axis=2)
    live = first_q[:, :, :, None, :] - last_k[:, :, :, :, None] >= -UNDERFLOW_LOG2
    gap = np.arange(nb)[None, :] + (nb - 1 - np.arange(nb))[:, None]
    return jnp.max(jnp.where(live, gap, -1), axis=(3, 4)).astype(jnp.int32)


def _chunk_bias_table(rel_bias, bound_qk):
    n = 4 * TC
    i = np.arange(n)
    dist = np.where(i < TC, i, i - n) + 2 * TC
    rb = rel_bias.astype(F32) * LOG2E
    spread = 2.0 * bound_qk + (jnp.max(rb) - jnp.min(rb))
    e = rb[:, np.clip(dist, -REL_CLIP, REL_CLIP) + REL_CLIP] - (bound_qk + jnp.max(rb))
    h = rel_bias.shape[0]
    return _bias_table(e.reshape(h, 1, n)), spread


def _bias_table(e):
    h, _, n = e.shape
    return pl.pallas_call(
        _bias_table_kernel,
        grid=(h,),
        in_specs=[pl.BlockSpec((1, 1, n), lambda hd: (hd, 0, 0))],
        out_specs=pl.BlockSpec((1, 3 * TC, TC), lambda hd: (hd, 0, 0)),
        out_shape=jax.ShapeDtypeStruct((h, 3 * TC, TC), F32),
        name="bias_table",
    )(e)


def _bias_table_kernel(e_ref, o_ref):
    n = e_ref.shape[2]
    rows = jnp.broadcast_to(e_ref[0], (3 * TC, n))
    skew = pltpu.roll(rows, 0, 1, stride=1, stride_axis=0)
    kchunk = lax.broadcasted_iota(jnp.int32, (3 * TC, TC), 0) // CHUNK
    qchunk = lax.broadcasted_iota(jnp.int32, (3 * TC, TC), 1) // CHUNK + (2 * TC) // CHUNK
    valid = (kchunk <= qchunk) & (kchunk >= qchunk - LEFT_CHUNKS)
    o_ref[0] = jnp.where(valid, skew[:, 0:TC], NEG_INF)


def kernel(x, p, positions, norm_g, ab_w_in, mla_q_norm, mla_w_uq, mla_kv_norm, mla_w_ukv, mla_q_gain, mla_k_gain, chk_q_gain, chk_k_gain, chk_rel_bias, ab_w_out, fox_w_in, fox_b_f, fox_q_gain, fox_k_gain, fox_w_out, pe_w, pe_gate_norm, pe_gate_w):
    B, S, _ = x.shape
    half = MLA_ROPE // 2
    inv_freq = 1.0 / (ROPE_THETA ** (jnp.arange(half, dtype=F32) / half))
    ang = positions.astype(F32)[:, None, :] * inv_freq[None, :, None]
    cos, sin = jnp.cos(ang), jnp.sin(ang)
    tri = (np.arange(TM)[:, None] <= np.arange(TM)[None, :]).astype(np.float32)
    tri = jnp.asarray(tri, BF16)

    h = x
    for i in range(DEPTH):
        l = i // 2
        ng = norm_g[i].astype(F32).reshape(1, -1)
        if i % 2 == 0:
            qg, kg = _col(mla_q_gain[l]) * (MLA_QK ** -0.5 * LOG2E), _col(mla_k_gain[l])
            bound = _score_bound(qg, kg, MLA_QK)
            cqg, ckg = _col(chk_q_gain[l]) * (CHK_DIM ** -0.5 * LOG2E), _col(chk_k_gain[l])
            qt, k, vt, gt, qbt, kb, vbt = _ab_proj(
                h, ng, ab_w_in[l].T.astype(BF16),
                _col(mla_q_norm[l]), mla_w_uq[l].T.astype(BF16),
                _col(mla_kv_norm[l]), mla_w_ukv[l].T.astype(BF16),
                qg, kg, cqg, ckg, cos, sin, bound.reshape(1, 1))
            dense = jnp.zeros((B, MLA_HEADS, S // TQ_DENSE), jnp.int32)
            o_a = _causal_attn(qt, k, vt, CHUNK, bound, dense, dense + 2 * (TQ_DENSE // TK), TQ_DENSE, (4, 2, 1))
            bias_t, spread = _chunk_bias_table(chk_rel_bias[l], _score_bound(cqg, ckg, CHK_DIM))
            o_b = _chunk_attn(qbt, kb, vbt, bias_t, spread)
            o_list, wo = [o_a, o_b], ab_w_out[l]
        else:
            qg, kg = _col(fox_q_gain[l]) * (FOX_DIM ** -0.5 * LOG2E), _col(fox_k_gain[l])
            bound = _score_bound(qg, kg, FOX_DIM)
            qa, ka, vt, gt, cum2 = _fox_proj(h, ng, fox_w_in[l].T.astype(BF16), _col(fox_b_f[l]), qg, kg,
                                             tri, bound.reshape(1, 1))
            o_c = _causal_attn(qa, ka, vt, 1, bound, _first_live_group(cum2, TQ_DECAY),
                               _prev_reach(cum2, TQ_DECAY), TQ_DECAY, (2, 1))
            o_list, wo = [o_c], fox_w_out[l]
        h = _out_ple(o_list, gt, h, p, i, wo.T.astype(BF16),
                     pe_gate_norm[i].astype(F32).reshape(1, -1),
                     pe_gate_w[i].astype(BF16), pe_w[i].astype(BF16))
    return h
```

```python
import functools

import numpy as np
import jax
import jax.numpy as jnp
from jax import lax
from jax.experimental import pallas as pl
from jax.experimental.pallas import tpu as pltpu

F32 = jnp.float32
BF16 = jnp.bfloat16

D_MODEL = 1024
DEPTH = 4
CHUNK = 64
PLE_DIM = 256
EPS = 1e-6
NEG_INF = -1e30
MLA_HEADS = 8
MLA_NOPE = 64
MLA_ROPE = 32
MLA_QK = MLA_NOPE + MLA_ROPE
MLA_V = 64
MLA_Q_LORA = 384
MLA_KV_LORA = 256
ROPE_THETA = 10000.0
CHK_HEADS = 8
CHK_DIM = 64
LEFT_CHUNKS = 8
REL_CLIP = 256
FOX_HEADS = 16
FOX_DIM = 64
MLA_WIDTH = MLA_HEADS * MLA_V
CHK_WIDTH = CHK_HEADS * CHK_DIM
FOX_WIDTH = FOX_HEADS * FOX_DIM

TM = 512
TK = 256
TQ = 512
TQ_DENSE = 1024
TQ_DECAY = 1024
TC = 256
OUT_SUB = 256
PROJ_ROWS = 256
QK_PAD = 128
V_PAD = 80
HEADS_PER_STEP = 4
LOOKAHEAD = 3
CHUNK_LOOKAHEAD = 2
LOG2E = 1.4426950408889634
UNDERFLOW_LOG2 = 160.0
MAX_SHIFT_BOUND = 45.0
VMEM_LIMIT = 56 * 1024 * 1024


def _nt(a, b):
    return lax.dot_general(a, b, (((1,), (1,)), ((), ())), preferred_element_type=F32)


def _nn(a, b):
    return jnp.dot(a, b, preferred_element_type=F32)


def _sigmoid(x):
    return 1.0 / (1.0 + jnp.exp(-x))


def _rms_rows(x, g):
    ms = jnp.mean(x * x, axis=-1, keepdims=True)
    return x * lax.rsqrt(ms + EPS) * g


def _split3(x):
    hi = x.astype(BF16)
    r1 = x - hi.astype(F32)
    mid = r1.astype(BF16)
    lo = (r1 - mid.astype(F32)).astype(BF16)
    return hi, mid, lo


def _store_kv_blocks(k_ref, vt_ref, hd, k_t, v_t, ones_row):
    k_rows = k_t.T.astype(BF16)
    v_aug = jnp.concatenate([v_t.astype(BF16), ones_row], axis=0)
    for c in range(TM // TK):
        k_ref[0, hd, c] = k_rows[c * TK:(c + 1) * TK]
        vt_ref[0, hd, c] = v_aug[:, c * TK:(c + 1) * TK]


def _rope_t(x, cos, sin):
    half = x.shape[0] // 2
    x1, x2 = x[:half], x[half:]
    return jnp.concatenate([x1 * cos - x2 * sin, x2 * cos + x1 * sin], axis=0)


def _fox_proj_kernel(h_ref, ng_ref, w_ref, bf_ref, qg_ref, kg_ref,
                     tri_ref, bound_ref, qa_ref, ka_ref, vt_ref, gt_ref, cum_ref, carry_ref):
    t = pl.program_id(1)
    u = _rms_rows(h_ref[0], ng_ref[...]).astype(BF16)

    wq0, wk0, wv0, wg0 = 0, FOX_WIDTH, 2 * FOX_WIDTH, 3 * FOX_WIDTH
    g_last = _nt(w_ref[wg0 + FOX_WIDTH - PROJ_ROWS:wg0 + FOX_WIDTH + FOX_HEADS, :], u)
    z = g_last[PROJ_ROWS:PROJ_ROWS + FOX_HEADS] + bf_ref[...]
    logf = jnp.minimum(z, 0.0) - jnp.log(1.0 + jnp.exp(-jnp.abs(z)))
    parts = jnp.concatenate(_split3(logf), axis=0)
    c3 = _nn(parts, tri_ref[...])
    local = c3[0:FOX_HEADS] + c3[FOX_HEADS:2 * FOX_HEADS] + c3[2 * FOX_HEADS:3 * FOX_HEADS]

    @pl.when(t == 0)
    def _():
        carry_ref[...] = jnp.zeros_like(carry_ref)

    cum = local + carry_ref[:, 0:1]
    carry_ref[...] = jnp.broadcast_to(cum[:, TM - 1:TM], carry_ref.shape)
    c_hi, c_mid, c_lo = [p.astype(F32) for p in _split3(cum * LOG2E)]
    cum_ref[0] = cum * LOG2E
    d_hi, d_mid, d_lo = [p.astype(F32) for p in _split3(cum * LOG2E + bound_ref[...])]

    row = lax.broadcasted_iota(jnp.int32, (16, TM), 0)
    ones_row = jnp.where(row == 0, 1.0, 0.0).astype(BF16)
    zeros48 = jnp.zeros((QK_PAD - FOX_DIM - 16, TM), F32)
    qg = qg_ref[...]
    kg = kg_ref[...]
    heads_per_chunk = PROJ_ROWS // FOX_DIM
    for c in range(FOX_WIDTH // PROJ_ROWS):
        rows = slice(c * PROJ_ROWS, (c + 1) * PROJ_ROWS)
        r0, r1 = rows.start, rows.stop
        qc = _nt(w_ref[wq0 + r0:wq0 + r1, :], u)
        kc = _nt(w_ref[wk0 + r0:wk0 + r1, :], u)
        vc = _nt(w_ref[wv0 + r0:wv0 + r1, :], u)
        gc = g_last[0:PROJ_ROWS] if r1 == FOX_WIDTH else _nt(w_ref[wg0 + r0:wg0 + r1, :], u)
        gt_ref[0, rows, :] = gc.astype(BF16)
        for j in range(heads_per_chunk):
            hd = c * heads_per_chunk + j
            hs = slice(j * FOX_DIM, (j + 1) * FOX_DIM)
            hi, mid, lo = c_hi[hd:hd + 1], c_mid[hd:hd + 1], c_lo[hd:hd + 1]
            khi, kmid, klo = d_hi[hd:hd + 1], d_mid[hd:hd + 1], d_lo[hd:hd + 1]
            qh = qc[hs]
            qn = qh * lax.rsqrt(jnp.mean(qh * qh, axis=0, keepdims=True) + EPS) * qg
            exq = jnp.where(row < 3, 1.0, jnp.where(row == 3, hi, jnp.where(row == 4, mid,
                            jnp.where(row == 5, lo, 0.0))))
            qa_ref[0, hd, 0:FOX_DIM, :] = qn.astype(BF16)
            qa_ref[0, hd, FOX_DIM:FOX_DIM + 16, :] = exq.astype(BF16)
            qa_ref[0, hd, FOX_DIM + 16:QK_PAD, :] = zeros48.astype(BF16)
            kh = kc[hs]
            kn = kh * lax.rsqrt(jnp.mean(kh * kh, axis=0, keepdims=True) + EPS) * kg
            exk = jnp.where(row == 0, -khi, jnp.where(row == 1, -kmid, jnp.where(row == 2, -klo,
                            jnp.where(row < 6, 1.0, 0.0))))
            kfull = jnp.concatenate([kn, exk, zeros48], axis=0)
            _store_kv_blocks(ka_ref, vt_ref, hd, kfull, vc[hs], ones_row)


def _fox_proj(h, ng, w_t, bfc, qg, kg, tri, bound):
    B, S, _ = h.shape
    nt = S // TM
    full = lambda shape: pl.BlockSpec(shape, lambda b, t: (0,) * len(shape))
    return pl.pallas_call(
        _fox_proj_kernel,
        grid=(B, nt),
        in_specs=[
            pl.BlockSpec((1, TM, D_MODEL), lambda b, t: (b, t, 0)),
            full((1, D_MODEL)),
            full((4 * FOX_WIDTH + FOX_HEADS, D_MODEL)),
            full((FOX_HEADS, 1)),
            full((FOX_DIM, 1)), full((FOX_DIM, 1)),
            full((TM, TM)), full((1, 1)),
        ],
        out_specs=[
            pl.BlockSpec((1, FOX_HEADS, QK_PAD, TM), lambda b, t: (b, 0, 0, t)),
            pl.BlockSpec((1, FOX_HEADS, TM // TK, TK, QK_PAD), lambda b, t: (b, 0, t, 0, 0)),
            pl.BlockSpec((1, FOX_HEADS, TM // TK, V_PAD, TK), lambda b, t: (b, 0, t, 0, 0)),
            pl.BlockSpec((1, FOX_WIDTH, TM), lambda b, t: (b, 0, t)),
            pl.BlockSpec((1, FOX_HEADS, TM), lambda b, t: (b, 0, t)),
        ],
        out_shape=[
            jax.ShapeDtypeStruct((B, FOX_HEADS, QK_PAD, S), BF16),
            jax.ShapeDtypeStruct((B, FOX_HEADS, S // TK, TK, QK_PAD), BF16),
            jax.ShapeDtypeStruct((B, FOX_HEADS, S // TK, V_PAD, TK), BF16),
            jax.ShapeDtypeStruct((B, FOX_WIDTH, S), BF16),
            jax.ShapeDtypeStruct((B, FOX_HEADS, S), F32),
        ],
        scratch_shapes=[pltpu.VMEM((FOX_HEADS, 128), F32)],
        compiler_params=pltpu.CompilerParams(
            dimension_semantics=("parallel", "arbitrary"), vmem_limit_bytes=VMEM_LIMIT),
        name="fox_proj",
    )(h, ng, w_t, bfc, qg, kg, tri, bound)


def _ab_proj_kernel(h_ref, ng_ref, w_ref, qnorm_ref, wuq_ref, kvnorm_ref, wukv_ref, qg_ref, kg_ref,
                    cqg_ref, ckg_ref,
                    cos_ref, sin_ref, bound_ref,
                    qt_ref, k_ref, vt_ref, gt_ref, qbt_ref, kb_ref, vbt_ref):
    u = _rms_rows(h_ref[0], ng_ref[...]).astype(BF16)
    cos = cos_ref[0]
    sin = sin_ref[0]

    off = np.cumsum((0, MLA_Q_LORA, MLA_KV_LORA, MLA_ROPE, MLA_WIDTH) + (CHK_WIDTH,) * 4)

    def rms_cols(x, g):
        return x * lax.rsqrt(jnp.mean(x * x, axis=0, keepdims=True) + EPS) * g

    lat = _nt(w_ref[off[0]:off[3], :], u)
    cqn = rms_cols(lat[off[0]:off[1]], qnorm_ref[...]).astype(BF16)
    ckvn = rms_cols(lat[off[1]:off[2]], kvnorm_ref[...]).astype(BF16)
    krt = lat[off[2]:off[3]]
    ss_kr = jnp.sum(krt * krt, axis=0, keepdims=True)
    q_all = _nn(wuq_ref[...], cqn)
    kv_all = _nn(wukv_ref[...], ckvn)
    for g0, w0 in ((0, off[3]), (MLA_WIDTH, off[7])):
        for c in range(MLA_WIDTH // PROJ_ROWS):
            r0, r1 = c * PROJ_ROWS, (c + 1) * PROJ_ROWS
            gt_ref[0, g0 + r0:g0 + r1, :] = _nt(w_ref[w0 + r0:w0 + r1, :], u).astype(BF16)
    qg = qg_ref[...]
    kg = kg_ref[...]
    row32 = lax.broadcasted_iota(jnp.int32, (QK_PAD - MLA_QK, TM), 0)
    q_pad = jnp.where(row32 == 0, 1.0, 0.0)
    k_pad = jnp.where(row32 == 0, -bound_ref[...], 0.0)
    ones_row = jnp.where(lax.broadcasted_iota(jnp.int32, (V_PAD - MLA_V, TM), 0) == 0, 1.0, 0.0).astype(BF16)
    for hd in range(MLA_HEADS):
        qh = q_all[hd * MLA_QK:(hd + 1) * MLA_QK]
        qn = qh * lax.rsqrt(jnp.mean(qh * qh, axis=0, keepdims=True) + EPS) * qg
        qt_ref[0, hd, 0:MLA_NOPE, :] = qn[0:MLA_NOPE].astype(BF16)
        qt_ref[0, hd, MLA_NOPE:MLA_QK, :] = _rope_t(qn[MLA_NOPE:MLA_QK], cos, sin).astype(BF16)
        qt_ref[0, hd, MLA_QK:QK_PAD, :] = q_pad.astype(BF16)
        kv = kv_all[hd * (MLA_NOPE + MLA_V):(hd + 1) * (MLA_NOPE + MLA_V)]
        kn = kv[0:MLA_NOPE]
        rk = lax.rsqrt((jnp.sum(kn * kn, axis=0, keepdims=True) + ss_kr) * (1.0 / MLA_QK) + EPS)
        kfull = jnp.concatenate([kn * rk * kg[0:MLA_NOPE],
                                 _rope_t(krt * rk * kg[MLA_NOPE:MLA_QK], cos, sin),
                                 k_pad], axis=0)
        _store_kv_blocks(k_ref, vt_ref, hd, kfull, kv[MLA_NOPE:MLA_NOPE + MLA_V], ones_row)

    cqg = cqg_ref[...]
    ckg = ckg_ref[...]
    zeros64 = jnp.zeros((CHK_DIM, TM), F32)
    heads_per_chunk = PROJ_ROWS // CHK_DIM
    for c in range(CHK_WIDTH // PROJ_ROWS):
        r0, r1 = c * PROJ_ROWS, (c + 1) * PROJ_ROWS
        qc = _nt(w_ref[off[4] + r0:off[4] + r1, :], u)
        kc = _nt(w_ref[off[5] + r0:off[5] + r1, :], u)
        vc = _nt(w_ref[off[6] + r0:off[6] + r1, :], u).astype(BF16)
        for j in range(heads_per_chunk):
            vbt_ref[0, c * heads_per_chunk + j, 0:CHK_DIM, :] = vc[j * CHK_DIM:(j + 1) * CHK_DIM]
            vbt_ref[0, c * heads_per_chunk + j, CHK_DIM:V_PAD, :] = ones_row
        kns = []
        for j in range(heads_per_chunk):
            hd = c * heads_per_chunk + j
            hs = slice(j * CHK_DIM, (j + 1) * CHK_DIM)
            qh = qc[hs]
            qn = (qh * lax.rsqrt(jnp.mean(qh * qh, axis=0, keepdims=True) + EPS) * cqg).astype(BF16)
            lo, hi = (0, CHK_DIM) if hd % 2 == 0 else (CHK_DIM, 2 * CHK_DIM)
            qbt_ref[0, hd, lo:hi, :] = qn
            qbt_ref[0, hd, CHK_DIM - lo:2 * CHK_DIM - lo, :] = zeros64.astype(BF16)
            kh = kc[hs]
            kns.append(kh * lax.rsqrt(jnp.mean(kh * kh, axis=0, keepdims=True) + EPS) * ckg)
        for pr in range(heads_per_chunk // 2):
            pair = jnp.concatenate([kns[2 * pr], kns[2 * pr + 1]], axis=0)
            kb_ref[0, c * (heads_per_chunk // 2) + pr] = pair.T.astype(BF16)


def _ab_proj(h, ng, w_t, qnorm, wuq, kvnorm, wukv, qg, kg, cqg, ckg, cos, sin, bound):
    B, S, _ = h.shape
    nt = S // TM
    full = lambda shape: pl.BlockSpec(shape, lambda b, t: (0,) * len(shape))
    return pl.pallas_call(
        _ab_proj_kernel,
        grid=(B, nt),
        in_specs=[
            pl.BlockSpec((1, TM, D_MODEL), lambda b, t: (b, t, 0)),
            full((1, D_MODEL)),
            full(w_t.shape),
            full((MLA_Q_LORA, 1)), full((MLA_HEADS * MLA_QK, MLA_Q_LORA)),
            full((MLA_KV_LORA, 1)), full((MLA_HEADS * (MLA_NOPE + MLA_V), MLA_KV_LORA)),
            full((MLA_QK, 1)), full((MLA_QK, 1)), full((CHK_DIM, 1)), full((CHK_DIM, 1)),
            pl.BlockSpec((1, MLA_ROPE // 2, TM), lambda b, t: (b, 0, t)),
            pl.BlockSpec((1, MLA_ROPE // 2, TM), lambda b, t: (b, 0, t)),
            full((1, 1)),
        ],
        out_specs=[
            pl.BlockSpec((1, MLA_HEADS, QK_PAD, TM), lambda b, t: (b, 0, 0, t)),
            pl.BlockSpec((1, MLA_HEADS, TM // TK, TK, QK_PAD), lambda b, t: (b, 0, t, 0, 0)),
            pl.BlockSpec((1, MLA_HEADS, TM // TK, V_PAD, TK), lambda b, t: (b, 0, t, 0, 0)),
            pl.BlockSpec((1, MLA_WIDTH + CHK_WIDTH, TM), lambda b, t: (b, 0, t)),
            pl.BlockSpec((1, CHK_HEADS, 2 * CHK_DIM, TM), lambda b, t: (b, 0, 0, t)),
            pl.BlockSpec((1, CHK_HEADS // 2, TM, 2 * CHK_DIM), lambda b, t: (b, 0, t, 0)),
            pl.BlockSpec((1, CHK_HEADS, V_PAD, TM), lambda b, t: (b, 0, 0, t)),
        ],
        out_shape=[
            jax.ShapeDtypeStruct((B, MLA_HEADS, QK_PAD, S), BF16),
            jax.ShapeDtypeStruct((B, MLA_HEADS, S // TK, TK, QK_PAD), BF16),
            jax.ShapeDtypeStruct((B, MLA_HEADS, S // TK, V_PAD, TK), BF16),
            jax.ShapeDtypeStruct((B, MLA_WIDTH + CHK_WIDTH, S), BF16),
            jax.ShapeDtypeStruct((B, CHK_HEADS, 2 * CHK_DIM, S), BF16),
            jax.ShapeDtypeStruct((B, CHK_HEADS // 2, S, 2 * CHK_DIM), BF16),
            jax.ShapeDtypeStruct((B, CHK_HEADS, V_PAD, S), BF16),
        ],
        compiler_params=pltpu.CompilerParams(
            dimension_semantics=("parallel", "parallel"), vmem_limit_bytes=VMEM_LIMIT),
        name="ab_proj",
    )(h, ng, w_t, qnorm, wuq, kvnorm, wukv, qg, kg, cqg, ckg, cos, sin, bound)


def _causal_attn_kernel(qt_ref, k_ref, vt_ref, o_ref, *, gran_shift):
    lax.fori_loop(0, qt_ref.shape[1], lambda hh, c: _causal_attn_head(
        hh, qt_ref, k_ref, vt_ref, o_ref, gran_shift=gran_shift), 0)


def _causal_attn_head(hh, qt_ref, k_ref, vt_ref, o_ref, *, gran_shift):
    qi = pl.program_id(2)
    dv = o_ref.shape[1] // qt_ref.shape[1]
    n_sub = TQ // TK
    qts = [qt_ref[0, hh, :, c * TK:(c + 1) * TK] for c in range(n_sub)]

    def scores(c, j, masked):
        s = _nn(k_ref[0, hh, j], qts[c])
        if masked:
            kpos = lax.broadcasted_iota(jnp.int32, (TK, TK), 0)
            qpos = lax.broadcasted_iota(jnp.int32, (TK, TK), 1)
            s = jnp.where((kpos >> gran_shift) <= (qpos >> gran_shift), s, NEG_INF)
        return s

    def update(s, j, carry):
        m, acc = carry
        m_new = jnp.maximum(m, jnp.max(s, axis=0, keepdims=True))
        p = jnp.exp2(s - m_new)
        acc = jnp.exp2(m - m_new) * acc + _nn(vt_ref[0, hh, j], p.astype(BF16))
        return m_new, acc

    def run(work, carries):
        carries = list(carries)
        pending = [scores(*w) for w in work[:LOOKAHEAD]]
        for i, (c, j, _) in enumerate(work):
            if i + LOOKAHEAD < len(work):
                pending.append(scores(*work[i + LOOKAHEAD]))
            carries[c] = update(pending.pop(0), j, carries[c])
        return tuple(carries)

    def steps(jj, carries):
        return run([(c, jj * n_sub + u, False) for u in range(n_sub) for c in range(n_sub)], carries)

    init = (jnp.full((1, TK), NEG_INF, F32), jnp.zeros((V_PAD, TK), F32))
    carries = lax.fori_loop(0, qi, steps, (init,) * n_sub)
    carries = run([(c, qi * n_sub + u, u == c) for c in range(n_sub) for u in range(c + 1)], carries)
    for c in range(n_sub):
        acc = carries[c][1]
        o_ref[0, pl.ds(hh * dv, dv), c * TK:(c + 1) * TK] = (acc[0:dv] / acc[dv:dv + 1]).astype(o_ref.dtype)
    return 0


def _shifted_attn_kernel(first_ref, reach_ref, qt_ref, k_ref, vt_ref, o_ref, *, gran_shift, long_steps):
    lax.fori_loop(0, qt_ref.shape[1], lambda hh, c: _shifted_attn_head(
        hh, first_ref, reach_ref, qt_ref, k_ref, vt_ref, o_ref, gran_shift=gran_shift,
        long_steps=long_steps), 0)


def _shifted_attn_head(hh, first_ref, reach_ref, qt_ref, k_ref, vt_ref, o_ref, *, gran_shift,
                       long_steps):
    qi = pl.program_id(2)
    hpb = qt_ref.shape[1]
    head = pl.program_id(1) * hpb + hh
    step = (pl.program_id(0) * pl.num_programs(1) * hpb + head) * pl.num_programs(2) + qi
    g0 = first_ref[step]
    reach = reach_ref[step]
    dv = o_ref.shape[1] // hpb
    tq = qt_ref.shape[3]
    nb = tq // TK
    qt = qt_ref[0, hh]

    def probs(j, lo, hi, masked):
        s = _nn(k_ref[0, hh, j], qt[:, lo:hi])
        if masked:
            kpos = lax.broadcasted_iota(jnp.int32, s.shape, 0)
            qpos = lax.broadcasted_iota(jnp.int32, s.shape, 1)
            s = jnp.where((kpos >> gran_shift) <= (qpos >> gran_shift), s, NEG_INF)
        return jnp.exp2(s).astype(BF16)

    def run(work, acc):
        pending = [probs(*w) for w in work[:LOOKAHEAD]]
        for i, (j, lo, hi, _) in enumerate(work):
            if i + LOOKAHEAD < len(work):
                pending.append(probs(*work[i + LOOKAHEAD]))
            pv = _nn(vt_ref[0, hh, j], pending.pop(0))
            parts = [acc[:, lo:hi] + pv]
            if lo > 0:
                parts.insert(0, acc[:, :lo])
            if hi < tq:
                parts.append(acc[:, hi:])
            acc = parts[0] if len(parts) == 1 else jnp.concatenate(parts, axis=1)
        return acc

    def full_groups(n_grp, base):
        return lambda jj, a: run([((base + jj * n_grp) * nb + u, 0, tq, False) for u in range(n_grp * nb)], a)

    n_loop = jnp.maximum(qi - 1 - g0, 0)
    acc = jnp.zeros((V_PAD, tq), F32)
    done = 0
    for n_grp in long_steps:
        n_it = (n_loop - done) // n_grp
        acc = lax.fori_loop(0, n_it, full_groups(n_grp, g0 + done), acc)
        done = done + n_it * n_grp
    diag = [(qi * nb + u, u * TK, tq, True) for u in range(nb)]

    def tail(max_reach):
        prev = [((qi - 1) * nb + u, 0, min(max_reach - (nb - 1 - u) + 1, nb) * TK, False)
                for u in range(nb) if max_reach - (nb - 1 - u) >= 0]
        return lambda a: run(prev + diag, a)

    levels = (nb - 2, nb - 1, 2 * nb - 2)
    level = jnp.where(reach <= levels[0], 0, jnp.where(reach <= levels[1], 1, 2))
    acc = lax.switch(jnp.where(qi > 0, level, len(levels)),
                     [tail(r) for r in levels] + [lambda a: run(diag, a)], acc)
    o_ref[0, pl.ds(hh * dv, dv), :] = (acc[0:dv] / acc[dv:dv + 1]).astype(o_ref.dtype)
    return 0


def _causal_attn(qt, k, vt, gran, bound, first_group, reach, tqf, long_steps):
    B, H, _, S = qt.shape
    nk, dv, hpb = k.shape[2], FOX_DIM, HEADS_PER_STEP
    assert TQ % TK == 0 and tqf % TK == 0 and MLA_V == FOX_DIM and vt.shape[3] == V_PAD and H % hpb == 0

    def call(body, tq, name, prefetch, **static):
        grid_spec = pltpu.PrefetchScalarGridSpec(
            num_scalar_prefetch=len(prefetch),
            grid=(B, H // hpb, S // tq),
            in_specs=[
                pl.BlockSpec((1, hpb, QK_PAD, tq), lambda b, h, q, *_: (b, h, 0, q)),
                pl.BlockSpec((1, hpb, nk, TK, QK_PAD), lambda b, h, q, *_: (b, h, 0, 0, 0)),
                pl.BlockSpec((1, hpb, nk, V_PAD, TK), lambda b, h, q, *_: (b, h, 0, 0, 0)),
            ],
            out_specs=pl.BlockSpec((1, hpb * dv, tq), lambda b, h, q, *_: (b, h, q)),
        )
        return pl.pallas_call(
            functools.partial(body, gran_shift=int(np.log2(gran)), **static),
            grid_spec=grid_spec,
            out_shape=jax.ShapeDtypeStruct((B, H * dv, S), BF16),
            compiler_params=pltpu.CompilerParams(
                dimension_semantics=("parallel", "parallel", "arbitrary"), vmem_limit_bytes=VMEM_LIMIT),
            name=name,
        )(*prefetch, qt, k, vt)

    return lax.cond(bound <= MAX_SHIFT_BOUND,
                    lambda: call(_shifted_attn_kernel, tqf, "shifted_attn",
                                 (first_group.reshape(-1), reach.reshape(-1)),
                                 long_steps=long_steps),
                    lambda: call(_causal_attn_kernel, TQ, "causal_attn", ()))


def _chunk_attn_kernel(qt_ref, k0_ref, k1_ref, k2_ref, v0_ref, v1_ref, v2_ref, bias_ref, o_ref, *,
                       shifted):
    t = pl.program_id(1)
    k_refs = (k0_ref, k1_ref, k2_ref)
    v_refs = (v0_ref, v1_ref, v2_ref)

    def scores(hd):
        qt = qt_ref[0, hd]
        ss = []
        for d in range(3):
            s = _nn(k_refs[d][0, hd // 2], qt) + bias_ref[hd, d * TC:(d + 1) * TC, :]
            if d < 2:
                s = jnp.where(t + (d - 2) >= 0, s, NEG_INF)
            ss.append(s)
        return ss

    def finish(hd, ss):
        if not shifted:
            m = jnp.max(jnp.maximum(jnp.maximum(ss[0], ss[1]), ss[2]), axis=0, keepdims=True)
            ss = [s - m for s in ss]
        p = jnp.concatenate([jnp.exp2(s).astype(BF16) for s in ss], axis=0)
        vt = jnp.concatenate([r[0, hd] for r in v_refs], axis=1)
        acc = _nn(vt, p)
        o_ref[0, hd * CHK_DIM:(hd + 1) * CHK_DIM, :] = (
            acc[0:CHK_DIM] / acc[CHK_DIM:CHK_DIM + 1]).astype(o_ref.dtype)

    pending = [scores(hd) for hd in range(CHUNK_LOOKAHEAD)]
    for hd in range(CHK_HEADS):
        if hd + CHUNK_LOOKAHEAD < CHK_HEADS:
            pending.append(scores(hd + CHUNK_LOOKAHEAD))
        finish(hd, pending.pop(0))


def _chunk_attn(qbt, kb, vbt, bias_t, spread):
    B, _, _, S = qbt.shape
    kspec = lambda d: pl.BlockSpec((1, CHK_HEADS // 2, TC, 2 * CHK_DIM),
                                   lambda b, t: (b, 0, jnp.maximum(t + (d - 2), 0), 0))
    vspec = lambda d: pl.BlockSpec((1, CHK_HEADS, V_PAD, TC),
                                   lambda b, t: (b, 0, 0, jnp.maximum(t + (d - 2), 0)))

    def call(shifted):
        return pl.pallas_call(
            functools.partial(_chunk_attn_kernel, shifted=shifted),
            grid=(B, S // TC),
            in_specs=[
                pl.BlockSpec((1, CHK_HEADS, 2 * CHK_DIM, TC), lambda b, t: (b, 0, 0, t)),
                kspec(0), kspec(1), kspec(2), vspec(0), vspec(1), vspec(2),
                pl.BlockSpec((CHK_HEADS, 3 * TC, TC), lambda b, t: (0, 0, 0)),
            ],
            out_specs=pl.BlockSpec((1, CHK_WIDTH, TC), lambda b, t: (b, 0, t)),
            out_shape=jax.ShapeDtypeStruct((B, CHK_WIDTH, S), BF16),
            compiler_params=pltpu.CompilerParams(
                dimension_semantics=("parallel", "parallel"), vmem_limit_bytes=VMEM_LIMIT),
            name="chunk_attn_shifted" if shifted else "chunk_attn",
        )(qbt, kb, kb, kb, vbt, vbt, vbt, bias_t)

    return lax.cond(spread <= 2 * MAX_SHIFT_BOUND, lambda: call(True), lambda: call(False))


def _out_ple_kernel(*refs, n_o):
    o_refs = refs[:n_o]
    gt_ref, h_ref, p_ref, wo_ref, pgn_ref, wgate_ref, pew_ref, out_ref = refs[n_o:]
    n_sub = TM // OUT_SUB
    subs = [slice(i * OUT_SUB, (i + 1) * OUT_SUB) for i in range(n_sub)]

    def gated(ts):
        ot = jnp.concatenate([r[0, :, ts] for r in o_refs], axis=0) if n_o > 1 else o_refs[0][0, :, ts]
        g = gt_ref[0, :, ts].astype(F32)
        return (ot.astype(F32) * (g * _sigmoid(g))).astype(BF16)

    mixed_t = [_nn(wo_ref[...], gated(ts)) for ts in subs]
    h1 = [h_ref[0, ts, :] + m.T for ts, m in zip(subs, mixed_t)]
    gate = [_nn(_rms_rows(x, pgn_ref[...]).astype(BF16), wgate_ref[...]) for x in h1]
    pe = [_nn(p_ref[0, 0, ts, :].astype(BF16), pew_ref[...]) for ts in subs]
    for ts, x, gt, e in zip(subs, h1, gate, pe):
        out_ref[0, ts, :] = x + e * _sigmoid(gt)


def _out_ple(o_list, gt, h, p, layer, wo_t, pgn, wgate, pew):
    B, S, _ = h.shape
    full = lambda shape: pl.BlockSpec(shape, lambda b, t: (0,) * len(shape))
    width = gt.shape[1]
    return pl.pallas_call(
        functools.partial(_out_ple_kernel, n_o=len(o_list)),
        grid=(B, S // TM),
        in_specs=[pl.BlockSpec((1, o.shape[1], TM), lambda b, t: (b, 0, t)) for o in o_list] + [
            pl.BlockSpec((1, width, TM), lambda b, t: (b, 0, t)),
            pl.BlockSpec((1, TM, D_MODEL), lambda b, t: (b, t, 0)),
            pl.BlockSpec((1, 1, TM, PLE_DIM), lambda b, t: (layer, b, t, 0)),
            full((D_MODEL, width)), full((1, D_MODEL)), full((D_MODEL, D_MODEL)),
            full((PLE_DIM, D_MODEL)),
        ],
        out_specs=pl.BlockSpec((1, TM, D_MODEL), lambda b, t: (b, t, 0)),
        out_shape=jax.ShapeDtypeStruct((B, S, D_MODEL), F32),
        compiler_params=pltpu.CompilerParams(
            dimension_semantics=("parallel", "parallel"), vmem_limit_bytes=VMEM_LIMIT),
        name="out_ple",
    )(*o_list, gt, h, p, wo_t, pgn, wgate, pew)


def _col(v):
    return v.astype(F32).reshape(-1, 1)


def _score_bound(qg, kg, d):
    return 1.02 * d * jnp.max(jnp.abs(qg)) * jnp.max(jnp.abs(kg)) + 1.0


def _first_live_group(cum2, tqf):
    nb = tqf // TK
    first_q = cum2[:, :, ::tqf]
    last_k = cum2[:, :, TK - 1::TK]
    dead = last_k[:, :, None, :] > first_q[:, :, :, None] + UNDERFLOW_LOG2
    return (jnp.sum(dead, axis=-1) // nb).astype(jnp.int32)


def _prev_reach(cum2, tqf):
    B, H, S = cum2.shape
    nb = tqf // TK
    first_q = cum2[:, :, ::TK].reshape(B, H, S // tqf, nb)
    last_k = jnp.roll(cum2[:, :, TK - 1::TK].reshape(B, H, S // tqf, nb), 1, axis=2)
    live = first_q[:, :, :, None, :] - last_k[:, :, :, :, None] >= -UNDERFLOW_LOG2
    gap = np.arange(nb)[None, :] + (nb - 1 - np.arange(nb))[:, None]
    return jnp.max(jnp.where(live, gap, -1), axis=(3, 4)).astype(jnp.int32)


def _chunk_bias_table(rel_bias, bound_qk):
    n = 4 * TC
    i = np.arange(n)
    dist = np.where(i < TC, i, i - n) + 2 * TC
    rb = rel_bias.astype(F32) * LOG2E
    spread = 2.0 * bound_qk + (jnp.max(rb) - jnp.min(rb))
    e = rb[:, np.clip(dist, -REL_CLIP, REL_CLIP) + REL_CLIP] - (bound_qk + jnp.max(rb))
    h = rel_bias.shape[0]
    return _bias_table(e.reshape(h, 1, n)), spread


def _bias_table(e):
    h, _, n = e.shape
    return pl.pallas_call(
        _bias_table_kernel,
        grid=(h,),
        in_specs=[pl.BlockSpec((1, 1, n), lambda hd: (hd, 0, 0))],
        out_specs=pl.BlockSpec((1, 3 * TC, TC), lambda hd: (hd, 0, 0)),
        out_shape=jax.ShapeDtypeStruct((h, 3 * TC, TC), F32),
        name="bias_table",
    )(e)


def _bias_table_kernel(e_ref, o_ref):
    n = e_ref.shape[2]
    rows = jnp.broadcast_to(e_ref[0], (3 * TC, n))
    skew = pltpu.roll(rows, 0, 1, stride=1, stride_axis=0)
    kchunk = lax.broadcasted_iota(jnp.int32, (3 * TC, TC), 0) // CHUNK
    qchunk = lax.broadcasted_iota(jnp.int32, (3 * TC, TC), 1) // CHUNK + (2 * TC) // CHUNK
    valid = (kchunk <= qchunk) & (kchunk >= qchunk - LEFT_CHUNKS)
    o_ref[0] = jnp.where(valid, skew[:, 0:TC], NEG_INF)


def kernel(x, p, positions, norm_g, ab_w_in, mla_q_norm, mla_w_uq, mla_kv_norm, mla_w_ukv, mla_q_gain, mla_k_gain, chk_q_gain, chk_k_gain, chk_rel_bias, ab_w_out, fox_w_in, fox_b_f, fox_q_gain, fox_k_gain, fox_w_out, pe_w, pe_gate_norm, pe_gate_w):
    B, S, _ = x.shape
    half = MLA_ROPE // 2
    inv_freq = 1.0 / (ROPE_THETA ** (jnp.arange(half, dtype=F32) / half))
    ang = positions.astype(F32)[:, None, :] * inv_freq[None, :, None]
    cos, sin = jnp.cos(ang), jnp.sin(ang)
    tri = (np.arange(TM)[:, None] <= np.arange(TM)[None, :]).astype(np.float32)
    tri = jnp.asarray(tri, BF16)

    h = x
    for i in range(DEPTH):
        l = i // 2
        ng = norm_g[i].astype(F32).reshape(1, -1)
        if i % 2 == 0:
            qg, kg = _col(mla_q_gain[l]) * (MLA_QK ** -0.5 * LOG2E), _col(mla_k_gain[l])
            bound = _score_bound(qg, kg, MLA_QK)
            cqg, ckg = _col(chk_q_gain[l]) * (CHK_DIM ** -0.5 * LOG2E), _col(chk_k_gain[l])
            qt, k, vt, gt, qbt, kb, vbt = _ab_proj(
                h, ng, ab_w_in[l].T.astype(BF16),
                _col(mla_q_norm[l]), mla_w_uq[l].T.astype(BF16),
                _col(mla_kv_norm[l]), mla_w_ukv[l].T.astype(BF16),
                qg, kg, cqg, ckg, cos, sin, bound.reshape(1, 1))
            dense = jnp.zeros((B, MLA_HEADS, S // TQ_DENSE), jnp.int32)
            o_a = _causal_attn(qt, k, vt, CHUNK, bound, dense, dense + 2 * (TQ_DENSE // TK), TQ_DENSE, (4, 2, 1))
            bias_t, spread = _chunk_bias_table(chk_rel_bias[l], _score_bound(cqg, ckg, CHK_DIM))
            o_b = _chunk_attn(qbt, kb, vbt, bias_t, spread)
            o_list, wo = [o_a, o_b], ab_w_out[l]
        else:
            qg, kg = _col(fox_q_gain[l]) * (FOX_DIM ** -0.5 * LOG2E), _col(fox_k_gain[l])
            bound = _score_bound(qg, kg, FOX_DIM)
            qa, ka, vt, gt, cum2 = _fox_proj(h, ng, fox_w_in[l].T.astype(BF16), _col(fox_b_f[l]), qg, kg,
                                             tri, bound.reshape(1, 1))
            o_c = _causal_attn(qa, ka, vt, 1, bound, _first_live_group(cum2, TQ_DECAY),
                               _prev_reach(cum2, TQ_DECAY), TQ_DECAY, (2, 1))
            o_list, wo = [o_c], fox_w_out[l]
        h = _out_ple(o_list, gt, h, p, i, wo.T.astype(BF16),
                     pe_gate_norm[i].astype(F32).reshape(1, -1),
                     pe_gate_w[i].astype(BF16), pe_w[i].astype(BF16))
    return h
```

```python
import functools

import numpy as np
import jax
import jax.numpy as jnp
from jax import lax
from jax.experimental import pallas as pl
from jax.experimental.pallas import tpu as pltpu

F32 = jnp.float32
BF16 = jnp.bfloat16

D_MODEL = 1024
DEPTH = 4
CHUNK = 64
PLE_DIM = 256
EPS = 1e-6
NEG_INF = -1e30
MLA_HEADS = 8
MLA_NOPE = 64
MLA_ROPE = 32
MLA_QK = MLA_NOPE + MLA_ROPE
MLA_V = 64
MLA_Q_LORA = 384
MLA_KV_LORA = 256
ROPE_THETA = 10000.0
CHK_HEADS = 8
CHK_DIM = 64
LEFT_CHUNKS = 8
REL_CLIP = 256
FOX_HEADS = 16
FOX_DIM = 64
MLA_WIDTH = MLA_HEADS * MLA_V
CHK_WIDTH = CHK_HEADS * CHK_DIM
FOX_WIDTH = FOX_HEADS * FOX_DIM

TM = 512
TK = 256
TQ = 512
TQ_DENSE = 1024
TQ_DECAY = 1024
TC = 256
TM_OUT = 1024
OUT_SUB = 256
PROJ_ROWS = 256
QK_PAD = 128
V_PAD = 80
HEADS_PER_STEP = 4
LOOKAHEAD = 3
CHUNK_LOOKAHEAD = 2
LOG2E = 1.4426950408889634
UNDERFLOW_LOG2 = 160.0
MAX_SHIFT_BOUND = 45.0
VMEM_LIMIT = 56 * 1024 * 1024


def _nt(a, b):
    return lax.dot_general(a, b, (((1,), (1,)), ((), ())), preferred_element_type=F32)


def _nn(a, b):
    return jnp.dot(a, b, preferred_element_type=F32)


def _sigmoid(x):
    return 1.0 / (1.0 + jnp.exp(-x))


def _rms_rows(x, g):
    ms = jnp.mean(x * x, axis=-1, keepdims=True)
    return x * lax.rsqrt(ms + EPS) * g


def _split3(x):
    hi = x.astype(BF16)
    r1 = x - hi.astype(F32)
    mid = r1.astype(BF16)
    lo = (r1 - mid.astype(F32)).astype(BF16)
    return hi, mid, lo


def _store_kv_blocks(k_ref, vt_ref, hd, k_t, v_t, ones_row):
    k_rows = k_t.T.astype(BF16)
    v_aug = jnp.concatenate([v_t.astype(BF16), ones_row], axis=0)
    for c in range(TM // TK):
        k_ref[0, hd, c] = k_rows[c * TK:(c + 1) * TK]
        vt_ref[0, hd, c] = v_aug[:, c * TK:(c + 1) * TK]


def _rope_t(x, cos, sin):
    half = x.shape[0] // 2
    x1, x2 = x[:half], x[half:]
    return jnp.concatenate([x1 * cos - x2 * sin, x2 * cos + x1 * sin], axis=0)


def _fox_proj_kernel(h_ref, ng_ref, w_ref, bf_ref, qg_ref, kg_ref,
                     tri_ref, bound_ref, qa_ref, ka_ref, vt_ref, gt_ref, cum_ref, carry_ref):
    t = pl.program_id(1)
    u = _rms_rows(h_ref[0], ng_ref[...]).astype(BF16)

    wq0, wk0, wv0, wg0 = 0, FOX_WIDTH, 2 * FOX_WIDTH, 3 * FOX_WIDTH
    g_last = _nt(w_ref[wg0 + FOX_WIDTH - PROJ_ROWS:wg0 + FOX_WIDTH + FOX_HEADS, :], u)
    z = g_last[PROJ_ROWS:PROJ_ROWS + FOX_HEADS] + bf_ref[...]
    logf = jnp.minimum(z, 0.0) - jnp.log(1.0 + jnp.exp(-jnp.abs(z)))
    parts = jnp.concatenate(_split3(logf), axis=0)
    c3 = _nn(parts, tri_ref[...])
    local = c3[0:FOX_HEADS] + c3[FOX_HEADS:2 * FOX_HEADS] + c3[2 * FOX_HEADS:3 * FOX_HEADS]

    @pl.when(t == 0)
    def _():
        carry_ref[...] = jnp.zeros_like(carry_ref)

    cum = local + carry_ref[:, 0:1]
    carry_ref[...] = jnp.broadcast_to(cum[:, TM - 1:TM], carry_ref.shape)
    c_hi, c_mid, c_lo = [p.astype(F32) for p in _split3(cum * LOG2E)]
    cum_ref[0] = cum * LOG2E
    d_hi, d_mid, d_lo = [p.astype(F32) for p in _split3(cum * LOG2E + bound_ref[...])]

    row = lax.broadcasted_iota(jnp.int32, (16, TM), 0)
    ones_row = jnp.where(row == 0, 1.0, 0.0).astype(BF16)
    zeros48 = jnp.zeros((QK_PAD - FOX_DIM - 16, TM), F32)
    qg = qg_ref[...]
    kg = kg_ref[...]
    heads_per_chunk = PROJ_ROWS // FOX_DIM
    for c in range(FOX_WIDTH // PROJ_ROWS):
        rows = slice(c * PROJ_ROWS, (c + 1) * PROJ_ROWS)
        r0, r1 = rows.start, rows.stop
        qc = _nt(w_ref[wq0 + r0:wq0 + r1, :], u)
        kc = _nt(w_ref[wk0 + r0:wk0 + r1, :], u)
        vc = _nt(w_ref[wv0 + r0:wv0 + r1, :], u)
        gc = g_last[0:PROJ_ROWS] if r1 == FOX_WIDTH else _nt(w_ref[wg0 + r0:wg0 + r1, :], u)
        gt_ref[0, rows, :] = gc.astype(BF16)
        for j in range(heads_per_chunk):
            hd = c * heads_per_chunk + j
            hs = slice(j * FOX_DIM, (j + 1) * FOX_DIM)
            hi, mid, lo = c_hi[hd:hd + 1], c_mid[hd:hd + 1], c_lo[hd:hd + 1]
            khi, kmid, klo = d_hi[hd:hd + 1], d_mid[hd:hd + 1], d_lo[hd:hd + 1]
            qh = qc[hs]
            qn = qh * lax.rsqrt(jnp.mean(qh * qh, axis=0, keepdims=True) + EPS) * qg
            exq = jnp.where(row < 3, 1.0, jnp.where(row == 3, hi, jnp.where(row == 4, mid,
                            jnp.where(row == 5, lo, 0.0))))
            qa_ref[0, hd, 0:FOX_DIM, :] = qn.astype(BF16)
            qa_ref[0, hd, FOX_DIM:FOX_DIM + 16, :] = exq.astype(BF16)
            qa_ref[0, hd, FOX_DIM + 16:QK_PAD, :] = zeros48.astype(BF16)
            kh = kc[hs]
            kn = kh * lax.rsqrt(jnp.mean(kh * kh, axis=0, keepdims=True) + EPS) * kg
            exk = jnp.where(row == 0, -khi, jnp.where(row == 1, -kmid, jnp.where(row == 2, -klo,
                            jnp.where(row < 6, 1.0, 0.0))))
            kfull = jnp.concatenate([kn, exk, zeros48], axis=0)
            _store_kv_blocks(ka_ref, vt_ref, hd, kfull, vc[hs], ones_row)


def _fox_proj(h, ng, w_t, bfc, qg, kg, tri, bound):
    B, S, _ = h.shape
    nt = S // TM
    full = lambda shape: pl.BlockSpec(shape, lambda b, t: (0,) * len(shape))
    return pl.pallas_call(
        _fox_proj_kernel,
        grid=(B, nt),
        in_specs=[
            pl.BlockSpec((1, TM, D_MODEL), lambda b, t: (b, t, 0)),
            full((1, D_MODEL)),
            full((4 * FOX_WIDTH + FOX_HEADS, D_MODEL)),
            full((FOX_HEADS, 1)),
            full((FOX_DIM, 1)), full((FOX_DIM, 1)),
            full((TM, TM)), full((1, 1)),
        ],
        out_specs=[
            pl.BlockSpec((1, FOX_HEADS, QK_PAD, TM), lambda b, t: (b, 0, 0, t)),
            pl.BlockSpec((1, FOX_HEADS, TM // TK, TK, QK_PAD), lambda b, t: (b, 0, t, 0, 0)),
            pl.BlockSpec((1, FOX_HEADS, TM // TK, V_PAD, TK), lambda b, t: (b, 0, t, 0, 0)),
            pl.BlockSpec((1, FOX_WIDTH, TM), lambda b, t: (b, 0, t)),
            pl.BlockSpec((1, FOX_HEADS, TM), lambda b, t: (b, 0, t)),
        ],
        out_shape=[
            jax.ShapeDtypeStruct((B, FOX_HEADS, QK_PAD, S), BF16),
            jax.ShapeDtypeStruct((B, FOX_HEADS, S // TK, TK, QK_PAD), BF16),
            jax.ShapeDtypeStruct((B, FOX_HEADS, S // TK, V_PAD, TK), BF16),
            jax.ShapeDtypeStruct((B, FOX_WIDTH, S), BF16),
            jax.ShapeDtypeStruct((B, FOX_HEADS, S), F32),
        ],
        scratch_shapes=[pltpu.VMEM((FOX_HEADS, 128), F32)],
        compiler_params=pltpu.CompilerParams(
            dimension_semantics=("parallel", "arbitrary"), vmem_limit_bytes=VMEM_LIMIT),
        name="fox_proj",
    )(h, ng, w_t, bfc, qg, kg, tri, bound)


def _ab_proj_kernel(h_ref, ng_ref, w_ref, qnorm_ref, wuq_ref, kvnorm_ref, wukv_ref, qg_ref, kg_ref,
                    cqg_ref, ckg_ref,
                    cos_ref, sin_ref, bound_ref,
                    qt_ref, k_ref, vt_ref, gt_ref, qbt_ref, kb_ref, vbt_ref):
    u = _rms_rows(h_ref[0], ng_ref[...]).astype(BF16)
    cos = cos_ref[0]
    sin = sin_ref[0]

    off = np.cumsum((0, MLA_Q_LORA, MLA_KV_LORA, MLA_ROPE, MLA_WIDTH) + (CHK_WIDTH,) * 4)

    def rms_cols(x, g):
        return x * lax.rsqrt(jnp.mean(x * x, axis=0, keepdims=True) + EPS) * g

    lat = _nt(w_ref[off[0]:off[3], :], u)
    cqn = rms_cols(lat[off[0]:off[1]], qnorm_ref[...]).astype(BF16)
    ckvn = rms_cols(lat[off[1]:off[2]], kvnorm_ref[...]).astype(BF16)
    krt = lat[off[2]:off[3]]
    ss_kr = jnp.sum(krt * krt, axis=0, keepdims=True)
    q_all = _nn(wuq_ref[...], cqn)
    kv_all = _nn(wukv_ref[...], ckvn)
    for g0, w0 in ((0, off[3]), (MLA_WIDTH, off[7])):
        for c in range(MLA_WIDTH // PROJ_ROWS):
            r0, r1 = c * PROJ_ROWS, (c + 1) * PROJ_ROWS
            gt_ref[0, g0 + r0:g0 + r1, :] = _nt(w_ref[w0 + r0:w0 + r1, :], u).astype(BF16)
    qg = qg_ref[...]
    kg = kg_ref[...]
    row32 = lax.broadcasted_iota(jnp.int32, (QK_PAD - MLA_QK, TM), 0)
    q_pad = jnp.where(row32 == 0, 1.0, 0.0)
    k_pad = jnp.where(row32 == 0, -bound_ref[...], 0.0)
    ones_row = jnp.where(lax.broadcasted_iota(jnp.int32, (V_PAD - MLA_V, TM), 0) == 0, 1.0, 0.0).astype(BF16)
    for hd in range(MLA_HEADS):
        qh = q_all[hd * MLA_QK:(hd + 1) * MLA_QK]
        qn = qh * lax.rsqrt(jnp.mean(qh * qh, axis=0, keepdims=True) + EPS) * qg
        qt_ref[0, hd, 0:MLA_NOPE, :] = qn[0:MLA_NOPE].astype(BF16)
        qt_ref[0, hd, MLA_NOPE:MLA_QK, :] = _rope_t(qn[MLA_NOPE:MLA_QK], cos, sin).astype(BF16)
        qt_ref[0, hd, MLA_QK:QK_PAD, :] = q_pad.astype(BF16)
        kv = kv_all[hd * (MLA_NOPE + MLA_V):(hd + 1) * (MLA_NOPE + MLA_V)]
        kn = kv[0:MLA_NOPE]
        rk = lax.rsqrt((jnp.sum(kn * kn, axis=0, keepdims=True) + ss_kr) * (1.0 / MLA_QK) + EPS)
        kfull = jnp.concatenate([kn * rk * kg[0:MLA_NOPE],
                                 _rope_t(krt * rk * kg[MLA_NOPE:MLA_QK], cos, sin),
                                 k_pad], axis=0)
        _store_kv_blocks(k_ref, vt_ref, hd, kfull, kv[MLA_NOPE:MLA_NOPE + MLA_V], ones_row)

    cqg = cqg_ref[...]
    ckg = ckg_ref[...]
    zeros64 = jnp.zeros((CHK_DIM, TM), F32)
    heads_per_chunk = PROJ_ROWS // CHK_DIM
    for c in range(CHK_WIDTH // PROJ_ROWS):
        r0, r1 = c * PROJ_ROWS, (c + 1) * PROJ_ROWS
        qc = _nt(w_ref[off[4] + r0:off[4] + r1, :], u)
        kc = _nt(w_ref[off[5] + r0:off[5] + r1, :], u)
        vc = _nt(w_ref[off[6] + r0:off[6] + r1, :], u).astype(BF16)
        for j in range(heads_per_chunk):
            vbt_ref[0, c * heads_per_chunk + j, 0:CHK_DIM, :] = vc[j * CHK_DIM:(j + 1) * CHK_DIM]
            vbt_ref[0, c * heads_per_chunk + j, CHK_DIM:V_PAD, :] = ones_row
        kns = []
        for j in range(heads_per_chunk):
            hd = c * heads_per_chunk + j
            hs = slice(j * CHK_DIM, (j + 1) * CHK_DIM)
            qh = qc[hs]
            qn = (qh * lax.rsqrt(jnp.mean(qh * qh, axis=0, keepdims=True) + EPS) * cqg).astype(BF16)
            lo, hi = (0, CHK_DIM) if hd % 2 == 0 else (CHK_DIM, 2 * CHK_DIM)
            qbt_ref[0, hd, lo:hi, :] = qn
            qbt_ref[0, hd, CHK_DIM - lo:2 * CHK_DIM - lo, :] = zeros64.astype(BF16)
            kh = kc[hs]
            kns.append(kh * lax.rsqrt(jnp.mean(kh * kh, axis=0, keepdims=True) + EPS) * ckg)
        for pr in range(heads_per_chunk // 2):
            pair = jnp.concatenate([kns[2 * pr], kns[2 * pr + 1]], axis=0)
            kb_ref[0, c * (heads_per_chunk // 2) + pr] = pair.T.astype(BF16)


def _ab_proj(h, ng, w_t, qnorm, wuq, kvnorm, wukv, qg, kg, cqg, ckg, cos, sin, bound):
    B, S, _ = h.shape
    nt = S // TM
    full = lambda shape: pl.BlockSpec(shape, lambda b, t: (0,) * len(shape))
    return pl.pallas_call(
        _ab_proj_kernel,
        grid=(B, nt),
        in_specs=[
            pl.BlockSpec((1, TM, D_MODEL), lambda b, t: (b, t, 0)),
            full((1, D_MODEL)),
            full(w_t.shape),
            full((MLA_Q_LORA, 1)), full((MLA_HEADS * MLA_QK, MLA_Q_LORA)),
            full((MLA_KV_LORA, 1)), full((MLA_HEADS * (MLA_NOPE + MLA_V), MLA_KV_LORA)),
            full((MLA_QK, 1)), full((MLA_QK, 1)), full((CHK_DIM, 1)), full((CHK_DIM, 1)),
            pl.BlockSpec((1, MLA_ROPE // 2, TM), lambda b, t: (b, 0, t)),
            pl.BlockSpec((1, MLA_ROPE // 2, TM), lambda b, t: (b, 0, t)),
            full((1, 1)),
        ],
        out_specs=[
            pl.BlockSpec((1, MLA_HEADS, QK_PAD, TM), lambda b, t: (b, 0, 0, t)),
            pl.BlockSpec((1, MLA_HEADS, TM // TK, TK, QK_PAD), lambda b, t: (b, 0, t, 0, 0)),
            pl.BlockSpec((1, MLA_HEADS, TM // TK, V_PAD, TK), lambda b, t: (b, 0, t, 0, 0)),
            pl.BlockSpec((1, MLA_WIDTH + CHK_WIDTH, TM), lambda b, t: (b, 0, t)),
            pl.BlockSpec((1, CHK_HEADS, 2 * CHK_DIM, TM), lambda b, t: (b, 0, 0, t)),
            pl.BlockSpec((1, CHK_HEADS // 2, TM, 2 * CHK_DIM), lambda b, t: (b, 0, t, 0)),
            pl.BlockSpec((1, CHK_HEADS, V_PAD, TM), lambda b, t: (b, 0, 0, t)),
        ],
        out_shape=[
            jax.ShapeDtypeStruct((B, MLA_HEADS, QK_PAD, S), BF16),
            jax.ShapeDtypeStruct((B, MLA_HEADS, S // TK, TK, QK_PAD), BF16),
            jax.ShapeDtypeStruct((B, MLA_HEADS, S // TK, V_PAD, TK), BF16),
            jax.ShapeDtypeStruct((B, MLA_WIDTH + CHK_WIDTH, S), BF16),
            jax.ShapeDtypeStruct((B, CHK_HEADS, 2 * CHK_DIM, S), BF16),
            jax.ShapeDtypeStruct((B, CHK_HEADS // 2, S, 2 * CHK_DIM), BF16),
            jax.ShapeDtypeStruct((B, CHK_HEADS, V_PAD, S), BF16),
        ],
        compiler_params=pltpu.CompilerParams(
            dimension_semantics=("parallel", "parallel"), vmem_limit_bytes=VMEM_LIMIT),
        name="ab_proj",
    )(h, ng, w_t, qnorm, wuq, kvnorm, wukv, qg, kg, cqg, ckg, cos, sin, bound)


def _causal_attn_kernel(qt_ref, k_ref, vt_ref, o_ref, *, gran_shift):
    lax.fori_loop(0, qt_ref.shape[1], lambda hh, c: _causal_attn_head(
        hh, qt_ref, k_ref, vt_ref, o_ref, gran_shift=gran_shift), 0)


def _causal_attn_head(hh, qt_ref, k_ref, vt_ref, o_ref, *, gran_shift):
    qi = pl.program_id(2)
    dv = o_ref.shape[1] // qt_ref.shape[1]
    n_sub = TQ // TK
    qts = [qt_ref[0, hh, :, c * TK:(c + 1) * TK] for c in range(n_sub)]

    def scores(c, j, masked):
        s = _nn(k_ref[0, hh, j], qts[c])
        if masked:
            kpos = lax.broadcasted_iota(jnp.int32, (TK, TK), 0)
            qpos = lax.broadcasted_iota(jnp.int32, (TK, TK), 1)
            s = jnp.where((kpos >> gran_shift) <= (qpos >> gran_shift), s, NEG_INF)
        return s

    def update(s, j, carry):
        m, acc = carry
        m_new = jnp.maximum(m, jnp.max(s, axis=0, keepdims=True))
        p = jnp.exp2(s - m_new)
        acc = jnp.exp2(m - m_new) * acc + _nn(vt_ref[0, hh, j], p.astype(BF16))
        return m_new, acc

    def run(work, carries):
        carries = list(carries)
        pending = [scores(*w) for w in work[:LOOKAHEAD]]
        for i, (c, j, _) in enumerate(work):
            if i + LOOKAHEAD < len(work):
                pending.append(scores(*work[i + LOOKAHEAD]))
            carries[c] = update(pending.pop(0), j, carries[c])
        return tuple(carries)

    def steps(jj, carries):
        return run([(c, jj * n_sub + u, False) for u in range(n_sub) for c in range(n_sub)], carries)

    init = (jnp.full((1, TK), NEG_INF, F32), jnp.zeros((V_PAD, TK), F32))
    carries = lax.fori_loop(0, qi, steps, (init,) * n_sub)
    carries = run([(c, qi * n_sub + u, u == c) for c in range(n_sub) for u in range(c + 1)], carries)
    for c in range(n_sub):
        acc = carries[c][1]
        o_ref[0, pl.ds(hh * dv, dv), c * TK:(c + 1) * TK] = (acc[0:dv] / acc[dv:dv + 1]).astype(o_ref.dtype)
    return 0


def _shifted_attn_kernel(first_ref, reach_ref, qt_ref, k_ref, vt_ref, o_ref, *, gran_shift, long_steps):
    lax.fori_loop(0, qt_ref.shape[1], lambda hh, c: _shifted_attn_head(
        hh, first_ref, reach_ref, qt_ref, k_ref, vt_ref, o_ref, gran_shift=gran_shift,
        long_steps=long_steps), 0)


def _shifted_attn_head(hh, first_ref, reach_ref, qt_ref, k_ref, vt_ref, o_ref, *, gran_shift,
                       long_steps):
    qi = pl.program_id(2)
    hpb = qt_ref.shape[1]
    head = pl.program_id(1) * hpb + hh
    step = (pl.program_id(0) * pl.num_programs(1) * hpb + head) * pl.num_programs(2) + qi
    g0 = first_ref[step]
    reach = reach_ref[step]
    dv = o_ref.shape[1] // hpb
    tq = qt_ref.shape[3]
    nb = tq // TK
    qt = qt_ref[0, hh]

    def probs(j, lo, hi, masked):
        s = _nn(k_ref[0, hh, j], qt[:, lo:hi])
        if masked:
            kpos = lax.broadcasted_iota(jnp.int32, s.shape, 0)
            qpos = lax.broadcasted_iota(jnp.int32, s.shape, 1)
            s = jnp.where((kpos >> gran_shift) <= (qpos >> gran_shift), s, NEG_INF)
        return jnp.exp2(s).astype(BF16)

    def run(work, acc):
        pending = [probs(*w) for w in work[:LOOKAHEAD]]
        for i, (j, lo, hi, _) in enumerate(work):
            if i + LOOKAHEAD < len(work):
                pending.append(probs(*work[i + LOOKAHEAD]))
            pv = _nn(vt_ref[0, hh, j], pending.pop(0))
            parts = [acc[:, lo:hi] + pv]
            if lo > 0:
                parts.insert(0, acc[:, :lo])
            if hi < tq:
                parts.append(acc[:, hi:])
            acc = parts[0] if len(parts) == 1 else jnp.concatenate(parts, axis=1)
        return acc

    def full_groups(n_grp, base):
        return lambda jj, a: run([((base + jj * n_grp) * nb + u, 0, tq, False) for u in range(n_grp * nb)], a)

    n_loop = jnp.maximum(qi - 1 - g0, 0)
    acc = jnp.zeros((V_PAD, tq), F32)
    done = 0
    for n_grp in long_steps:
        n_it = (n_loop - done) // n_grp
        acc = lax.fori_loop(0, n_it, full_groups(n_grp, g0 + done), acc)
        done = done + n_it * n_grp
    diag = [(qi * nb + u, u * TK, tq, True) for u in range(nb)]

    def tail(max_reach):
        prev = [((qi - 1) * nb + u, 0, min(max_reach - (nb - 1 - u) + 1, nb) * TK, False)
                for u in range(nb) if max_reach - (nb - 1 - u) >= 0]
        return lambda a: run(prev + diag, a)

    levels = (nb - 2, nb - 1, 2 * nb - 2)
    level = jnp.where(reach <= levels[0], 0, jnp.where(reach <= levels[1], 1, 2))
    acc = lax.switch(jnp.where(qi > 0, level, len(levels)),
                     [tail(r) for r in levels] + [lambda a: run(diag, a)], acc)
    o_ref[0, pl.ds(hh * dv, dv), :] = (acc[0:dv] / acc[dv:dv + 1]).astype(o_ref.dtype)
    return 0


def _causal_attn(qt, k, vt, gran, bound, first_group, reach, tqf, long_steps):
    B, H, _, S = qt.shape
    nk, dv, hpb = k.shape[2], FOX_DIM, HEADS_PER_STEP
    assert TQ % TK == 0 and tqf % TK == 0 and MLA_V == FOX_DIM and vt.shape[3] == V_PAD and H % hpb == 0

    def call(body, tq, name, prefetch, **static):
        grid_spec = pltpu.PrefetchScalarGridSpec(
            num_scalar_prefetch=len(prefetch),
            grid=(B, H // hpb, S // tq),
            in_specs=[
                pl.BlockSpec((1, hpb, QK_PAD, tq), lambda b, h, q, *_: (b, h, 0, q)),
                pl.BlockSpec((1, hpb, nk, TK, QK_PAD), lambda b, h, q, *_: (b, h, 0, 0, 0)),
                pl.BlockSpec((1, hpb, nk, V_PAD, TK), lambda b, h, q, *_: (b, h, 0, 0, 0)),
            ],
            out_specs=pl.BlockSpec((1, hpb * dv, tq), lambda b, h, q, *_: (b, h, q)),
        )
        return pl.pallas_call(
            functools.partial(body, gran_shift=int(np.log2(gran)), **static),
            grid_spec=grid_spec,
            out_shape=jax.ShapeDtypeStruct((B, H * dv, S), BF16),
            compiler_params=pltpu.CompilerParams(
                dimension_semantics=("parallel", "parallel", "arbitrary"), vmem_limit_bytes=VMEM_LIMIT),
            name=name,
        )(*prefetch, qt, k, vt)

    return lax.cond(bound <= MAX_SHIFT_BOUND,
                    lambda: call(_shifted_attn_kernel, tqf, "shifted_attn",
                                 (first_group.reshape(-1), reach.reshape(-1)),
                                 long_steps=long_steps),
                    lambda: call(_causal_attn_kernel, TQ, "causal_attn", ()))


def _chunk_attn_kernel(qt_ref, k0_ref, k1_ref, k2_ref, v0_ref, v1_ref, v2_ref, bias_ref, o_ref, *,
                       shifted):
    t = pl.program_id(1)
    k_refs = (k0_ref, k1_ref, k2_ref)
    v_refs = (v0_ref, v1_ref, v2_ref)

    def scores(hd):
        qt = qt_ref[0, hd]
        ss = []
        for d in range(3):
            s = _nn(k_refs[d][0, hd // 2], qt) + bias_ref[hd, d * TC:(d + 1) * TC, :]
            if d < 2:
                s = jnp.where(t + (d - 2) >= 0, s, NEG_INF)
            ss.append(s)
        return ss

    def finish(hd, ss):
        if not shifted:
            m = jnp.max(jnp.maximum(jnp.maximum(ss[0], ss[1]), ss[2]), axis=0, keepdims=True)
            ss = [s - m for s in ss]
        p = jnp.concatenate([jnp.exp2(s).astype(BF16) for s in ss], axis=0)
        vt = jnp.concatenate([r[0, hd] for r in v_refs], axis=1)
        acc = _nn(vt, p)
        o_ref[0, hd * CHK_DIM:(hd + 1) * CHK_DIM, :] = (
            acc[0:CHK_DIM] / acc[CHK_DIM:CHK_DIM + 1]).astype(o_ref.dtype)

    pending = [scores(hd) for hd in range(CHUNK_LOOKAHEAD)]
    for hd in range(CHK_HEADS):
        if hd + CHUNK_LOOKAHEAD < CHK_HEADS:
            pending.append(scores(hd + CHUNK_LOOKAHEAD))
        finish(hd, pending.pop(0))


def _chunk_attn(qbt, kb, vbt, bias_t, spread):
    B, _, _, S = qbt.shape
    kspec = lambda d: pl.BlockSpec((1, CHK_HEADS // 2, TC, 2 * CHK_DIM),
                                   lambda b, t: (b, 0, jnp.maximum(t + (d - 2), 0), 0))
    vspec = lambda d: pl.BlockSpec((1, CHK_HEADS, V_PAD, TC),
                                   lambda b, t: (b, 0, 0, jnp.maximum(t + (d - 2), 0)))

    def call(shifted):
        return pl.pallas_call(
            functools.partial(_chunk_attn_kernel, shifted=shifted),
            grid=(B, S // TC),
            in_specs=[
                pl.BlockSpec((1, CHK_HEADS, 2 * CHK_DIM, TC), lambda b, t: (b, 0, 0, t)),
                kspec(0), kspec(1), kspec(2), vspec(0), vspec(1), vspec(2),
                pl.BlockSpec((CHK_HEADS, 3 * TC, TC), lambda b, t: (0, 0, 0)),
            ],
            out_specs=pl.BlockSpec((1, CHK_WIDTH, TC), lambda b, t: (b, 0, t)),
            out_shape=jax.ShapeDtypeStruct((B, CHK_WIDTH, S), BF16),
            compiler_params=pltpu.CompilerParams(
                dimension_semantics=("parallel", "parallel"), vmem_limit_bytes=VMEM_LIMIT),
            name="chunk_attn_shifted" if shifted else "chunk_attn",
        )(qbt, kb, kb, kb, vbt, vbt, vbt, bias_t)

    return lax.cond(spread <= 2 * MAX_SHIFT_BOUND, lambda: call(True), lambda: call(False))


def _out_ple_kernel(*refs, n_o):
    o_refs = refs[:n_o]
    gt_ref, h_ref, p_ref, wo_ref, pgn_ref, wgate_ref, pew_ref, out_ref = refs[n_o:]
    n_sub = TM_OUT // OUT_SUB
    subs = [slice(i * OUT_SUB, (i + 1) * OUT_SUB) for i in range(n_sub)]

    def gated(ts):
        ot = jnp.concatenate([r[0, :, ts] for r in o_refs], axis=0) if n_o > 1 else o_refs[0][0, :, ts]
        g = gt_ref[0, :, ts].astype(F32)
        return (ot.astype(F32) * (g * _sigmoid(g))).astype(BF16)

    mixed_t = [_nn(wo_ref[...], gated(ts)) for ts in subs]
    h1 = [h_ref[0, ts, :] + m.T for ts, m in zip(subs, mixed_t)]
    gate = [_nn(_rms_rows(x, pgn_ref[...]).astype(BF16), wgate_ref[...]) for x in h1]
    pe = [_nn(p_ref[0, 0, ts, :].astype(BF16), pew_ref[...]) for ts in subs]
    for ts, x, gt, e in zip(subs, h1, gate, pe):
        out_ref[0, ts, :] = x + e * _sigmoid(gt)


def _out_ple(o_list, gt, h, p, layer, wo_t, pgn, wgate, pew):
    B, S, _ = h.shape
    full = lambda shape: pl.BlockSpec(shape, lambda b, t: (0,) * len(shape))
    width = gt.shape[1]
    return pl.pallas_call(
        functools.partial(_out_ple_kernel, n_o=len(o_list)),
        grid=(B, S // TM_OUT),
        in_specs=[pl.BlockSpec((1, o.shape[1], TM_OUT), lambda b, t: (b, 0, t)) for o in o_list] + [
            pl.BlockSpec((1, width, TM_OUT), lambda b, t: (b, 0, t)),
            pl.BlockSpec((1, TM_OUT, D_MODEL), lambda b, t: (b, t, 0)),
            pl.BlockSpec((1, 1, TM_OUT, PLE_DIM), lambda b, t: (layer, b, t, 0)),
            full((D_MODEL, width)), full((1, D_MODEL)), full((D_MODEL, D_MODEL)),
            full((PLE_DIM, D_MODEL)),
        ],
        out_specs=pl.BlockSpec((1, TM_OUT, D_MODEL), lambda b, t: (b, t, 0)),
        out_shape=jax.ShapeDtypeStruct((B, S, D_MODEL), F32),
        compiler_params=pltpu.CompilerParams(
            dimension_semantics=("parallel", "parallel"), vmem_limit_bytes=VMEM_LIMIT),
        name="out_ple",
    )(*o_list, gt, h, p, wo_t, pgn, wgate, pew)


def _col(v):
    return v.astype(F32).reshape(-1, 1)


def _score_bound(qg, kg, d):
    return 1.02 * d * jnp.max(jnp.abs(qg)) * jnp.max(jnp.abs(kg)) + 1.0


def _first_live_group(cum2, tqf):
    nb = tqf // TK
    first_q = cum2[:, :, ::tqf]
    last_k = cum2[:, :, TK - 1::TK]
    dead = last_k[:, :, None, :] > first_q[:, :, :, None] + UNDERFLOW_LOG2
    return (jnp.sum(dead, axis=-1) // nb).astype(jnp.int32)


def _prev_reach(cum2, tqf):
    B, H, S = cum2.shape
    nb = tqf // TK
    first_q = cum2[:, :, ::TK].reshape(B, H, S // tqf, nb)
    last_k = jnp.roll(cum2[:, :, TK - 1::TK].reshape(B, H, S // tqf, nb), 1, axis=2)
    live = first_q[:, :, :, None, :] - last_k[:, :, :, :, None] >= -UNDERFLOW_LOG2
    gap = np.arange(nb)[None, :] + (nb - 1 - np.arange(nb))[:, None]
    return jnp.max(jnp.where(live, gap, -1), axis=(3, 4)).astype(jnp.int32)


def _chunk_bias_table(rel_bias, bound_qk):
    n = 4 * TC
    i = np.arange(n)
    dist = np.where(i < TC, i, i - n) + 2 * TC
    rb = rel_bias.astype(F32) * LOG2E
    spread = 2.0 * bound_qk + (jnp.max(rb) - jnp.min(rb))
    e = rb[:, np.clip(dist, -REL_CLIP, REL_CLIP) + REL_CLIP] - (bound_qk + jnp.max(rb))
    h = rel_bias.shape[0]
    return _bias_table(e.reshape(h, 1, n)), spread


def _bias_table(e):
    h, _, n = e.shape
    return pl.pallas_call(
        _bias_table_kernel,
        grid=(h,),
        in_specs=[pl.BlockSpec((1, 1, n), lambda hd: (hd, 0, 0))],
        out_specs=pl.BlockSpec((1, 3 * TC, TC), lambda hd: (hd, 0, 0)),
        out_shape=jax.ShapeDtypeStruct((h, 3 * TC, TC), F32),
        name="bias_table",
    )(e)


def _bias_table_kernel(e_ref, o_ref):
    n = e_ref.shape[2]
    rows = jnp.broadcast_to(e_ref[0], (3 * TC, n))
    skew = pltpu.roll(rows, 0, 1, stride=1, stride_axis=0)
    kchunk = lax.broadcasted_iota(jnp.int32, (3 * TC, TC), 0) // CHUNK
    qchunk = lax.broadcasted_iota(jnp.int32, (3 * TC, TC), 1) // CHUNK + (2 * TC) // CHUNK
    valid = (kchunk <= qchunk) & (kchunk >= qchunk - LEFT_CHUNKS)
    o_ref[0] = jnp.where(valid, skew[:, 0:TC], NEG_INF)


def kernel(x, p, positions, norm_g, ab_w_in, mla_q_norm, mla_w_uq, mla_kv_norm, mla_w_ukv, mla_q_gain, mla_k_gain, chk_q_gain, chk_k_gain, chk_rel_bias, ab_w_out, fox_w_in, fox_b_f, fox_q_gain, fox_k_gain, fox_w_out, pe_w, pe_gate_norm, pe_gate_w):
    B, S, _ = x.shape
    half = MLA_ROPE // 2
    inv_freq = 1.0 / (ROPE_THETA ** (jnp.arange(half, dtype=F32) / half))
    ang = positions.astype(F32)[:, None, :] * inv_freq[None, :, None]
    cos, sin = jnp.cos(ang), jnp.sin(ang)
    tri = (np.arange(TM)[:, None] <= np.arange(TM)[None, :]).astype(np.float32)
    tri = jnp.asarray(tri, BF16)

    h = x
    for i in range(DEPTH):
        l = i // 2
        ng = norm_g[i].astype(F32).reshape(1, -1)
        if i % 2 == 0:
            qg, kg = _col(mla_q_gain[l]) * (MLA_QK ** -0.5 * LOG2E), _col(mla_k_gain[l])
            bound = _score_bound(qg, kg, MLA_QK)
            cqg, ckg = _col(chk_q_gain[l]) * (CHK_DIM ** -0.5 * LOG2E), _col(chk_k_gain[l])
            qt, k, vt, gt, qbt, kb, vbt = _ab_proj(
                h, ng, ab_w_in[l].T.astype(BF16),
                _col(mla_q_norm[l]), mla_w_uq[l].T.astype(BF16),
                _col(mla_kv_norm[l]), mla_w_ukv[l].T.astype(BF16),
                qg, kg, cqg, ckg, cos, sin, bound.reshape(1, 1))
            dense = jnp.zeros((B, MLA_HEADS, S // TQ_DENSE), jnp.int32)
            o_a = _causal_attn(qt, k, vt, CHUNK, bound, dense, dense + 2 * (TQ_DENSE // TK), TQ_DENSE, (4, 2, 1))
            bias_t, spread = _chunk_bias_table(chk_rel_bias[l], _score_bound(cqg, ckg, CHK_DIM))
            o_b = _chunk_attn(qbt, kb, vbt, bias_t, spread)
            o_list, wo = [o_a, o_b], ab_w_out[l]
        else:
            qg, kg = _col(fox_q_gain[l]) * (FOX_DIM ** -0.5 * LOG2E), _col(fox_k_gain[l])
            bound = _score_bound(qg, kg, FOX_DIM)
            qa, ka, vt, gt, cum2 = _fox_proj(h, ng, fox_w_in[l].T.astype(BF16), _col(fox_b_f[l]), qg, kg,
                                             tri, bound.reshape(1, 1))
            o_c = _causal_attn(qa, ka, vt, 1, bound, _first_live_group(cum2, TQ_DECAY),
                               _prev_reach(cum2, TQ_DECAY), TQ_DECAY, (2, 1))
            o_list, wo = [o_c], fox_w_out[l]
        h = _out_ple(o_list, gt, h, p, i, wo.T.astype(BF16),
                     pe_gate_norm[i].astype(F32).reshape(1, -1),
                     pe_gate_w[i].astype(BF16), pe_w[i].astype(BF16))
    return h
```
